```python
import math
import jax, jax.numpy as jnp
from jax import lax
import numpy as np

D_MODEL = 1024
BATCH = 8
SEQ = 2048
DEPTH = 2

GRID_W = 64
CTX_LEN = 256
RMS_EPS = 1e-6
N_MOD = 9
D_FF = 2816
POOL_WINDOWS = (2, 4, 8, 16)
POOL_GROUPS = len(POOL_WINDOWS)
POOL_DIM = D_MODEL // 2
POOL_GROUP_DIM = POOL_DIM // POOL_GROUPS
MLA_HEADS = D_MODEL // 128
QK_NOPE_DIM = 64
QK_ROPE_DIM = 32
QK_HEAD_DIM = QK_NOPE_DIM + QK_ROPE_DIM
V_HEAD_DIM = 64
Q_LORA_RANK = 768
KV_LORA_RANK = 256
ROPE_AXIS_DIM = QK_ROPE_DIM // 2
ROPE_THETA = 10000.0
ATTN_SCALE = 1.0 / math.sqrt(QK_HEAD_DIM)
QBLOCK = 128
AB_IN_DIM = POOL_DIM + Q_LORA_RANK + KV_LORA_RANK + QK_ROPE_DIM
AB_OUT_DIM = POOL_DIM + MLA_HEADS * V_HEAD_DIM
CONV_DIM = D_MODEL
CONV_WIDTH = 3
N_EVEN = (DEPTH + 1) // 2
N_ODD = DEPTH // 2

kernel_name = 'hybrid_pool_mla_shortconv_dit_block'


def rmsnorm(x, g):
    xf = x.astype(jnp.float32)
    y = xf * lax.rsqrt(jnp.mean(xf * xf, axis=-1, keepdims=True) + RMS_EPS)
    return (y * g.astype(jnp.float32)).astype(x.dtype)


def modulation(cond, w, b):
    m = jax.nn.silu(cond) @ w + b
    return m.reshape(cond.shape[0], 1, N_MOD, cond.shape[-1])


def adaln(s, gain, m, k):
    shift, scale, gate = m[:, :, 3 * k], m[:, :, 3 * k + 1], m[:, :, 3 * k + 2]
    return rmsnorm(s, gain) * (1 + scale) + shift, gate


def swiglu(u, wg, wu, wd):
    return (jax.nn.silu(u @ wg) * (u @ wu)) @ wd


def ffn_half(s, m, k, gain, wg, wu, wd):
    u, gate = adaln(s, gain, m, k)
    return s + 0.5 * gate * swiglu(u, wg, wu, wd)


def multiscale_pool(u, w_grp, scale):
    b, l, _ = u.shape
    uf = u.astype(jnp.float32).reshape(b, l, POOL_GROUPS, POOL_GROUP_DIM)
    cs = jnp.concatenate([jnp.zeros_like(uf[:, :1]), jnp.cumsum(uf, axis=1)], axis=1)
    t = jnp.arange(l)
    outs = []
    for gi, w in enumerate(POOL_WINDOWS):
        lo = jnp.clip(t - w // 2, 0, l - 1)
        hi = jnp.clip(t + (w - w // 2 - 1), 0, l - 1)
        win_sum = cs[:, hi + 1, gi] - cs[:, lo, gi]
        cnt = (hi - lo + 1).astype(jnp.float32)[None, :, None]
        outs.append(win_sum / cnt - uf[:, :, gi])
    p = jnp.stack(outs, axis=2).astype(u.dtype)
    y = jnp.einsum('blgc,gcd->blgd', p, w_grp).reshape(b, l, POOL_DIM)
    return y * scale


def axial_rope_tables(length):
    rows = length // GRID_W
    row = jnp.repeat(jnp.arange(rows), GRID_W).astype(jnp.float32)
    col = jnp.tile(jnp.arange(GRID_W), rows).astype(jnp.float32)
    freqs = jnp.power(ROPE_THETA, -jnp.arange(0, ROPE_AXIS_DIM, 2, dtype=jnp.float32) / ROPE_AXIS_DIM)
    ang_r = (row[:, None] * freqs)[:, None, :]
    ang_c = (col[:, None] * freqs)[:, None, :]
    return (jnp.cos(ang_r), jnp.sin(ang_r), jnp.cos(ang_c), jnp.sin(ang_c))


def rot_half(z, cos, sin):
    h = z.shape[-1] // 2
    z1, z2 = z[..., :h], z[..., h:]
    return jnp.concatenate([z1 * cos - z2 * sin, z2 * cos + z1 * sin], axis=-1)


def apply_axial_rope(z, tabs):
    cos_r, sin_r, cos_c, sin_c = (tb.astype(z.dtype) for tb in tabs)
    return jnp.concatenate([rot_half(z[..., :ROPE_AXIS_DIM], cos_r, sin_r),
                            rot_half(z[..., ROPE_AXIS_DIM:], cos_c, sin_c)], axis=-1)


def mla_queries(cq, q_norm_g, w_uq):
    b, l, _ = cq.shape
    q = (rmsnorm(cq, q_norm_g) @ w_uq).reshape(b, l, MLA_HEADS, QK_HEAD_DIM)
    return q[..., :QK_NOPE_DIM], q[..., QK_NOPE_DIM:]


def mla_keys_values(ckv, kv_norm_g, w_ukv):
    b, l, _ = ckv.shape
    kv = (rmsnorm(ckv, kv_norm_g) @ w_ukv).reshape(b, l, MLA_HEADS, QK_NOPE_DIM + V_HEAD_DIM)
    return kv[..., :QK_NOPE_DIM], kv[..., QK_NOPE_DIM:]


def mla_attention(q_nope, q_rope, k_nope, k_rope, v):
    b, l, h, _ = q_nope.shape
    nb = l // QBLOCK

    def to_blocks(z):
        return z.reshape(b, nb, QBLOCK, h, z.shape[-1]).swapaxes(0, 1)

    def block(args):
        qn, qr = args
        s = (jnp.einsum('bqhd,bkhd->bhqk', qn, k_nope, preferred_element_type=jnp.float32)
             + jnp.einsum('bqhr,bkr->bhqk', qr, k_rope, preferred_element_type=jnp.float32))
        p = jax.nn.softmax(s * ATTN_SCALE, axis=-1).astype(v.dtype)
        return jnp.einsum('bhqk,bkhd->bqhd', p, v)

    o = lax.map(block, (to_blocks(q_nope), to_blocks(q_rope)))
    return o.swapaxes(0, 1).reshape(b, l, h * V_HEAD_DIM)


def pool_mla_mixer(uh, ug, tabs, w_in, pool_w, pool_scale, q_norm_g, w_uq, kv_norm_g, w_ukv, w_out, ctx_out):
    cuts = [POOL_DIM, POOL_DIM + Q_LORA_RANK, POOL_DIM + Q_LORA_RANK + KV_LORA_RANK]
    pool_h, cq_h, ckv_h, kr_h = jnp.split(uh @ w_in, cuts, axis=-1)
    pool_g, cq_g, ckv_g, kr_g = jnp.split(ug @ w_in, cuts, axis=-1)
    qn_h, qr_h = mla_queries(cq_h, q_norm_g, w_uq)
    qr_h = apply_axial_rope(qr_h, tabs)
    kn_h, v_h = mla_keys_values(ckv_h, kv_norm_g, w_ukv)
    kr_h = apply_axial_rope(kr_h[:, :, None, :], tabs)[:, :, 0]
    kn_g, v_g = mla_keys_values(ckv_g, kv_norm_g, w_ukv)
    attn_h = mla_attention(qn_h, qr_h,
                           jnp.concatenate([kn_h, kn_g], axis=1),
                           jnp.concatenate([kr_h, kr_g], axis=1),
                           jnp.concatenate([v_h, v_g], axis=1))
    out_h = jnp.concatenate([multiscale_pool(pool_h, pool_w, pool_scale), attn_h], axis=-1) @ w_out
    out_g = None
    if ctx_out:
        qn_g, qr_g = mla_queries(cq_g, q_norm_g, w_uq)
        attn_g = mla_attention(qn_g, qr_g, kn_g, kr_g, v_g)
        out_g = jnp.concatenate([multiscale_pool(pool_g, pool_w, pool_scale), attn_g], axis=-1) @ w_out
    return out_h, out_g


def short_conv_mixer(u, w_in, conv_w, w_out):
    b_gate, c_gate, val = jnp.split(u @ w_in, 3, axis=-1)
    z = c_gate * val
    z = lax.conv_general_dilated(z, conv_w[:, None, :].astype(z.dtype), window_strides=(1,),
                                 padding=((CONV_WIDTH // 2, CONV_WIDTH // 2),),
                                 dimension_numbers=('NWC', 'WIO', 'NWC'),
                                 feature_group_count=z.shape[-1])
    return (b_gate * z) @ w_out


def setup_inputs(seed: int = 0) -> dict:
    key = jax.random.key(seed)
    ks = jax.random.split(key, 24)

    def nrm(k, shape, scale):
        return jax.random.normal(k, shape, jnp.float32) * scale

    D, F = D_MODEL, D_FF
    return {
        'x': nrm(ks[0], (BATCH, SEQ, D), 1.0),
        'c': nrm(ks[1], (BATCH, D), 1.0),
        'ctx': nrm(ks[2], (BATCH, CTX_LEN, D), 1.0),
        'c_ctx': nrm(ks[3], (D,), 1.0),
        'norm_g': 1.0 + nrm(ks[4], (DEPTH, 3, D), 0.02),
        'w_mod': nrm(ks[5], (DEPTH, D, N_MOD * D), 0.5 * D ** -0.5),
        'b_mod': nrm(ks[6], (DEPTH, N_MOD * D), 0.02),
        'ffn_w_gate': nrm(ks[7], (DEPTH, 2, D, F), D ** -0.5),
        'ffn_w_up': nrm(ks[8], (DEPTH, 2, D, F), D ** -0.5),
        'ffn_w_down': nrm(ks[9], (DEPTH, 2, F, D), F ** -0.5),
        'ab_w_in': nrm(ks[10], (N_EVEN, D, AB_IN_DIM), D ** -0.5),
        'pool_w': nrm(ks[11], (N_EVEN, POOL_GROUPS, POOL_GROUP_DIM, POOL_GROUP_DIM), POOL_GROUP_DIM ** -0.5),
        'pool_scale': 1.0 + nrm(ks[12], (N_EVEN, POOL_DIM), 0.1),
        'q_norm_g': 1.0 + nrm(ks[13], (N_EVEN, Q_LORA_RANK), 0.02),
        'w_uq': nrm(ks[14], (N_EVEN, Q_LORA_RANK, MLA_HEADS * QK_HEAD_DIM), Q_LORA_RANK ** -0.5),
        'kv_norm_g': 1.0 + nrm(ks[15], (N_EVEN, KV_LORA_RANK), 0.02),
        'w_ukv': nrm(ks[16], (N_EVEN, KV_LORA_RANK, MLA_HEADS * (QK_NOPE_DIM + V_HEAD_DIM)), KV_LORA_RANK ** -0.5),
        'ab_w_out': nrm(ks[17], (N_EVEN, AB_OUT_DIM, D), AB_OUT_DIM ** -0.5),
        'conv_w_in': nrm(ks[18], (N_ODD, D, 3 * CONV_DIM), D ** -0.5),
        'conv_w': nrm(ks[19], (N_ODD, CONV_WIDTH, CONV_DIM), CONV_WIDTH ** -0.5),
        'conv_w_out': nrm(ks[20], (N_ODD, CONV_DIM, D), CONV_DIM ** -0.5),
        'final_norm_g': 1.0 + nrm(ks[21], (D,), 0.02),
    }


def reference(x, c, ctx, c_ctx, norm_g, w_mod, b_mod, ffn_w_gate, ffn_w_up, ffn_w_down,
              ab_w_in, pool_w, pool_scale, q_norm_g, w_uq, kv_norm_g, w_ukv, ab_w_out,
              conv_w_in, conv_w, conv_w_out, final_norm_g):
    tabs = axial_rope_tables(x.shape[1])
    h, g = x, ctx
    cond_g = c_ctx[None, :]
    for i in range(DEPTH):
        last = i == DEPTH - 1
        even = i % 2 == 0
        j = i // 2
        ctx_out = not last
        need_g = even or ctx_out
        m_h = modulation(c, w_mod[i], b_mod[i])
        h = ffn_half(h, m_h, 0, norm_g[i, 0], ffn_w_gate[i, 0], ffn_w_up[i, 0], ffn_w_down[i, 0])
        if need_g:
            m_g = modulation(cond_g, w_mod[i], b_mod[i])
            g = ffn_half(g, m_g, 0, norm_g[i, 0], ffn_w_gate[i, 0], ffn_w_up[i, 0], ffn_w_down[i, 0])
        uh, gate_h = adaln(h, norm_g[i, 1], m_h, 1)
        out_g = None
        if even:
            ug, gate_g = adaln(g, norm_g[i, 1], m_g, 1)
            out_h, out_g = pool_mla_mixer(uh, ug, tabs, ab_w_in[j], pool_w[j], pool_scale[j],
                                          q_norm_g[j], w_uq[j], kv_norm_g[j], w_ukv[j], ab_w_out[j], ctx_out)
        else:
            out_h = short_conv_mixer(uh, conv_w_in[j], conv_w[j], conv_w_out[j])
            if ctx_out:
                ug, gate_g = adaln(g, norm_g[i, 1], m_g, 1)
                out_g = short_conv_mixer(ug, conv_w_in[j], conv_w[j], conv_w_out[j])
        h = h + gate_h * out_h
        h = ffn_half(h, m_h, 2, norm_g[i, 2], ffn_w_gate[i, 1], ffn_w_up[i, 1], ffn_w_down[i, 1])
        if ctx_out:
            g = g + gate_g * out_g
            g = ffn_half(g, m_g, 2, norm_g[i, 2], ffn_w_gate[i, 1], ffn_w_up[i, 1], ffn_w_down[i, 1])
    return rmsnorm(h, final_norm_g)
```

```python
import functools
import math

import jax
import jax.numpy as jnp
from jax import lax
from jax.experimental import pallas as pl
from jax.experimental.pallas import tpu as pltpu

D_MODEL = 1024
SEQ = 2048
GRID_W = 64
CTX_LEN = 256
RMS_EPS = 1e-6
N_MOD = 9
D_FF = 2816
POOL_WINDOWS = (2, 4, 8, 16)
POOL_DIM = D_MODEL // 2
POOL_GROUP_DIM = POOL_DIM // len(POOL_WINDOWS)
MLA_HEADS = D_MODEL // 128
QK_NOPE_DIM = 64
QK_ROPE_DIM = 32
QK_HEAD_DIM = QK_NOPE_DIM + QK_ROPE_DIM
V_HEAD_DIM = 64
Q_LORA_RANK = 768
KV_LORA_RANK = 256
ROPE_AXIS_DIM = QK_ROPE_DIM // 2
ROPE_THETA = 10000.0
ATTN_SCALE = 1.0 / math.sqrt(QK_HEAD_DIM)
ATTN_V_DIM = MLA_HEADS * V_HEAD_DIM

LANES = 128
SUBLANES = 8
HEAD_PAD = LANES
QK_PAD_DIM = MLA_HEADS * HEAD_PAD
MOD_ROWS = 16
MOD_TILE_N = 1536
FF_CHUNK = 256
N_FF_CHUNKS = D_FF // FF_CHUNK
W_IN_PAD = 1664
HALO = SUBLANES
VMEM_LIMIT = 56 * 1024 * 1024

BF16 = jnp.bfloat16
F32 = jnp.float32


def _sigmoid(x):
    return 1.0 / (1.0 + jnp.exp(-x))


def _rms(x):
    return x * lax.rsqrt(jnp.mean(x * x, axis=-1, keepdims=True) + RMS_EPS)


def _adaln(x, gain, mod_ref, k):
    shift = mod_ref[0, 3 * k:3 * k + 1, :]
    scale = mod_ref[0, 3 * k + 1:3 * k + 2, :]
    return _rms(x) * gain * (1.0 + scale) + shift


def _params(semantics):
    return pltpu.CompilerParams(dimension_semantics=semantics, vmem_limit_bytes=VMEM_LIMIT)


def _mod_kernel(cond_ref, w_ref, b_ref, o_ref):
    cond = cond_ref[...]
    a = (cond * _sigmoid(cond)).astype(BF16)
    w = w_ref[0].astype(BF16)
    o_ref[0] = jnp.dot(a, w, preferred_element_type=F32) + b_ref[0]


def _modulation(cond, w_mod, b_mod):
    depth, d, n = w_mod.shape
    return pl.pallas_call(
        _mod_kernel,
        grid=(depth, n // MOD_TILE_N),
        in_specs=[
            pl.BlockSpec((MOD_ROWS, d), lambda i, j: (0, 0)),
            pl.BlockSpec((1, d, MOD_TILE_N), lambda i, j: (i, 0, j)),
            pl.BlockSpec((1, 1, MOD_TILE_N), lambda i, j: (i, 0, j)),
        ],
        out_specs=pl.BlockSpec((1, MOD_ROWS, MOD_TILE_N), lambda i, j: (i, 0, j)),
        out_shape=jax.ShapeDtypeStruct((depth, MOD_ROWS, n), F32),
        compiler_params=_params(("arbitrary", "arbitrary")),
        name="modulation",
    )(cond, w_mod, b_mod.reshape(depth, 1, n))


def _ffn_kernel(*refs, k, final):
    if final:
        s_ref, mod_ref, gain_ref, w1_ref, wd_ref, fg_ref, o_ref, u_scr, acc_scr = refs
    else:
        s_ref, mod_ref, gain_ref, w1_ref, wd_ref, o_ref, u_scr, acc_scr = refs
    u_scr[...] = _adaln(s_ref[...], gain_ref[...], mod_ref, k).astype(BF16)
    acc_scr[...] = jnp.zeros_like(acc_scr)

    def chunk(f, carry):
        gu = jnp.dot(u_scr[...], w1_ref[f], preferred_element_type=F32)
        g, up = gu[:, :FF_CHUNK], gu[:, FF_CHUNK:]
        a = (g * _sigmoid(g) * up).astype(BF16)
        acc_scr[...] += jnp.dot(a, wd_ref[f], preferred_element_type=F32)
        return carry

    lax.fori_loop(0, N_FF_CHUNKS, chunk, 0)
    gate = mod_ref[0, 3 * k + 2:3 * k + 3, :]
    y = s_ref[...] + 0.5 * gate * acc_scr[...]
    if final:
        y = _rms(y) * fg_ref[...]
    o_ref[...] = y


def _ffn_half(s, mod, gain, w1, wd, k, tm, final_gain=None):
    n_tok, d = s.shape
    tiles_per_batch = n_tok // mod.shape[0] // tm
    final = final_gain is not None
    in_specs = [
        pl.BlockSpec((tm, d), lambda i: (i, 0)),
        pl.BlockSpec((1, N_MOD, d), lambda i: (i // tiles_per_batch, 0, 0)),
        pl.BlockSpec((1, d), lambda i: (0, 0)),
        pl.BlockSpec(w1.shape, lambda i: (0, 0, 0), pipeline_mode=pl.Buffered(1)),
        pl.BlockSpec(wd.shape, lambda i: (0, 0, 0), pipeline_mode=pl.Buffered(1)),
    ]
    args = [s, mod, gain.reshape(1, d), w1, wd]
    if final:
        in_specs.append(pl.BlockSpec((1, d), lambda i: (0, 0)))
        args.append(final_gain.reshape(1, d))
    return pl.pallas_call(
        functools.partial(_ffn_kernel, k=k, final=final),
        grid=(n_tok // tm,),
        in_specs=in_specs,
        out_specs=pl.BlockSpec((tm, d), lambda i: (i, 0)),
        out_shape=jax.ShapeDtypeStruct((n_tok, d), F32),
        scratch_shapes=[pltpu.VMEM((tm, d), BF16), pltpu.VMEM((tm, d), F32)],
        compiler_params=_params(("arbitrary",)),
        name="ffn_half",
    )(*args)


def _rope(z, c_ref, s1_ref, s2_ref):
    fwd = pltpu.roll(z, HEAD_PAD - ROPE_AXIS_DIM // 2, axis=1)
    bwd = pltpu.roll(z, ROPE_AXIS_DIM // 2, axis=1)
    return z * c_ref[...] + fwd * s1_ref[...] + bwd * s2_ref[...]


def _mix_in_kernel(*refs, latent):
    if latent:
        (s_ref, mod_ref, gain_ref, w_in_ref, qg_ref, wq_ref, kvg_ref, wk_ref, wv_ref, place_ref,
         c_ref, s1_ref, s2_ref, pool_ref, q_ref, k_ref, v_ref) = refs
    else:
        (s_ref, mod_ref, gain_ref, w_in_ref, kvg_ref, wk_ref, wv_ref, place_ref,
         k_ref, v_ref) = refs
    u = _adaln(s_ref[...], gain_ref[...], mod_ref, 1).astype(BF16)
    z = jnp.dot(u, w_in_ref[...], preferred_element_type=F32)
    cuts = (POOL_DIM, POOL_DIM + Q_LORA_RANK, POOL_DIM + Q_LORA_RANK + KV_LORA_RANK)
    ckv = (_rms(z[:, cuts[1]:cuts[2]]) * kvg_ref[...]).astype(BF16)
    kr = pltpu.roll(z[:, cuts[2]:], QK_NOPE_DIM, axis=1)
    if latent:
        pool_ref[...] = z[:, :cuts[0]]
        cq = (_rms(z[:, cuts[0]:cuts[1]]) * qg_ref[...]).astype(BF16)
        q = jnp.dot(cq, wq_ref[...], preferred_element_type=F32)
        for h in range(MLA_HEADS):
            sl = slice(h * HEAD_PAD, (h + 1) * HEAD_PAD)
            q_ref[:, sl] = _rope(q[:, sl], c_ref, s1_ref, s2_ref).astype(BF16)
        kr = _rope(kr, c_ref, s1_ref, s2_ref)
    k = jnp.dot(ckv, wk_ref[...], preferred_element_type=F32)
    k = k + jnp.dot(kr.astype(BF16), place_ref[...], preferred_element_type=F32)
    k_ref[...] = k.astype(BF16)
    v_ref[...] = jnp.dot(ckv, wv_ref[...], preferred_element_type=F32).astype(BF16)


def _mix_in(s, mod, gain, w_in, kvg, wk, wv, place, tm, latent_args=None):
    n_tok, d = s.shape
    tiles_per_batch = n_tok // mod.shape[0] // tm
    latent = latent_args is not None
    const = lambda a: pl.BlockSpec(a.shape, lambda i: (0,) * a.ndim)
    tok = lambda w: pl.BlockSpec((tm, w), lambda i: (i, 0))
    in_specs = [tok(d), pl.BlockSpec((1, N_MOD, d), lambda i: (i // tiles_per_batch, 0, 0)),
                pl.BlockSpec((1, d), lambda i: (0, 0)), const(w_in)]
    args = [s, mod, gain.reshape(1, d), w_in]
    if latent:
        qg, wq, tabs = latent_args
        in_specs += [const(qg), const(wq)]
        args += [qg, wq]
    in_specs += [const(kvg), const(wk), const(wv), const(place)]
    args += [kvg, wk, wv, place]
    out_specs = [tok(QK_PAD_DIM), tok(ATTN_V_DIM)]
    out_shape = [jax.ShapeDtypeStruct((n_tok, QK_PAD_DIM), BF16),
                 jax.ShapeDtypeStruct((n_tok, ATTN_V_DIM), BF16)]
    if latent:
        rope_spec = pl.BlockSpec((tm, HEAD_PAD), lambda i: (i % tiles_per_batch, 0))
        in_specs += [rope_spec] * 3
        args += list(tabs)
        out_specs = [tok(POOL_DIM), tok(QK_PAD_DIM)] + out_specs
        out_shape = [jax.ShapeDtypeStruct((n_tok, POOL_DIM), F32),
                     jax.ShapeDtypeStruct((n_tok, QK_PAD_DIM), BF16)] + out_shape
    return pl.pallas_call(
        functools.partial(_mix_in_kernel, latent=latent),
        grid=(n_tok // tm,),
        in_specs=in_specs,
        out_specs=out_specs,
        out_shape=out_shape,
        compiler_params=_params(("arbitrary",)),
        name="mix_in_latent" if latent else "mix_in_context",
    )(*args)


def _attn_kernel(q_ref, kh_ref, kg_ref, vh_ref, vg_ref, o_ref):
    nt = (((1,), (1,)), ((), ()))
    outs = []
    for h in range(MLA_HEADS):
        qk = slice(h * HEAD_PAD, (h + 1) * HEAD_PAD)
        vv = slice(h * V_HEAD_DIM, (h + 1) * V_HEAD_DIM)
        q = q_ref[0, :, qk]
        s_h = lax.dot_general(q, kh_ref[0, :, qk], nt, preferred_element_type=F32) * ATTN_SCALE
        s_g = lax.dot_general(q, kg_ref[0, :, qk], nt, preferred_element_type=F32) * ATTN_SCALE
        m = jnp.maximum(jnp.max(s_h, axis=-1, keepdims=True), jnp.max(s_g, axis=-1, keepdims=True))
        e_h = jnp.exp(s_h - m)
        e_g = jnp.exp(s_g - m)
        denom = jnp.sum(e_h, axis=-1, keepdims=True) + jnp.sum(e_g, axis=-1, keepdims=True)
        o = jnp.dot(e_h.astype(BF16), vh_ref[0, :, vv], preferred_element_type=F32)
        o = o + jnp.dot(e_g.astype(BF16), vg_ref[0, :, vv], preferred_element_type=F32)
        outs.append(o / denom)
    o_ref[0] = jnp.concatenate(outs, axis=-1).astype(BF16)


def _attention(q, k_h, k_g, v_h, v_g, tq):
    b, l, _ = q.shape
    t_g = k_g.shape[1]
    return pl.pallas_call(
        _attn_kernel,
        grid=(b, l // tq),
        in_specs=[
            pl.BlockSpec((1, tq, QK_PAD_DIM), lambda i, j: (i, j, 0)),
            pl.BlockSpec((1, l, QK_PAD_DIM), lambda i, j: (i, 0, 0)),
            pl.BlockSpec((1, t_g, QK_PAD_DIM), lambda i, j: (i, 0, 0)),
            pl.BlockSpec((1, l, ATTN_V_DIM), lambda i, j: (i, 0, 0)),
            pl.BlockSpec((1, t_g, ATTN_V_DIM), lambda i, j: (i, 0, 0)),
        ],
        out_specs=pl.BlockSpec((1, tq, ATTN_V_DIM), lambda i, j: (i, j, 0)),
        out_shape=jax.ShapeDtypeStruct((b, l, ATTN_V_DIM), BF16),
        compiler_params=_params(("arbitrary", "arbitrary")),
        name="latent_attention",
    )(q, k_h, k_g, v_h, v_g)


def _mix_out_kernel(h_ref, mod_ref, pool_ref, attn_ref, pw_ref, ps_ref, wo_pool_ref, wo_attn_ref,
                    o_ref, win_scr, y_scr, *, tm, seq):
    t0 = pl.multiple_of(pl.program_id(1) * tm, tm)
    prev = pool_ref[0, pl.ds(pl.multiple_of(jnp.maximum(t0 - HALO, 0), HALO), HALO), :]
    nxt = pool_ref[0, pl.ds(pl.multiple_of(jnp.minimum(t0 + tm, seq - HALO), HALO), HALO), :]
    win_scr[0:HALO, :] = jnp.where(t0 > 0, prev, 0.0)
    win_scr[HALO:HALO + tm, :] = pool_ref[0, pl.ds(t0, tm), :]
    win_scr[HALO + tm:, :] = jnp.where(t0 + tm < seq, nxt, 0.0)
    t = t0 + lax.broadcasted_iota(jnp.int32, (tm, 1), 0)
    for g, w in enumerate(POOL_WINDOWS):
        lanes = slice(g * POOL_GROUP_DIM, (g + 1) * POOL_GROUP_DIM)
        total = win_scr[HALO - w // 2:HALO - w // 2 + tm, lanes]
        for off in range(-(w // 2) + 1, w - w // 2):
            total = total + win_scr[HALO + off:HALO + off + tm, lanes]
        lo = jnp.maximum(t - w // 2, 0)
        hi = jnp.minimum(t + (w - w // 2 - 1), seq - 1)
        cnt = (hi - lo + 1).astype(F32)
        p = total / cnt - win_scr[HALO:HALO + tm, lanes]
        y = jnp.dot(p.astype(BF16), pw_ref[g], preferred_element_type=F32)
        y_scr[:, lanes] = (y * ps_ref[:, lanes]).astype(BF16)
    out = jnp.dot(y_scr[...], wo_pool_ref[...], preferred_element_type=F32)
    out = out + jnp.dot(attn_ref[0], wo_attn_ref[...], preferred_element_type=F32)
    gate = mod_ref[0, 5:6, :]
    o_ref[0] = h_ref[0] + gate * out


def _mix_out(h, mod, pool, attn, pool_w, pool_scale, wo_pool, wo_attn, tm):
    b, l, d = h.shape
    const = lambda a: pl.BlockSpec(a.shape, lambda i, j: (0,) * a.ndim)
    return pl.pallas_call(
        functools.partial(_mix_out_kernel, tm=tm, seq=l),
        grid=(b, l // tm),
        in_specs=[
            pl.BlockSpec((1, tm, d), lambda i, j: (i, j, 0)),
            pl.BlockSpec((1, N_MOD, d), lambda i, j: (i, 0, 0)),
            pl.BlockSpec((1, l, POOL_DIM), lambda i, j: (i, 0, 0)),
            pl.BlockSpec((1, tm, ATTN_V_DIM), lambda i, j: (i, j, 0)),
            const(pool_w), const(pool_scale), const(wo_pool), const(wo_attn),
        ],
        out_specs=pl.BlockSpec((1, tm, d), lambda i, j: (i, j, 0)),
        out_shape=jax.ShapeDtypeStruct((b, l, d), F32),
        scratch_shapes=[pltpu.VMEM((tm + 2 * HALO, POOL_DIM), F32), pltpu.VMEM((tm, POOL_DIM), BF16)],
        compiler_params=_params(("arbitrary", "arbitrary")),
        name="mix_out",
    )(h, mod, pool, attn, pool_w, pool_scale, wo_pool, wo_attn)


def _conv_kernel(h_ref, hp_ref, hn_ref, mod_ref, gain_ref, w_in_ref, cw_ref, w_out_ref, o_ref,
                 u_scr, z_scr, *, tm, seq):
    t0 = pl.program_id(1) * tm
    gain = gain_ref[...]
    u_scr[0:HALO, :] = _adaln(hp_ref[0], gain, mod_ref, 1).astype(BF16)
    u_scr[HALO:HALO + tm, :] = _adaln(h_ref[0], gain, mod_ref, 1).astype(BF16)
    u_scr[HALO + tm:, :] = _adaln(hn_ref[0], gain, mod_ref, 1).astype(BF16)
    d = h_ref.shape[-1]
    cv = jnp.dot(u_scr[...], w_in_ref[:, d:], preferred_element_type=F32)
    t = t0 - HALO + lax.broadcasted_iota(jnp.int32, (tm + 2 * HALO, 1), 0)
    inside = jnp.logical_and(t >= 0, t < seq)
    z_scr[...] = jnp.where(inside, cv[:, :d] * cv[:, d:], 0.0)
    y = (cw_ref[0:1, :] * z_scr[HALO - 1:HALO - 1 + tm, :]
         + cw_ref[1:2, :] * z_scr[HALO:HALO + tm, :]
         + cw_ref[2:3, :] * z_scr[HALO + 1:HALO + 1 + tm, :])
    bg = jnp.dot(u_scr[HALO:HALO + tm, :], w_in_ref[:, :d], preferred_element_type=F32)
    out = jnp.dot((bg * y).astype(BF16), w_out_ref[...], preferred_element_type=F32)
    gate = mod_ref[0, 5:6, :]
    o_ref[0] = h_ref[0] + gate * out


def _conv_mixer(h, mod, gain, w_in, conv_w, w_out, tm):
    b, l, d = h.shape
    hb = tm // HALO
    const = lambda a: pl.BlockSpec(a.shape, lambda i, j: (0,) * a.ndim)
    return pl.pallas_call(
        functools.partial(_conv_kernel, tm=tm, seq=l),
        grid=(b, l // tm),
        in_specs=[
            pl.BlockSpec((1, tm, d), lambda i, j: (i, j, 0)),
            pl.BlockSpec((1, HALO, d), lambda i, j: (i, jnp.maximum(j * hb - 1, 0), 0)),
            pl.BlockSpec((1, HALO, d), lambda i, j: (i, jnp.minimum((j + 1) * hb, l // HALO - 1), 0)),
            pl.BlockSpec((1, N_MOD, d), lambda i, j: (i, 0, 0)),
            pl.BlockSpec((1, d), lambda i, j: (0, 0)),
            const(w_in), const(conv_w), const(w_out),
        ],
        out_specs=pl.BlockSpec((1, tm, d), lambda i, j: (i, j, 0)),
        out_shape=jax.ShapeDtypeStruct((b, l, d), F32),
        scratch_shapes=[pltpu.VMEM((tm + 2 * HALO, d), BF16), pltpu.VMEM((tm + 2 * HALO, d), F32)],
        compiler_params=_params(("arbitrary", "arbitrary")),
        name="conv_mixer",
    )(h, h, h, mod, gain.reshape(1, d), w_in, conv_w, w_out)


def _ffn_weights(wg, wu, wd):
    d = wg.shape[0]
    split = lambda w: w.reshape(d, N_FF_CHUNKS, FF_CHUNK).transpose(1, 0, 2)
    w1 = jnp.concatenate([split(wg), split(wu)], axis=-1).astype(BF16)
    return w1, wd.reshape(N_FF_CHUNKS, FF_CHUNK, d).astype(BF16)


def _head_pad(w, per_head, start, width):
    r = w.shape[0]
    w = w.reshape(r, MLA_HEADS, per_head)[:, :, start:start + width]
    return jnp.pad(w, ((0, 0), (0, 0), (0, HEAD_PAD - width))).reshape(r, QK_PAD_DIM)


def _rope_tables(length):
    pos = jnp.arange(length)
    row = (pos // GRID_W).astype(F32)
    col = (pos % GRID_W).astype(F32)
    half = ROPE_AXIS_DIM // 2
    freqs = jnp.power(ROPE_THETA, -jnp.arange(0, ROPE_AXIS_DIM, 2, dtype=F32) / ROPE_AXIS_DIM)
    lane = jnp.arange(HEAD_PAD)
    o = lane - QK_NOPE_DIM
    rotary = jnp.logical_and(o >= 0, o < QK_ROPE_DIM)
    o = jnp.clip(o, 0, QK_ROPE_DIM - 1)
    ang = jnp.where((o // ROPE_AXIS_DIM == 0)[None, :], row[:, None], col[:, None]) * freqs[o % half][None, :]
    first = (o % ROPE_AXIS_DIM) < half
    cos = jnp.where(rotary[None, :], jnp.cos(ang), 1.0)
    sin = jnp.where(rotary[None, :], jnp.sin(ang), 0.0)
    return cos, jnp.where(first[None, :], -sin, 0.0), jnp.where(first[None, :], 0.0, sin)


def kernel(x, c, ctx, c_ctx, norm_g, w_mod, b_mod, ffn_w_gate, ffn_w_up, ffn_w_down, ab_w_in, pool_w,
           pool_scale, q_norm_g, w_uq, kv_norm_g, w_ukv, ab_w_out, conv_w_in, conv_w, conv_w_out,
           final_norm_g):
    b, l, d = x.shape
    t_g = ctx.shape[1]
    tm_h, tm_g, tq = 512, 256, 256

    cond = jnp.zeros((MOD_ROWS, d), F32).at[:b].set(c).at[b].set(c_ctx)
    m = _modulation(cond, w_mod, b_mod)
    mod_h = [m[i, :b].reshape(b, N_MOD, d) for i in range(2)]
    mod_g = jnp.broadcast_to(m[0, b].reshape(1, N_MOD, d), (b, N_MOD, d))

    ffn_w = [[_ffn_weights(ffn_w_gate[i, j], ffn_w_up[i, j], ffn_w_down[i, j]) for j in range(2)]
             for i in range(2)]

    h = x.reshape(b * l, d)
    g = ctx.reshape(b * t_g, d)

    h = _ffn_half(h, mod_h[0], norm_g[0, 0], *ffn_w[0][0], k=0, tm=tm_h)
    g = _ffn_half(g, mod_g, norm_g[0, 0], *ffn_w[0][0], k=0, tm=tm_g)

    w_in = jnp.pad(ab_w_in[0], ((0, 0), (0, W_IN_PAD - ab_w_in.shape[-1]))).astype(BF16)
    wq = _head_pad(w_uq[0], QK_HEAD_DIM, 0, QK_HEAD_DIM).astype(BF16)
    wk = _head_pad(w_ukv[0], QK_NOPE_DIM + V_HEAD_DIM, 0, QK_NOPE_DIM).astype(BF16)
    wv = w_ukv[0].reshape(KV_LORA_RANK, MLA_HEADS, QK_NOPE_DIM + V_HEAD_DIM)[:, :, QK_NOPE_DIM:]
    wv = wv.reshape(KV_LORA_RANK, ATTN_V_DIM).astype(BF16)
    lane = jnp.arange(HEAD_PAD)
    rotary = jnp.logical_and(lane >= QK_NOPE_DIM, lane < QK_HEAD_DIM)
    place = jnp.tile(jnp.where(rotary[:, None], jnp.eye(HEAD_PAD, dtype=F32), 0.0), (1, MLA_HEADS)).astype(BF16)
    qg = q_norm_g[0].reshape(1, Q_LORA_RANK)
    kvg = kv_norm_g[0].reshape(1, KV_LORA_RANK)
    tabs = _rope_tables(l)

    pool, q, k_h, v_h = _mix_in(h, mod_h[0], norm_g[0, 1], w_in, kvg, wk, wv, place, tm_h,
                                latent_args=(qg, wq, tabs))
    k_g, v_g = _mix_in(g, mod_g, norm_g[0, 1], w_in, kvg, wk, wv, place, tm_g)
    attn = _attention(q.reshape(b, l, -1), k_h.reshape(b, l, -1), k_g.reshape(b, t_g, -1),
                      v_h.reshape(b, l, -1), v_g.reshape(b, t_g, -1), tq)
    wo = ab_w_out[0].astype(BF16)
    h = _mix_out(h.reshape(b, l, d), mod_h[0], pool.reshape(b, l, -1), attn, pool_w[0].astype(BF16),
                 pool_scale[0].reshape(1, POOL_DIM), wo[:POOL_DIM], wo[POOL_DIM:], tm_h)
    h = _ffn_half(h.reshape(b * l, d), mod_h[0], norm_g[0, 2], *ffn_w[0][1], k=2, tm=tm_h)

    h = _ffn_half(h, mod_h[1], norm_g[1, 0], *ffn_w[1][0], k=0, tm=tm_h)
    h = _conv_mixer(h.reshape(b, l, d), mod_h[1], norm_g[1, 1], conv_w_in[0].astype(BF16), conv_w[0],
                    conv_w_out[0].astype(BF16), tm_h)
    h = _ffn_half(h.reshape(b * l, d), mod_h[1], norm_g[1, 2], *ffn_w[1][1], k=2, tm=tm_h,
                  final_gain=final_norm_g)
    return h.reshape(b, l, d)
```

```python
import functools
import math

import jax
import jax.numpy as jnp
from jax import lax
from jax.experimental import pallas as pl
from jax.experimental.pallas import tpu as pltpu

D_MODEL = 1024
SEQ = 2048
GRID_W = 64
CTX_LEN = 256
RMS_EPS = 1e-6
N_MOD = 9
D_FF = 2816
POOL_WINDOWS = (2, 4, 8, 16)
POOL_DIM = D_MODEL // 2
POOL_GROUP_DIM = POOL_DIM // len(POOL_WINDOWS)
MLA_HEADS = D_MODEL // 128
QK_NOPE_DIM = 64
QK_ROPE_DIM = 32
QK_HEAD_DIM = QK_NOPE_DIM + QK_ROPE_DIM
V_HEAD_DIM = 64
Q_LORA_RANK = 768
KV_LORA_RANK = 256
ROPE_AXIS_DIM = QK_ROPE_DIM // 2
ROPE_THETA = 10000.0
ATTN_SCALE = 1.0 / math.sqrt(QK_HEAD_DIM)
ATTN_V_DIM = MLA_HEADS * V_HEAD_DIM

LANES = 128
SUBLANES = 8
HEAD_PAD = LANES
QK_PAD_DIM = MLA_HEADS * HEAD_PAD
MOD_ROWS = 16
MOD_TILE_N = 1536
FF_CHUNK = 256
N_FF_CHUNKS = D_FF // FF_CHUNK
W_IN_PAD = 1664
HALO = SUBLANES
VMEM_LIMIT = 56 * 1024 * 1024

BF16 = jnp.bfloat16
F32 = jnp.float32


def _sigmoid(x):
    return 1.0 / (1.0 + jnp.exp(-x))


def _rms(x):
    return x * lax.rsqrt(jnp.mean(x * x, axis=-1, keepdims=True) + RMS_EPS)


def _adaln(x, gain, mod_ref, k):
    shift = mod_ref[0, 3 * k:3 * k + 1, :]
    scale = mod_ref[0, 3 * k + 1:3 * k + 2, :]
    return _rms(x) * gain * (1.0 + scale) + shift


def _params(semantics):
    return pltpu.CompilerParams(dimension_semantics=semantics, vmem_limit_bytes=VMEM_LIMIT)


def _mod_kernel(cond_ref, w_ref, b_ref, o_ref):
    cond = cond_ref[...]
    a = (cond * _sigmoid(cond)).astype(BF16)
    w = w_ref[0].astype(BF16)
    o_ref[0] = jnp.dot(a, w, preferred_element_type=F32) + b_ref[0]


def _modulation(cond, w_mod, b_mod):
    depth, d, n = w_mod.shape
    return pl.pallas_call(
        _mod_kernel,
        grid=(depth, n // MOD_TILE_N),
        in_specs=[
            pl.BlockSpec((MOD_ROWS, d), lambda i, j: (0, 0)),
            pl.BlockSpec((1, d, MOD_TILE_N), lambda i, j: (i, 0, j)),
            pl.BlockSpec((1, 1, MOD_TILE_N), lambda i, j: (i, 0, j)),
        ],
        out_specs=pl.BlockSpec((1, MOD_ROWS, MOD_TILE_N), lambda i, j: (i, 0, j)),
        out_shape=jax.ShapeDtypeStruct((depth, MOD_ROWS, n), F32),
        compiler_params=_params(("arbitrary", "arbitrary")),
        name="modulation",
    )(cond, w_mod, b_mod.reshape(depth, 1, n))


def _ffn_kernel(*refs, k, final):
    if final:
        s_ref, mod_ref, gain_ref, wg_ref, wu_ref, wd_ref, fg_ref, o_ref, u_scr, a_scr = refs
    else:
        s_ref, mod_ref, gain_ref, wg_ref, wu_ref, wd_ref, o_ref, u_scr, a_scr = refs
    u_scr[...] = _adaln(s_ref[...], gain_ref[...], mod_ref, k).astype(BF16)
    for f in range(N_FF_CHUNKS):
        cols = slice(f * FF_CHUNK, (f + 1) * FF_CHUNK)
        g = jnp.dot(u_scr[...], wg_ref[:, cols], preferred_element_type=F32)
        up = jnp.dot(u_scr[...], wu_ref[:, cols], preferred_element_type=F32)
        a_scr[:, cols] = (g * _sigmoid(g) * up).astype(BF16)
    out = jnp.dot(a_scr[...], wd_ref[...], preferred_element_type=F32)
    gate = mod_ref[0, 3 * k + 2:3 * k + 3, :]
    y = s_ref[...] + 0.5 * gate * out
    if final:
        y = _rms(y) * fg_ref[...]
    o_ref[...] = y


def _ffn_half(s, mod, gain, wg, wu, wd, k, tm, final_gain=None):
    n_tok, d = s.shape
    tiles_per_batch = n_tok // mod.shape[0] // tm
    final = final_gain is not None
    resident = lambda w: pl.BlockSpec(w.shape, lambda i: (0, 0), pipeline_mode=pl.Buffered(1))
    in_specs = [
        pl.BlockSpec((tm, d), lambda i: (i, 0)),
        pl.BlockSpec((1, N_MOD, d), lambda i: (i // tiles_per_batch, 0, 0)),
        pl.BlockSpec((1, d), lambda i: (0, 0)),
        resident(wg), resident(wu), resident(wd),
    ]
    args = [s, mod, gain.reshape(1, d), wg, wu, wd]
    if final:
        in_specs.append(pl.BlockSpec((1, d), lambda i: (0, 0)))
        args.append(final_gain.reshape(1, d))
    return pl.pallas_call(
        functools.partial(_ffn_kernel, k=k, final=final),
        grid=(n_tok // tm,),
        in_specs=in_specs,
        out_specs=pl.BlockSpec((tm, d), lambda i: (i, 0)),
        out_shape=jax.ShapeDtypeStruct((n_tok, d), F32),
        scratch_shapes=[pltpu.VMEM((tm, d), BF16), pltpu.VMEM((tm, D_FF), BF16)],
        compiler_params=_params(("arbitrary",)),
        name="ffn_half",
    )(*args)


def _rope(z, c_ref, s1_ref, s2_ref):
    fwd = pltpu.roll(z, HEAD_PAD - ROPE_AXIS_DIM // 2, axis=1)
    bwd = pltpu.roll(z, ROPE_AXIS_DIM // 2, axis=1)
    return z * c_ref[...] + fwd * s1_ref[...] + bwd * s2_ref[...]


def _mix_in_kernel(*refs, latent):
    if latent:
        (s_ref, mod_ref, gain_ref, w_in_ref, qg_ref, wq_ref, kvg_ref, wk_ref, wv_ref, place_ref,
         c_ref, s1_ref, s2_ref, pool_ref, q_ref, k_ref, v_ref) = refs
    else:
        (s_ref, mod_ref, gain_ref, w_in_ref, kvg_ref, wk_ref, wv_ref, place_ref,
         k_ref, v_ref) = refs
    u = _adaln(s_ref[...], gain_ref[...], mod_ref, 1).astype(BF16)
    z = jnp.dot(u, w_in_ref[...], preferred_element_type=F32)
    cuts = (POOL_DIM, POOL_DIM + Q_LORA_RANK, POOL_DIM + Q_LORA_RANK + KV_LORA_RANK)
    ckv = (_rms(z[:, cuts[1]:cuts[2]]) * kvg_ref[...]).astype(BF16)
    kr = pltpu.roll(z[:, cuts[2]:], QK_NOPE_DIM, axis=1)
    if latent:
        pool_ref[...] = z[:, :cuts[0]]
        cq = (_rms(z[:, cuts[0]:cuts[1]]) * qg_ref[...]).astype(BF16)
        q = jnp.dot(cq, wq_ref[...], preferred_element_type=F32)
        for h in range(MLA_HEADS):
            sl = slice(h * HEAD_PAD, (h + 1) * HEAD_PAD)
            q_ref[:, sl] = _rope(q[:, sl], c_ref, s1_ref, s2_ref).astype(BF16)
        kr = _rope(kr, c_ref, s1_ref, s2_ref)
    k = jnp.dot(ckv, wk_ref[...], preferred_element_type=F32)
    k = k + jnp.dot(kr.astype(BF16), place_ref[...], preferred_element_type=F32)
    k_ref[...] = k.astype(BF16)
    v_ref[...] = jnp.dot(ckv, wv_ref[...], preferred_element_type=F32).astype(BF16)


def _mix_in(s, mod, gain, w_in, kvg, wk, wv, place, tm, latent_args=None):
    n_tok, d = s.shape
    tiles_per_batch = n_tok // mod.shape[0] // tm
    latent = latent_args is not None
    const = lambda a: pl.BlockSpec(a.shape, lambda i: (0,) * a.ndim)
    tok = lambda w: pl.BlockSpec((tm, w), lambda i: (i, 0))
    in_specs = [tok(d), pl.BlockSpec((1, N_MOD, d), lambda i: (i // tiles_per_batch, 0, 0)),
                pl.BlockSpec((1, d), lambda i: (0, 0)), const(w_in)]
    args = [s, mod, gain.reshape(1, d), w_in]
    if latent:
        qg, wq, tabs = latent_args
        in_specs += [const(qg), const(wq)]
        args += [qg, wq]
    in_specs += [const(kvg), const(wk), const(wv), const(place)]
    args += [kvg, wk, wv, place]
    out_specs = [tok(QK_PAD_DIM), tok(ATTN_V_DIM)]
    out_shape = [jax.ShapeDtypeStruct((n_tok, QK_PAD_DIM), BF16),
                 jax.ShapeDtypeStruct((n_tok, ATTN_V_DIM), BF16)]
    if latent:
        rope_spec = pl.BlockSpec((tm, HEAD_PAD), lambda i: (i % tiles_per_batch, 0))
        in_specs += [rope_spec] * 3
        args += list(tabs)
        out_specs = [tok(POOL_DIM), tok(QK_PAD_DIM)] + out_specs
        out_shape = [jax.ShapeDtypeStruct((n_tok, POOL_DIM), F32),
                     jax.ShapeDtypeStruct((n_tok, QK_PAD_DIM), BF16)] + out_shape
    return pl.pallas_call(
        functools.partial(_mix_in_kernel, latent=latent),
        grid=(n_tok // tm,),
        in_specs=in_specs,
        out_specs=out_specs,
        out_shape=out_shape,
        compiler_params=_params(("arbitrary",)),
        name="mix_in_latent" if latent else "mix_in_context",
    )(*args)


def _attn_kernel(q_ref, kh_ref, kg_ref, vh_ref, vg_ref, o_ref):
    nt = (((1,), (1,)), ((), ()))
    outs = []
    for h in range(MLA_HEADS):
        qk = slice(h * HEAD_PAD, (h + 1) * HEAD_PAD)
        vv = slice(h * V_HEAD_DIM, (h + 1) * V_HEAD_DIM)
        q = q_ref[0, :, qk]
        s_h = lax.dot_general(q, kh_ref[0, :, qk], nt, preferred_element_type=F32) * ATTN_SCALE
        s_g = lax.dot_general(q, kg_ref[0, :, qk], nt, preferred_element_type=F32) * ATTN_SCALE
        m = jnp.maximum(jnp.max(s_h, axis=-1, keepdims=True), jnp.max(s_g, axis=-1, keepdims=True))
        e_h = jnp.exp(s_h - m)
        e_g = jnp.exp(s_g - m)
        denom = jnp.sum(e_h, axis=-1, keepdims=True) + jnp.sum(e_g, axis=-1, keepdims=True)
        o = jnp.dot(e_h.astype(BF16), vh_ref[0, :, vv], preferred_element_type=F32)
        o = o + jnp.dot(e_g.astype(BF16), vg_ref[0, :, vv], preferred_element_type=F32)
        outs.append(o / denom)
    o_ref[0] = jnp.concatenate(outs, axis=-1).astype(BF16)


def _attention(q, k_h, k_g, v_h, v_g, tq):
    b, l, _ = q.shape
    t_g = k_g.shape[1]
    return pl.pallas_call(
        _attn_kernel,
        grid=(b, l // tq),
        in_specs=[
            pl.BlockSpec((1, tq, QK_PAD_DIM), lambda i, j: (i, j, 0)),
            pl.BlockSpec((1, l, QK_PAD_DIM), lambda i, j: (i, 0, 0)),
            pl.BlockSpec((1, t_g, QK_PAD_DIM), lambda i, j: (i, 0, 0)),
            pl.BlockSpec((1, l, ATTN_V_DIM), lambda i, j: (i, 0, 0)),
            pl.BlockSpec((1, t_g, ATTN_V_DIM), lambda i, j: (i, 0, 0)),
        ],
        out_specs=pl.BlockSpec((1, tq, ATTN_V_DIM), lambda i, j: (i, j, 0)),
        out_shape=jax.ShapeDtypeStruct((b, l, ATTN_V_DIM), BF16),
        compiler_params=_params(("arbitrary", "arbitrary")),
        name="latent_attention",
    )(q, k_h, k_g, v_h, v_g)


def _mix_out_kernel(h_ref, mod_ref, pool_ref, attn_ref, pw_ref, ps_ref, wo_pool_ref, wo_attn_ref,
                    o_ref, win_scr, y_scr, *, tm, seq):
    t0 = pl.multiple_of(pl.program_id(1) * tm, tm)
    prev = pool_ref[0, pl.ds(pl.multiple_of(jnp.maximum(t0 - HALO, 0), HALO), HALO), :]
    nxt = pool_ref[0, pl.ds(pl.multiple_of(jnp.minimum(t0 + tm, seq - HALO), HALO), HALO), :]
    win_scr[0:HALO, :] = jnp.where(t0 > 0, prev, 0.0)
    win_scr[HALO:HALO + tm, :] = pool_ref[0, pl.ds(t0, tm), :]
    win_scr[HALO + tm:, :] = jnp.where(t0 + tm < seq, nxt, 0.0)
    t = t0 + lax.broadcasted_iota(jnp.int32, (tm, 1), 0)
    for g, w in enumerate(POOL_WINDOWS):
        lanes = slice(g * POOL_GROUP_DIM, (g + 1) * POOL_GROUP_DIM)
        total = win_scr[HALO - w // 2:HALO - w // 2 + tm, lanes]
        for off in range(-(w // 2) + 1, w - w // 2):
            total = total + win_scr[HALO + off:HALO + off + tm, lanes]
        lo = jnp.maximum(t - w // 2, 0)
        hi = jnp.minimum(t + (w - w // 2 - 1), seq - 1)
        cnt = (hi - lo + 1).astype(F32)
        p = total / cnt - win_scr[HALO:HALO + tm, lanes]
        y = jnp.dot(p.astype(BF16), pw_ref[g], preferred_element_type=F32)
        y_scr[:, lanes] = (y * ps_ref[:, lanes]).astype(BF16)
    out = jnp.dot(y_scr[...], wo_pool_ref[...], preferred_element_type=F32)
    out = out + jnp.dot(attn_ref[0], wo_attn_ref[...], preferred_element_type=F32)
    gate = mod_ref[0, 5:6, :]
    o_ref[0] = h_ref[0] + gate * out


def _mix_out(h, mod, pool, attn, pool_w, pool_scale, wo_pool, wo_attn, tm):
    b, l, d = h.shape
    const = lambda a: pl.BlockSpec(a.shape, lambda i, j: (0,) * a.ndim)
    return pl.pallas_call(
        functools.partial(_mix_out_kernel, tm=tm, seq=l),
        grid=(b, l // tm),
        in_specs=[
            pl.BlockSpec((1, tm, d), lambda i, j: (i, j, 0)),
            pl.BlockSpec((1, N_MOD, d), lambda i, j: (i, 0, 0)),
            pl.BlockSpec((1, l, POOL_DIM), lambda i, j: (i, 0, 0)),
            pl.BlockSpec((1, tm, ATTN_V_DIM), lambda i, j: (i, j, 0)),
            const(pool_w), const(pool_scale), const(wo_pool), const(wo_attn),
        ],
        out_specs=pl.BlockSpec((1, tm, d), lambda i, j: (i, j, 0)),
        out_shape=jax.ShapeDtypeStruct((b, l, d), F32),
        scratch_shapes=[pltpu.VMEM((tm + 2 * HALO, POOL_DIM), F32), pltpu.VMEM((tm, POOL_DIM), BF16)],
        compiler_params=_params(("arbitrary", "arbitrary")),
        name="mix_out",
    )(h, mod, pool, attn, pool_w, pool_scale, wo_pool, wo_attn)


def _conv_kernel(h_ref, hp_ref, hn_ref, mod_ref, gain_ref, w_in_ref, cw_ref, w_out_ref, o_ref,
                 u_scr, z_scr, *, tm, seq):
    t0 = pl.program_id(1) * tm
    gain = gain_ref[...]
    u_scr[0:HALO, :] = _adaln(hp_ref[0], gain, mod_ref, 1).astype(BF16)
    u_scr[HALO:HALO + tm, :] = _adaln(h_ref[0], gain, mod_ref, 1).astype(BF16)
    u_scr[HALO + tm:, :] = _adaln(hn_ref[0], gain, mod_ref, 1).astype(BF16)
    d = h_ref.shape[-1]
    cv = jnp.dot(u_scr[...], w_in_ref[:, d:], preferred_element_type=F32)
    t = t0 - HALO + lax.broadcasted_iota(jnp.int32, (tm + 2 * HALO, 1), 0)
    inside = jnp.logical_and(t >= 0, t < seq)
    z_scr[...] = jnp.where(inside, cv[:, :d] * cv[:, d:], 0.0)
    y = (cw_ref[0:1, :] * z_scr[HALO - 1:HALO - 1 + tm, :]
         + cw_ref[1:2, :] * z_scr[HALO:HALO + tm, :]
         + cw_ref[2:3, :] * z_scr[HALO + 1:HALO + 1 + tm, :])
    bg = jnp.dot(u_scr[HALO:HALO + tm, :], w_in_ref[:, :d], preferred_element_type=F32)
    out = jnp.dot((bg * y).astype(BF16), w_out_ref[...], preferred_element_type=F32)
    gate = mod_ref[0, 5:6, :]
    o_ref[0] = h_ref[0] + gate * out


def _conv_mixer(h, mod, gain, w_in, conv_w, w_out, tm):
    b, l, d = h.shape
    hb = tm // HALO
    const = lambda a: pl.BlockSpec(a.shape, lambda i, j: (0,) * a.ndim)
    return pl.pallas_call(
        functools.partial(_conv_kernel, tm=tm, seq=l),
        grid=(b, l // tm),
        in_specs=[
            pl.BlockSpec((1, tm, d), lambda i, j: (i, j, 0)),
            pl.BlockSpec((1, HALO, d), lambda i, j: (i, jnp.maximum(j * hb - 1, 0), 0)),
            pl.BlockSpec((1, HALO, d), lambda i, j: (i, jnp.minimum((j + 1) * hb, l // HALO - 1), 0)),
            pl.BlockSpec((1, N_MOD, d), lambda i, j: (i, 0, 0)),
            pl.BlockSpec((1, d), lambda i, j: (0, 0)),
            const(w_in), const(conv_w), const(w_out),
        ],
        out_specs=pl.BlockSpec((1, tm, d), lambda i, j: (i, j, 0)),
        out_shape=jax.ShapeDtypeStruct((b, l, d), F32),
        scratch_shapes=[pltpu.VMEM((tm + 2 * HALO, d), BF16), pltpu.VMEM((tm + 2 * HALO, d), F32)],
        compiler_params=_params(("arbitrary", "arbitrary")),
        name="conv_mixer",
    )(h, h, h, mod, gain.reshape(1, d), w_in, conv_w, w_out)


def _ffn_weights(wg, wu, wd):
    return wg.astype(BF16), wu.astype(BF16), wd.astype(BF16)


def _head_pad(w, per_head, start, width):
    r = w.shape[0]
    w = w.reshape(r, MLA_HEADS, per_head)[:, :, start:start + width]
    return jnp.pad(w, ((0, 0), (0, 0), (0, HEAD_PAD - width))).reshape(r, QK_PAD_DIM)


def _rope_tables(length):
    pos = jnp.arange(length)
    row = (pos // GRID_W).astype(F32)
    col = (pos % GRID_W).astype(F32)
    half = ROPE_AXIS_DIM // 2
    freqs = jnp.power(ROPE_THETA, -jnp.arange(0, ROPE_AXIS_DIM, 2, dtype=F32) / ROPE_AXIS_DIM)
    lane = jnp.arange(HEAD_PAD)
    o = lane - QK_NOPE_DIM
    rotary = jnp.logical_and(o >= 0, o < QK_ROPE_DIM)
    o = jnp.clip(o, 0, QK_ROPE_DIM - 1)
    ang = jnp.where((o // ROPE_AXIS_DIM == 0)[None, :], row[:, None], col[:, None]) * freqs[o % half][None, :]
    first = (o % ROPE_AXIS_DIM) < half
    cos = jnp.where(rotary[None, :], jnp.cos(ang), 1.0)
    sin = jnp.where(rotary[None, :], jnp.sin(ang), 0.0)
    return cos, jnp.where(first[None, :], -sin, 0.0), jnp.where(first[None, :], 0.0, sin)


def kernel(x, c, ctx, c_ctx, norm_g, w_mod, b_mod, ffn_w_gate, ffn_w_up, ffn_w_down, ab_w_in, pool_w,
           pool_scale, q_norm_g, w_uq, kv_norm_g, w_ukv, ab_w_out, conv_w_in, conv_w, conv_w_out,
           final_norm_g):
    b, l, d = x.shape
    t_g = ctx.shape[1]
    tm_h, tm_g, tq = 512, 256, 256

    cond = jnp.zeros((MOD_ROWS, d), F32).at[:b].set(c).at[b].set(c_ctx)
    m = _modulation(cond, w_mod, b_mod)
    mod_h = [m[i, :b].reshape(b, N_MOD, d) for i in range(2)]
    mod_g = jnp.broadcast_to(m[0, b].reshape(1, N_MOD, d), (b, N_MOD, d))

    ffn_w = [[_ffn_weights(ffn_w_gate[i, j], ffn_w_up[i, j], ffn_w_down[i, j]) for j in range(2)]
             for i in range(2)]

    h = x.reshape(b * l, d)
    g = ctx.reshape(b * t_g, d)

    h = _ffn_half(h, mod_h[0], norm_g[0, 0], *ffn_w[0][0], k=0, tm=tm_h)
    g = _ffn_half(g, mod_g, norm_g[0, 0], *ffn_w[0][0], k=0, tm=tm_g)

    w_in = jnp.pad(ab_w_in[0], ((0, 0), (0, W_IN_PAD - ab_w_in.shape[-1]))).astype(BF16)
    wq = _head_pad(w_uq[0], QK_HEAD_DIM, 0, QK_HEAD_DIM).astype(BF16)
    wk = _head_pad(w_ukv[0], QK_NOPE_DIM + V_HEAD_DIM, 0, QK_NOPE_DIM).astype(BF16)
    wv = w_ukv[0].reshape(KV_LORA_RANK, MLA_HEADS, QK_NOPE_DIM + V_HEAD_DIM)[:, :, QK_NOPE_DIM:]
    wv = wv.reshape(KV_LORA_RANK, ATTN_V_DIM).astype(BF16)
    lane = jnp.arange(HEAD_PAD)
    rotary = jnp.logical_and(lane >= QK_NOPE_DIM, lane < QK_HEAD_DIM)
    place = jnp.tile(jnp.where(rotary[:, None], jnp.eye(HEAD_PAD, dtype=F32), 0.0), (1, MLA_HEADS)).astype(BF16)
    qg = q_norm_g[0].reshape(1, Q_LORA_RANK)
    kvg = kv_norm_g[0].reshape(1, KV_LORA_RANK)
    tabs = _rope_tables(l)

    pool, q, k_h, v_h = _mix_in(h, mod_h[0], norm_g[0, 1], w_in, kvg, wk, wv, place, tm_h,
                                latent_args=(qg, wq, tabs))
    k_g, v_g = _mix_in(g, mod_g, norm_g[0, 1], w_in, kvg, wk, wv, place, tm_g)
    attn = _attention(q.reshape(b, l, -1), k_h.reshape(b, l, -1), k_g.reshape(b, t_g, -1),
                      v_h.reshape(b, l, -1), v_g.reshape(b, t_g, -1), tq)
    wo = ab_w_out[0].astype(BF16)
    h = _mix_out(h.reshape(b, l, d), mod_h[0], pool.reshape(b, l, -1), attn, pool_w[0].astype(BF16),
                 pool_scale[0].reshape(1, POOL_DIM), wo[:POOL_DIM], wo[POOL_DIM:], tm_h)
    h = _ffn_half(h.reshape(b * l, d), mod_h[0], norm_g[0, 2], *ffn_w[0][1], k=2, tm=tm_h)

    h = _ffn_half(h, mod_h[1], norm_g[1, 0], *ffn_w[1][0], k=0, tm=tm_h)
    h = _conv_mixer(h.reshape(b, l, d), mod_h[1], norm_g[1, 1], conv_w_in[0].astype(BF16), conv_w[0],
                    conv_w_out[0].astype(BF16), tm_h)
    h = _ffn_half(h.reshape(b * l, d), mod_h[1], norm_g[1, 2], *ffn_w[1][1], k=2, tm=tm_h,
                  final_gain=final_norm_g)
    return h.reshape(b, l, d)
```

```python
import functools
import math

import jax
import jax.numpy as jnp
from jax import lax
from jax.experimental import pallas as pl
from jax.experimental.pallas import tpu as pltpu

D_MODEL = 1024
SEQ = 2048
GRID_W = 64
CTX_LEN = 256
RMS_EPS = 1e-6
N_MOD = 9
D_FF = 2816
POOL_WINDOWS = (2, 4, 8, 16)
POOL_DIM = D_MODEL // 2
POOL_GROUP_DIM = POOL_DIM // len(POOL_WINDOWS)
MLA_HEADS = D_MODEL // 128
QK_NOPE_DIM = 64
QK_ROPE_DIM = 32
QK_HEAD_DIM = QK_NOPE_DIM + QK_ROPE_DIM
V_HEAD_DIM = 64
Q_LORA_RANK = 768
KV_LORA_RANK = 256
ROPE_AXIS_DIM = QK_ROPE_DIM // 2
ROPE_THETA = 10000.0
ATTN_SCALE = 1.0 / math.sqrt(QK_HEAD_DIM)
ATTN_V_DIM = MLA_HEADS * V_HEAD_DIM

LANES = 128
SUBLANES = 8
HEAD_PAD = LANES
QK_PAD_DIM = MLA_HEADS * HEAD_PAD
MOD_ROWS = 16
MOD_TILE_N = 1536
FF_CHUNK = 256
N_FF_CHUNKS = D_FF // FF_CHUNK
W_IN_PAD = 1664
HALO = SUBLANES
PV_WIDTH = 256
PV_HEADS = PV_WIDTH // V_HEAD_DIM
ATTN_SUB_Q = 256
VMEM_LIMIT = 56 * 1024 * 1024

BF16 = jnp.bfloat16
F32 = jnp.float32


def _sigmoid(x):
    return 1.0 / (1.0 + jnp.exp(-x))


def _rms(x):
    return x * lax.rsqrt(jnp.mean(x * x, axis=-1, keepdims=True) + RMS_EPS)


def _adaln(x, gain, mod_ref, k):
    shift = mod_ref[0, 3 * k:3 * k + 1, :]
    scale = mod_ref[0, 3 * k + 1:3 * k + 2, :]
    return _rms(x) * gain * (1.0 + scale) + shift


def _params(semantics):
    return pltpu.CompilerParams(dimension_semantics=semantics, vmem_limit_bytes=VMEM_LIMIT)


def _mod_kernel(cond_ref, w_ref, b_ref, o_ref):
    cond = cond_ref[...]
    a = (cond * _sigmoid(cond)).astype(BF16)
    w = w_ref[0].astype(BF16)
    o_ref[0] = jnp.dot(a, w, preferred_element_type=F32) + b_ref[0]


def _modulation(cond, w_mod, b_mod):
    depth, d, n = w_mod.shape
    return pl.pallas_call(
        _mod_kernel,
        grid=(depth, n // MOD_TILE_N),
        in_specs=[
            pl.BlockSpec((MOD_ROWS, d), lambda i, j: (0, 0)),
            pl.BlockSpec((1, d, MOD_TILE_N), lambda i, j: (i, 0, j)),
            pl.BlockSpec((1, 1, MOD_TILE_N), lambda i, j: (i, 0, j)),
        ],
        out_specs=pl.BlockSpec((1, MOD_ROWS, MOD_TILE_N), lambda i, j: (i, 0, j)),
        out_shape=jax.ShapeDtypeStruct((depth, MOD_ROWS, n), F32),
        compiler_params=_params(("arbitrary", "arbitrary")),
        name="modulation",
    )(cond, w_mod, b_mod.reshape(depth, 1, n))


def _ffn_kernel(*refs, k, final):
    if final:
        s_ref, mod_ref, gain_ref, wg_ref, wu_ref, wd_ref, fg_ref, o_ref, u_scr, a_scr = refs
    else:
        s_ref, mod_ref, gain_ref, wg_ref, wu_ref, wd_ref, o_ref, u_scr, a_scr = refs
    u_scr[...] = _adaln(s_ref[...], gain_ref[...], mod_ref, k).astype(BF16)
    for f in range(N_FF_CHUNKS):
        cols = slice(f * FF_CHUNK, (f + 1) * FF_CHUNK)
        g = jnp.dot(u_scr[...], wg_ref[:, cols], preferred_element_type=F32)
        up = jnp.dot(u_scr[...], wu_ref[:, cols], preferred_element_type=F32)
        a_scr[:, cols] = (g * _sigmoid(g) * up).astype(BF16)
    out = jnp.dot(a_scr[...], wd_ref[...], preferred_element_type=F32)
    gate = mod_ref[0, 3 * k + 2:3 * k + 3, :]
    y = s_ref[...] + 0.5 * gate * out
    if final:
        y = _rms(y) * fg_ref[...]
    o_ref[...] = y


def _ffn_half(s, mod, gain, wg, wu, wd, k, tm, final_gain=None):
    n_tok, d = s.shape
    tiles_per_batch = n_tok // mod.shape[0] // tm
    final = final_gain is not None
    resident = lambda w: pl.BlockSpec(w.shape, lambda i: (0, 0), pipeline_mode=pl.Buffered(1))
    in_specs = [
        pl.BlockSpec((tm, d), lambda i: (i, 0)),
        pl.BlockSpec((1, N_MOD, d), lambda i: (i // tiles_per_batch, 0, 0)),
        pl.BlockSpec((1, d), lambda i: (0, 0)),
        resident(wg), resident(wu), resident(wd),
    ]
    args = [s, mod, gain.reshape(1, d), wg, wu, wd]
    if final:
        in_specs.append(pl.BlockSpec((1, d), lambda i: (0, 0)))
        args.append(final_gain.reshape(1, d))
    return pl.pallas_call(
        functools.partial(_ffn_kernel, k=k, final=final),
        grid=(n_tok // tm,),
        in_specs=in_specs,
        out_specs=pl.BlockSpec((tm, d), lambda i: (i, 0)),
        out_shape=jax.ShapeDtypeStruct((n_tok, d), F32),
        scratch_shapes=[pltpu.VMEM((tm, d), BF16), pltpu.VMEM((tm, D_FF), BF16)],
        compiler_params=_params(("arbitrary",)),
        name="ffn_half",
    )(*args)


def _rope(z, c_ref, s1_ref, s2_ref):
    fwd = pltpu.roll(z, HEAD_PAD - ROPE_AXIS_DIM // 2, axis=1)
    bwd = pltpu.roll(z, ROPE_AXIS_DIM // 2, axis=1)
    return z * c_ref[...] + fwd * s1_ref[...] + bwd * s2_ref[...]


def _mix_in_kernel(*refs, latent):
    if latent:
        (s_ref, mod_ref, gain_ref, w_in_ref, qg_ref, wq_ref, kvg_ref, wk_ref, wv_ref, place_ref,
         c_ref, s1_ref, s2_ref, pool_ref, q_ref, k_ref, v_ref) = refs
    else:
        (s_ref, mod_ref, gain_ref, w_in_ref, kvg_ref, wk_ref, wv_ref, place_ref,
         k_ref, v_ref) = refs
    u = _adaln(s_ref[...], gain_ref[...], mod_ref, 1).astype(BF16)
    z = jnp.dot(u, w_in_ref[...], preferred_element_type=F32)
    cuts = (POOL_DIM, POOL_DIM + Q_LORA_RANK, POOL_DIM + Q_LORA_RANK + KV_LORA_RANK)
    ckv = (_rms(z[:, cuts[1]:cuts[2]]) * kvg_ref[...]).astype(BF16)
    kr = pltpu.roll(z[:, cuts[2]:], QK_NOPE_DIM, axis=1)
    if latent:
        pool_ref[...] = z[:, :cuts[0]]
        cq = (_rms(z[:, cuts[0]:cuts[1]]) * qg_ref[...]).astype(BF16)
        q = jnp.dot(cq, wq_ref[...], preferred_element_type=F32)
        for h in range(MLA_HEADS):
            sl = slice(h * HEAD_PAD, (h + 1) * HEAD_PAD)
            q_ref[:, sl] = _rope(q[:, sl], c_ref, s1_ref, s2_ref).astype(BF16)
        kr = _rope(kr, c_ref, s1_ref, s2_ref)
    k = jnp.dot(ckv, wk_ref[...], preferred_element_type=F32)
    k = k + jnp.dot(kr.astype(BF16), place_ref[...], preferred_element_type=F32)
    k_ref[...] = k.astype(BF16)
    v_ref[...] = jnp.dot(ckv, wv_ref[...], preferred_element_type=F32).astype(BF16)


def _mix_in(s, mod, gain, w_in, kvg, wk, wv, place, tm, latent_args=None):
    n_tok, d = s.shape
    tiles_per_batch = n_tok // mod.shape[0] // tm
    latent = latent_args is not None
    const = lambda a: pl.BlockSpec(a.shape, lambda i: (0,) * a.ndim)
    tok = lambda w: pl.BlockSpec((tm, w), lambda i: (i, 0))
    in_specs = [tok(d), pl.BlockSpec((1, N_MOD, d), lambda i: (i // tiles_per_batch, 0, 0)),
                pl.BlockSpec((1, d), lambda i: (0, 0)), const(w_in)]
    args = [s, mod, gain.reshape(1, d), w_in]
    if latent:
        qg, wq, tabs = latent_args
        in_specs += [const(qg), const(wq)]
        args += [qg, wq]
    in_specs += [const(kvg), const(wk), const(wv), const(place)]
    args += [kvg, wk, wv, place]
    out_specs = [tok(QK_PAD_DIM), tok(ATTN_V_DIM)]
    out_shape = [jax.ShapeDtypeStruct((n_tok, QK_PAD_DIM), BF16),
                 jax.ShapeDtypeStruct((n_tok, ATTN_V_DIM), BF16)]
    if latent:
        rope_spec = pl.BlockSpec((tm, HEAD_PAD), lambda i: (i % tiles_per_batch, 0))
        in_specs += [rope_spec] * 3
        args += list(tabs)
        out_specs = [tok(POOL_DIM), tok(QK_PAD_DIM)] + out_specs
        out_shape = [jax.ShapeDtypeStruct((n_tok, POOL_DIM), F32),
                     jax.ShapeDtypeStruct((n_tok, QK_PAD_DIM), BF16)] + out_shape
    return pl.pallas_call(
        functools.partial(_mix_in_kernel, latent=latent),
        grid=(n_tok // tm,),
        in_specs=in_specs,
        out_specs=out_specs,
        out_shape=out_shape,
        compiler_params=_params(("arbitrary",)),
        name="mix_in_latent" if latent else "mix_in_context",
    )(*args)


def _attn_kernel(q_ref, kh_ref, kg_ref, vh_ref, vg_ref, o_ref):
    nt = (((1,), (1,)), ((), ()))
    c = ATTN_SCALE * math.log2(math.e)
    lane_head = lax.broadcasted_iota(jnp.int32, (1, PV_WIDTH), 1) // V_HEAD_DIM
    for sub in range(q_ref.shape[1] // ATTN_SUB_Q):
        rows = slice(sub * ATTN_SUB_Q, (sub + 1) * ATTN_SUB_Q)
        for grp in range(MLA_HEADS // PV_HEADS):
            vcols = slice(grp * PV_WIDTH, (grp + 1) * PV_WIDTH)
            acc = None
            for hh in range(PV_HEADS):
                h = grp * PV_HEADS + hh
                qk = slice(h * HEAD_PAD, (h + 1) * HEAD_PAD)
                q = q_ref[0, rows, qk]
                s_h = lax.dot_general(q, kh_ref[0, :, qk], nt, preferred_element_type=F32)
                s_g = lax.dot_general(q, kg_ref[0, :, qk], nt, preferred_element_type=F32)
                m = jnp.maximum(jnp.max(s_h, axis=-1, keepdims=True),
                                jnp.max(s_g, axis=-1, keepdims=True))
                e_h = jnp.exp2((s_h - m) * c)
                e_g = jnp.exp2((s_g - m) * c)
                denom = jnp.sum(e_h, axis=-1, keepdims=True) + jnp.sum(e_g, axis=-1, keepdims=True)
                res = jnp.dot(e_h.astype(BF16), vh_ref[0, :, vcols], preferred_element_type=F32)
                res = res + jnp.dot(e_g.astype(BF16), vg_ref[0, :, vcols], preferred_element_type=F32)
                term = jnp.where(lane_head == hh, res * (1.0 / denom), 0.0)
                acc = term if acc is None else acc + term
            o_ref[0, rows, vcols] = acc.astype(BF16)


def _attention(q, k_h, k_g, v_h, v_g, tq):
    b, l, _ = q.shape
    t_g = k_g.shape[1]
    return pl.pallas_call(
        _attn_kernel,
        grid=(b, l // tq),
        in_specs=[
            pl.BlockSpec((1, tq, QK_PAD_DIM), lambda i, j: (i, j, 0)),
            pl.BlockSpec((1, l, QK_PAD_DIM), lambda i, j: (i, 0, 0)),
            pl.BlockSpec((1, t_g, QK_PAD_DIM), lambda i, j: (i, 0, 0)),
            pl.BlockSpec((1, l, ATTN_V_DIM), lambda i, j: (i, 0, 0)),
            pl.BlockSpec((1, t_g, ATTN_V_DIM), lambda i, j: (i, 0, 0)),
        ],
        out_specs=pl.BlockSpec((1, tq, ATTN_V_DIM), lambda i, j: (i, j, 0)),
        out_shape=jax.ShapeDtypeStruct((b, l, ATTN_V_DIM), BF16),
        compiler_params=_params(("arbitrary", "arbitrary")),
        name="latent_attention",
    )(q, k_h, k_g, v_h, v_g)


def _mix_out_kernel(h_ref, mod_ref, pool_ref, attn_ref, pw_ref, ps_ref, wo_pool_ref, wo_attn_ref,
                    o_ref, win_scr, y_scr, *, tm, seq):
    t0 = pl.multiple_of(pl.program_id(1) * tm, tm)
    prev = pool_ref[0, pl.ds(pl.multiple_of(jnp.maximum(t0 - HALO, 0), HALO), HALO), :]
    nxt = pool_ref[0, pl.ds(pl.multiple_of(jnp.minimum(t0 + tm, seq - HALO), HALO), HALO), :]
    win_scr[0:HALO, :] = jnp.where(t0 > 0, prev, 0.0)
    win_scr[HALO:HALO + tm, :] = pool_ref[0, pl.ds(t0, tm), :]
    win_scr[HALO + tm:, :] = jnp.where(t0 + tm < seq, nxt, 0.0)
    t = t0 + lax.broadcasted_iota(jnp.int32, (tm, 1), 0)
    for g, w in enumerate(POOL_WINDOWS):
        lanes = slice(g * POOL_GROUP_DIM, (g + 1) * POOL_GROUP_DIM)
        total = win_scr[HALO - w // 2:HALO - w // 2 + tm, lanes]
        for off in range(-(w // 2) + 1, w - w // 2):
            total = total + win_scr[HALO + off:HALO + off + tm, lanes]
        lo = jnp.maximum(t - w // 2, 0)
        hi = jnp.minimum(t + (w - w // 2 - 1), seq - 1)
        cnt = (hi - lo + 1).astype(F32)
        p = total / cnt - win_scr[HALO:HALO + tm, lanes]
        y = jnp.dot(p.astype(BF16), pw_ref[g], preferred_element_type=F32)
        y_scr[:, lanes] = (y * ps_ref[:, lanes]).astype(BF16)
    out = jnp.dot(y_scr[...], wo_pool_ref[...], preferred_element_type=F32)
    out = out + jnp.dot(attn_ref[0], wo_attn_ref[...], preferred_element_type=F32)
    gate = mod_ref[0, 5:6, :]
    o_ref[0] = h_ref[0] + gate * out


def _mix_out(h, mod, pool, attn, pool_w, pool_scale, wo_pool, wo_attn, tm):
    b, l, d = h.shape
    const = lambda a: pl.BlockSpec(a.shape, lambda i, j: (0,) * a.ndim)
    return pl.pallas_call(
        functools.partial(_mix_out_kernel, tm=tm, seq=l),
        grid=(b, l // tm),
        in_specs=[
            pl.BlockSpec((1, tm, d), lambda i, j: (i, j, 0)),
            pl.BlockSpec((1, N_MOD, d), lambda i, j: (i, 0, 0)),
            pl.BlockSpec((1, l, POOL_DIM), lambda i, j: (i, 0, 0)),
            pl.BlockSpec((1, tm, ATTN_V_DIM), lambda i, j: (i, j, 0)),
            const(pool_w), const(pool_scale), const(wo_pool), const(wo_attn),
        ],
        out_specs=pl.BlockSpec((1, tm, d), lambda i, j: (i, j, 0)),
        out_shape=jax.ShapeDtypeStruct((b, l, d), F32),
        scratch_shapes=[pltpu.VMEM((tm + 2 * HALO, POOL_DIM), F32), pltpu.VMEM((tm, POOL_DIM), BF16)],
        compiler_params=_params(("arbitrary", "arbitrary")),
        name="mix_out",
    )(h, mod, pool, attn, pool_w, pool_scale, wo_pool, wo_attn)


def _conv_kernel(h_ref, hp_ref, hn_ref, mod_ref, gain_ref, w_in_ref, cw_ref, w_out_ref, o_ref,
                 u_scr, z_scr, *, tm, seq):
    t0 = pl.program_id(1) * tm
    gain = gain_ref[...]
    u_scr[0:HALO, :] = _adaln(hp_ref[0], gain, mod_ref, 1).astype(BF16)
    u_scr[HALO:HALO + tm, :] = _adaln(h_ref[0], gain, mod_ref, 1).astype(BF16)
    u_scr[HALO + tm:, :] = _adaln(hn_ref[0], gain, mod_ref, 1).astype(BF16)
    d = h_ref.shape[-1]
    cv = jnp.dot(u_scr[...], w_in_ref[:, d:], preferred_element_type=F32)
    t = t0 - HALO + lax.broadcasted_iota(jnp.int32, (tm + 2 * HALO, 1), 0)
    inside = jnp.logical_and(t >= 0, t < seq)
    z_scr[...] = jnp.where(inside, cv[:, :d] * cv[:, d:], 0.0)
    y = (cw_ref[0:1, :] * z_scr[HALO - 1:HALO - 1 + tm, :]
         + cw_ref[1:2, :] * z_scr[HALO:HALO + tm, :]
         + cw_ref[2:3, :] * z_scr[HALO + 1:HALO + 1 + tm, :])
    bg = jnp.dot(u_scr[HALO:HALO + tm, :], w_in_ref[:, :d], preferred_element_type=F32)
    out = jnp.dot((bg * y).astype(BF16), w_out_ref[...], preferred_element_type=F32)
    gate = mod_ref[0, 5:6, :]
    o_ref[0] = h_ref[0] + gate * out


def _conv_mixer(h, mod, gain, w_in, conv_w, w_out, tm):
    b, l, d = h.shape
    hb = tm // HALO
    const = lambda a: pl.BlockSpec(a.shape, lambda i, j: (0,) * a.ndim)
    return pl.pallas_call(
        functools.partial(_conv_kernel, tm=tm, seq=l),
        grid=(b, l // tm),
        in_specs=[
            pl.BlockSpec((1, tm, d), lambda i, j: (i, j, 0)),
            pl.BlockSpec((1, HALO, d), lambda i, j: (i, jnp.maximum(j * hb - 1, 0), 0)),
            pl.BlockSpec((1, HALO, d), lambda i, j: (i, jnp.minimum((j + 1) * hb, l // HALO - 1), 0)),
            pl.BlockSpec((1, N_MOD, d), lambda i, j: (i, 0, 0)),
            pl.BlockSpec((1, d), lambda i, j: (0, 0)),
            const(w_in), const(conv_w), const(w_out),
        ],
        out_specs=pl.BlockSpec((1, tm, d), lambda i, j: (i, j, 0)),
        out_shape=jax.ShapeDtypeStruct((b, l, d), F32),
        scratch_shapes=[pltpu.VMEM((tm + 2 * HALO, d), BF16), pltpu.VMEM((tm + 2 * HALO, d), F32)],
        compiler_params=_params(("arbitrary", "arbitrary")),
        name="conv_mixer",
    )(h, h, h, mod, gain.reshape(1, d), w_in, conv_w, w_out)


def _ffn_weights(wg, wu, wd):
    return wg.astype(BF16), wu.astype(BF16), wd.astype(BF16)


def _head_pad(w, per_head, start, width):
    r = w.shape[0]
    w = w.reshape(r, MLA_HEADS, per_head)[:, :, start:start + width]
    return jnp.pad(w, ((0, 0), (0, 0), (0, HEAD_PAD - width))).reshape(r, QK_PAD_DIM)


def _rope_tables(length):
    pos = jnp.arange(length)
    row = (pos // GRID_W).astype(F32)
    col = (pos % GRID_W).astype(F32)
    half = ROPE_AXIS_DIM // 2
    freqs = jnp.power(ROPE_THETA, -jnp.arange(0, ROPE_AXIS_DIM, 2, dtype=F32) / ROPE_AXIS_DIM)
    lane = jnp.arange(HEAD_PAD)
    o = lane - QK_NOPE_DIM
    rotary = jnp.logical_and(o >= 0, o < QK_ROPE_DIM)
    o = jnp.clip(o, 0, QK_ROPE_DIM - 1)
    ang = jnp.where((o // ROPE_AXIS_DIM == 0)[None, :], row[:, None], col[:, None]) * freqs[o % half][None, :]
    first = (o % ROPE_AXIS_DIM) < half
    cos = jnp.where(rotary[None, :], jnp.cos(ang), 1.0)
    sin = jnp.where(rotary[None, :], jnp.sin(ang), 0.0)
    return cos, jnp.where(first[None, :], -sin, 0.0), jnp.where(first[None, :], 0.0, sin)


def kernel(x, c, ctx, c_ctx, norm_g, w_mod, b_mod, ffn_w_gate, ffn_w_up, ffn_w_down, ab_w_in, pool_w,
           pool_scale, q_norm_g, w_uq, kv_norm_g, w_ukv, ab_w_out, conv_w_in, conv_w, conv_w_out,
           final_norm_g):
    b, l, d = x.shape
    t_g = ctx.shape[1]
    tm_h, tm_g, tq = 512, 256, 512

    cond = jnp.zeros((MOD_ROWS, d), F32).at[:b].set(c).at[b].set(c_ctx)
    m = _modulation(cond, w_mod, b_mod)
    mod_h = [m[i, :b].reshape(b, N_MOD, d) for i in range(2)]
    mod_g = jnp.broadcast_to(m[0, b].reshape(1, N_MOD, d), (b, N_MOD, d))

    ffn_w = [[_ffn_weights(ffn_w_gate[i, j], ffn_w_up[i, j], ffn_w_down[i, j]) for j in range(2)]
             for i in range(2)]

    h = x.reshape(b * l, d)
    g = ctx.reshape(b * t_g, d)

    h = _ffn_half(h, mod_h[0], norm_g[0, 0], *ffn_w[0][0], k=0, tm=tm_h)
    g = _ffn_half(g, mod_g, norm_g[0, 0], *ffn_w[0][0], k=0, tm=tm_g)

    w_in = jnp.pad(ab_w_in[0], ((0, 0), (0, W_IN_PAD - ab_w_in.shape[-1]))).astype(BF16)
    wq = _head_pad(w_uq[0], QK_HEAD_DIM, 0, QK_HEAD_DIM).astype(BF16)
    wk = _head_pad(w_ukv[0], QK_NOPE_DIM + V_HEAD_DIM, 0, QK_NOPE_DIM).astype(BF16)
    wv = w_ukv[0].reshape(KV_LORA_RANK, MLA_HEADS, QK_NOPE_DIM + V_HEAD_DIM)[:, :, QK_NOPE_DIM:]
    wv = wv.reshape(KV_LORA_RANK, ATTN_V_DIM).astype(BF16)
    lane = jnp.arange(HEAD_PAD)
    rotary = jnp.logical_and(lane >= QK_NOPE_DIM, lane < QK_HEAD_DIM)
    place = jnp.tile(jnp.where(rotary[:, None], jnp.eye(HEAD_PAD, dtype=F32), 0.0), (1, MLA_HEADS)).astype(BF16)
    qg = q_norm_g[0].reshape(1, Q_LORA_RANK)
    kvg = kv_norm_g[0].reshape(1, KV_LORA_RANK)
    tabs = _rope_tables(l)

    pool, q, k_h, v_h = _mix_in(h, mod_h[0], norm_g[0, 1], w_in, kvg, wk, wv, place, tm_h,
                                latent_args=(qg, wq, tabs))
    k_g, v_g = _mix_in(g, mod_g, norm_g[0, 1], w_in, kvg, wk, wv, place, tm_g)
    attn = _attention(q.reshape(b, l, -1), k_h.reshape(b, l, -1), k_g.reshape(b, t_g, -1),
                      v_h.reshape(b, l, -1), v_g.reshape(b, t_g, -1), tq)
    wo = ab_w_out[0].astype(BF16)
    h = _mix_out(h.reshape(b, l, d), mod_h[0], pool.reshape(b, l, -1), attn, pool_w[0].astype(BF16),
                 pool_scale[0].reshape(1, POOL_DIM), wo[:POOL_DIM], wo[POOL_DIM:], tm_h)
    h = _ffn_half(h.reshape(b * l, d), mod_h[0], norm_g[0, 2], *ffn_w[0][1], k=2, tm=tm_h)

    h = _ffn_half(h, mod_h[1], norm_g[1, 0], *ffn_w[1][0], k=0, tm=tm_h)
    h = _conv_mixer(h.reshape(b, l, d), mod_h[1], norm_g[1, 1], conv_w_in[0].astype(BF16), conv_w[0],
                    conv_w_out[0].astype(BF16), tm_h)
    h = _ffn_half(h.reshape(b * l, d), mod_h[1], norm_g[1, 2], *ffn_w[1][1], k=2, tm=tm_h,
                  final_gain=final_norm_g)
    return h.reshape(b, l, d)
```

```python
import functools
import math

import jax
import jax.numpy as jnp
from jax import lax
from jax.experimental import pallas as pl
from jax.experimental.pallas import tpu as pltpu

D_MODEL = 1024
SEQ = 2048
GRID_W = 64
CTX_LEN = 256
RMS_EPS = 1e-6
N_MOD = 9
D_FF = 2816
POOL_WINDOWS = (2, 4, 8, 16)
POOL_DIM = D_MODEL // 2
POOL_GROUP_DIM = POOL_DIM // len(POOL_WINDOWS)
MLA_HEADS = D_MODEL // 128
QK_NOPE_DIM = 64
QK_ROPE_DIM = 32
QK_HEAD_DIM = QK_NOPE_DIM + QK_ROPE_DIM
V_HEAD_DIM = 64
Q_LORA_RANK = 768
KV_LORA_RANK = 256
ROPE_AXIS_DIM = QK_ROPE_DIM // 2
ROPE_THETA = 10000.0
ATTN_SCALE = 1.0 / math.sqrt(QK_HEAD_DIM)
ATTN_V_DIM = MLA_HEADS * V_HEAD_DIM

LANES = 128
SUBLANES = 8
HEAD_PAD = LANES
QK_PAD_DIM = MLA_HEADS * HEAD_PAD
MOD_ROWS = 16
MOD_TILE_N = 1536
FF_CHUNK = 256
N_FF_CHUNKS = D_FF // FF_CHUNK
W_IN_PAD = 1664
HALO = SUBLANES
PV_WIDTH = 256
PV_HEADS = PV_WIDTH // V_HEAD_DIM
ATTN_SUB_Q = 256
VMEM_LIMIT = 56 * 1024 * 1024

BF16 = jnp.bfloat16
F32 = jnp.float32


def _sigmoid(x):
    return 1.0 / (1.0 + jnp.exp(-x))


def _rms(x):
    return x * lax.rsqrt(jnp.mean(x * x, axis=-1, keepdims=True) + RMS_EPS)


def _adaln(x, gain, mod_ref, k):
    shift = mod_ref[0, 3 * k:3 * k + 1, :]
    scale = mod_ref[0, 3 * k + 1:3 * k + 2, :]
    return _rms(x) * gain * (1.0 + scale) + shift


def _mm(a, w):
    return lax.dot_general(a, w, (((1,), (0,)), ((), ())), preferred_element_type=F32)


def _params(semantics):
    return pltpu.CompilerParams(dimension_semantics=semantics, vmem_limit_bytes=VMEM_LIMIT)


def _mod_kernel(cond_ref, w_ref, b_ref, o_ref):
    cond = cond_ref[...]
    a = (cond * _sigmoid(cond)).astype(BF16)
    w = w_ref[0].astype(BF16)
    o_ref[0] = jnp.dot(a, w, preferred_element_type=F32) + b_ref[0]


def _modulation(cond, w_mod, b_mod):
    depth, d, n = w_mod.shape
    return pl.pallas_call(
        _mod_kernel,
        grid=(depth, n // MOD_TILE_N),
        in_specs=[
            pl.BlockSpec((MOD_ROWS, d), lambda i, j: (0, 0)),
            pl.BlockSpec((1, d, MOD_TILE_N), lambda i, j: (i, 0, j)),
            pl.BlockSpec((1, 1, MOD_TILE_N), lambda i, j: (i, 0, j)),
        ],
        out_specs=pl.BlockSpec((1, MOD_ROWS, MOD_TILE_N), lambda i, j: (i, 0, j)),
        out_shape=jax.ShapeDtypeStruct((depth, MOD_ROWS, n), F32),
        compiler_params=_params(("arbitrary", "arbitrary")),
        name="modulation",
    )(cond, w_mod, b_mod.reshape(depth, 1, n))


def _ffn_kernel(*refs, k, final):
    if final:
        s_ref, mod_ref, gain_ref, wg_ref, wu_ref, wd_ref, fg_ref, o_ref, u_scr, a_scr = refs
    else:
        s_ref, mod_ref, gain_ref, wg_ref, wu_ref, wd_ref, o_ref, u_scr, a_scr = refs
    u_scr[...] = _adaln(s_ref[...], gain_ref[...], mod_ref, k).astype(BF16)
    for f in range(N_FF_CHUNKS):
        cols = slice(f * FF_CHUNK, (f + 1) * FF_CHUNK)
        g = _mm(u_scr[...], wg_ref[:, cols])
        up = _mm(u_scr[...], wu_ref[:, cols])
        a_scr[:, cols] = (g * _sigmoid(g) * up).astype(BF16)
    out = _mm(a_scr[...], wd_ref[...])
    gate = mod_ref[0, 3 * k + 2:3 * k + 3, :]
    y = s_ref[...] + 0.5 * gate * out
    if final:
        y = _rms(y) * fg_ref[...]
    o_ref[...] = y


def _ffn_half(s, mod, gain, wg, wu, wd, sel, k, tm, final_gain=None):
    n_tok, d = s.shape
    tiles_per_batch = n_tok // mod.shape[0] // tm
    final = final_gain is not None
    resident = lambda w: pl.BlockSpec((None, None) + w.shape[2:], lambda i: sel + (0, 0),
                                      pipeline_mode=pl.Buffered(1))
    in_specs = [
        pl.BlockSpec((tm, d), lambda i: (i, 0)),
        pl.BlockSpec((1, N_MOD, d), lambda i: (i // tiles_per_batch, 0, 0)),
        pl.BlockSpec((1, d), lambda i: (0, 0)),
        resident(wg), resident(wu), resident(wd),
    ]
    args = [s, mod, gain.reshape(1, d), wg, wu, wd]
    if final:
        in_specs.append(pl.BlockSpec((1, d), lambda i: (0, 0)))
        args.append(final_gain.reshape(1, d))
    return pl.pallas_call(
        functools.partial(_ffn_kernel, k=k, final=final),
        grid=(n_tok // tm,),
        in_specs=in_specs,
        out_specs=pl.BlockSpec((tm, d), lambda i: (i, 0)),
        out_shape=jax.ShapeDtypeStruct((n_tok, d), F32),
        scratch_shapes=[pltpu.VMEM((tm, d), BF16), pltpu.VMEM((tm, D_FF), BF16)],
        compiler_params=_params(("arbitrary",)),
        name="ffn_half",
    )(*args)


def _rope(z, c_ref, s1_ref, s2_ref):
    fwd = pltpu.roll(z, HEAD_PAD - ROPE_AXIS_DIM // 2, axis=1)
    bwd = pltpu.roll(z, ROPE_AXIS_DIM // 2, axis=1)
    return z * c_ref[...] + fwd * s1_ref[...] + bwd * s2_ref[...]


def _mix_in_kernel(*refs, latent):
    if latent:
        (s_ref, mod_ref, gain_ref, w_in_ref, qg_ref, wq_ref, kvg_ref, wk_ref, wv_ref, place_ref,
         c_ref, s1_ref, s2_ref, pool_ref, q_ref, k_ref, v_ref) = refs
    else:
        (s_ref, mod_ref, gain_ref, w_in_ref, kvg_ref, wk_ref, wv_ref, place_ref,
         k_ref, v_ref) = refs
    u = _adaln(s_ref[...], gain_ref[...], mod_ref, 1).astype(BF16)
    z = jnp.dot(u, w_in_ref[...], preferred_element_type=F32)
    cuts = (POOL_DIM, POOL_DIM + Q_LORA_RANK, POOL_DIM + Q_LORA_RANK + KV_LORA_RANK)
    ckv = (_rms(z[:, cuts[1]:cuts[2]]) * kvg_ref[...]).astype(BF16)
    kr = pltpu.roll(z[:, cuts[2]:], QK_NOPE_DIM, axis=1)
    if latent:
        pool_ref[...] = z[:, :cuts[0]]
        cq = (_rms(z[:, cuts[0]:cuts[1]]) * qg_ref[...]).astype(BF16)
        q = jnp.dot(cq, wq_ref[...], preferred_element_type=F32)
        for h in range(MLA_HEADS):
            sl = slice(h * HEAD_PAD, (h + 1) * HEAD_PAD)
            q_ref[:, sl] = _rope(q[:, sl], c_ref, s1_ref, s2_ref).astype(BF16)
        kr = _rope(kr, c_ref, s1_ref, s2_ref)
    k = jnp.dot(ckv, wk_ref[...], preferred_element_type=F32)
    k = k + jnp.dot(kr.astype(BF16), place_ref[...], preferred_element_type=F32)
    k_ref[...] = k.astype(BF16)
    v_ref[...] = jnp.dot(ckv, wv_ref[...], preferred_element_type=F32).astype(BF16)


def _mix_in(s, mod, gain, w_in, kvg, wk, wv, place, tm, latent_args=None):
    n_tok, d = s.shape
    tiles_per_batch = n_tok // mod.shape[0] // tm
    latent = latent_args is not None
    const = lambda a: pl.BlockSpec(a.shape, lambda i: (0,) * a.ndim)
    tok = lambda w: pl.BlockSpec((tm, w), lambda i: (i, 0))
    in_specs = [tok(d), pl.BlockSpec((1, N_MOD, d), lambda i: (i // tiles_per_batch, 0, 0)),
                pl.BlockSpec((1, d), lambda i: (0, 0)), const(w_in)]
    args = [s, mod, gain.reshape(1, d), w_in]
    if latent:
        qg, wq, tabs = latent_args
        in_specs += [const(qg), const(wq)]
        args += [qg, wq]
    in_specs += [const(kvg), const(wk), const(wv), const(place)]
    args += [kvg, wk, wv, place]
    out_specs = [tok(QK_PAD_DIM), tok(ATTN_V_DIM)]
    out_shape = [jax.ShapeDtypeStruct((n_tok, QK_PAD_DIM), BF16),
                 jax.ShapeDtypeStruct((n_tok, ATTN_V_DIM), BF16)]
    if latent:
        rope_spec = pl.BlockSpec((tm, HEAD_PAD), lambda i: (i % tiles_per_batch, 0))
        in_specs += [rope_spec] * 3
        args += list(tabs)
        out_specs = [tok(POOL_DIM), tok(QK_PAD_DIM)] + out_specs
        out_shape = [jax.ShapeDtypeStruct((n_tok, POOL_DIM), F32),
                     jax.ShapeDtypeStruct((n_tok, QK_PAD_DIM), BF16)] + out_shape
    return pl.pallas_call(
        functools.partial(_mix_in_kernel, latent=latent),
        grid=(n_tok // tm,),
        in_specs=in_specs,
        out_specs=out_specs,
        out_shape=out_shape,
        compiler_params=_params(("arbitrary",)),
        name="mix_in_latent" if latent else "mix_in_context",
    )(*args)


def _attn_kernel(q_ref, kh_ref, kg_ref, vh_ref, vg_ref, o_ref):
    nt = (((1,), (1,)), ((), ()))
    c = ATTN_SCALE * math.log2(math.e)
    lane_head = lax.broadcasted_iota(jnp.int32, (1, PV_WIDTH), 1) // V_HEAD_DIM
    for sub in range(q_ref.shape[1] // ATTN_SUB_Q):
        rows = slice(sub * ATTN_SUB_Q, (sub + 1) * ATTN_SUB_Q)
        for grp in range(MLA_HEADS // PV_HEADS):
            vcols = slice(grp * PV_WIDTH, (grp + 1) * PV_WIDTH)
            acc = None
            for hh in range(PV_HEADS):
                h = grp * PV_HEADS + hh
                qk = slice(h * HEAD_PAD, (h + 1) * HEAD_PAD)
                q = q_ref[0, rows, qk]
                s_h = lax.dot_general(q, kh_ref[0, :, qk], nt, preferred_element_type=F32)
                s_g = lax.dot_general(q, kg_ref[0, :, qk], nt, preferred_element_type=F32)
                m = jnp.maximum(jnp.max(s_h, axis=-1, keepdims=True),
                                jnp.max(s_g, axis=-1, keepdims=True))
                e_h = jnp.exp2((s_h - m) * c)
                e_g = jnp.exp2((s_g - m) * c)
                denom = jnp.sum(e_h, axis=-1, keepdims=True) + jnp.sum(e_g, axis=-1, keepdims=True)
                res = jnp.dot(e_h.astype(BF16), vh_ref[0, :, vcols], preferred_element_type=F32)
                res = res + jnp.dot(e_g.astype(BF16), vg_ref[0, :, vcols], preferred_element_type=F32)
                term = jnp.where(lane_head == hh, res * (1.0 / denom), 0.0)
                acc = term if acc is None else acc + term
            o_ref[0, rows, vcols] = acc.astype(BF16)


def _attention(q, k_h, k_g, v_h, v_g, tq):
    b, l, _ = q.shape
    t_g = k_g.shape[1]
    return pl.pallas_call(
        _attn_kernel,
        grid=(b, l // tq),
        in_specs=[
            pl.BlockSpec((1, tq, QK_PAD_DIM), lambda i, j: (i, j, 0)),
            pl.BlockSpec((1, l, QK_PAD_DIM), lambda i, j: (i, 0, 0)),
            pl.BlockSpec((1, t_g, QK_PAD_DIM), lambda i, j: (i, 0, 0)),
            pl.BlockSpec((1, l, ATTN_V_DIM), lambda i, j: (i, 0, 0)),
            pl.BlockSpec((1, t_g, ATTN_V_DIM), lambda i, j: (i, 0, 0)),
        ],
        out_specs=pl.BlockSpec((1, tq, ATTN_V_DIM), lambda i, j: (i, j, 0)),
        out_shape=jax.ShapeDtypeStruct((b, l, ATTN_V_DIM), BF16),
        compiler_params=_params(("arbitrary", "arbitrary")),
        name="latent_attention",
    )(q, k_h, k_g, v_h, v_g)


def _mix_out_kernel(h_ref, mod_ref, pool_ref, attn_ref, pw_ref, ps_ref, wo_pool_ref, wo_attn_ref,
                    o_ref, win_scr, y_scr, *, tm, seq):
    t0 = pl.multiple_of(pl.program_id(1) * tm, tm)
    prev = pool_ref[0, pl.ds(pl.multiple_of(jnp.maximum(t0 - HALO, 0), HALO), HALO), :]
    nxt = pool_ref[0, pl.ds(pl.multiple_of(jnp.minimum(t0 + tm, seq - HALO), HALO), HALO), :]
    win_scr[0:HALO, :] = jnp.where(t0 > 0, prev, 0.0)
    win_scr[HALO:HALO + tm, :] = pool_ref[0, pl.ds(t0, tm), :]
    win_scr[HALO + tm:, :] = jnp.where(t0 + tm < seq, nxt, 0.0)
    t = t0 + lax.broadcasted_iota(jnp.int32, (tm, 1), 0)
    for g, w in enumerate(POOL_WINDOWS):
        lanes = slice(g * POOL_GROUP_DIM, (g + 1) * POOL_GROUP_DIM)
        total = win_scr[HALO - w // 2:HALO - w // 2 + tm, lanes]
        for off in range(-(w // 2) + 1, w - w // 2):
            total = total + win_scr[HALO + off:HALO + off + tm, lanes]
        lo = jnp.maximum(t - w // 2, 0)
        hi = jnp.minimum(t + (w - w // 2 - 1), seq - 1)
        cnt = (hi - lo + 1).astype(F32)
        p = total / cnt - win_scr[HALO:HALO + tm, lanes]
        y = jnp.dot(p.astype(BF16), pw_ref[g], preferred_element_type=F32)
        y_scr[:, lanes] = (y * ps_ref[:, lanes]).astype(BF16)
    out = jnp.dot(y_scr[...], wo_pool_ref[...], preferred_element_type=F32)
    out = out + jnp.dot(attn_ref[0], wo_attn_ref[...], preferred_element_type=F32)
    gate = mod_ref[0, 5:6, :]
    o_ref[0] = h_ref[0] + gate * out


def _mix_out(h, mod, pool, attn, pool_w, pool_scale, wo_pool, wo_attn, tm):
    b, l, d = h.shape
    const = lambda a: pl.BlockSpec(a.shape, lambda i, j: (0,) * a.ndim)
    return pl.pallas_call(
        functools.partial(_mix_out_kernel, tm=tm, seq=l),
        grid=(b, l // tm),
        in_specs=[
            pl.BlockSpec((1, tm, d), lambda i, j: (i, j, 0)),
            pl.BlockSpec((1, N_MOD, d), lambda i, j: (i, 0, 0)),
            pl.BlockSpec((1, l, POOL_DIM), lambda i, j: (i, 0, 0)),
            pl.BlockSpec((1, tm, ATTN_V_DIM), lambda i, j: (i, j, 0)),
            const(pool_w), const(pool_scale), const(wo_pool), const(wo_attn),
        ],
        out_specs=pl.BlockSpec((1, tm, d), lambda i, j: (i, j, 0)),
        out_shape=jax.ShapeDtypeStruct((b, l, d), F32),
        scratch_shapes=[pltpu.VMEM((tm + 2 * HALO, POOL_DIM), F32), pltpu.VMEM((tm, POOL_DIM), BF16)],
        compiler_params=_params(("arbitrary", "arbitrary")),
        name="mix_out",
    )(h, mod, pool, attn, pool_w, pool_scale, wo_pool, wo_attn)


def _conv_kernel(h_ref, hp_ref, hn_ref, mod_ref, gain_ref, w_in_ref, cw_ref, w_out_ref, o_ref,
                 u_scr, z_scr, *, tm, seq):
    t0 = pl.program_id(1) * tm
    gain = gain_ref[...]
    u_scr[0:HALO, :] = _adaln(hp_ref[0], gain, mod_ref, 1).astype(BF16)
    u_scr[HALO:HALO + tm, :] = _adaln(h_ref[0], gain, mod_ref, 1).astype(BF16)
    u_scr[HALO + tm:, :] = _adaln(hn_ref[0], gain, mod_ref, 1).astype(BF16)
    d = h_ref.shape[-1]
    cv = jnp.dot(u_scr[...], w_in_ref[:, d:], preferred_element_type=F32)
    t = t0 - HALO + lax.broadcasted_iota(jnp.int32, (tm + 2 * HALO, 1), 0)
    inside = jnp.logical_and(t >= 0, t < seq)
    z_scr[...] = jnp.where(inside, cv[:, :d] * cv[:, d:], 0.0)
    y = (cw_ref[0:1, :] * z_scr[HALO - 1:HALO - 1 + tm, :]
         + cw_ref[1:2, :] * z_scr[HALO:HALO + tm, :]
         + cw_ref[2:3, :] * z_scr[HALO + 1:HALO + 1 + tm, :])
    bg = jnp.dot(u_scr[HALO:HALO + tm, :], w_in_ref[:, :d], preferred_element_type=F32)
    out = jnp.dot((bg * y).astype(BF16), w_out_ref[...], preferred_element_type=F32)
    gate = mod_ref[0, 5:6, :]
    o_ref[0] = h_ref[0] + gate * out


def _conv_mixer(h, mod, gain, w_in, conv_w, w_out, tm):
    b, l, d = h.shape
    hb = tm // HALO
    const = lambda a: pl.BlockSpec(a.shape, lambda i, j: (0,) * a.ndim)
    return pl.pallas_call(
        functools.partial(_conv_kernel, tm=tm, seq=l),
        grid=(b, l // tm),
        in_specs=[
            pl.BlockSpec((1, tm, d), lambda i, j: (i, j, 0)),
            pl.BlockSpec((1, HALO, d), lambda i, j: (i, jnp.maximum(j * hb - 1, 0), 0)),
            pl.BlockSpec((1, HALO, d), lambda i, j: (i, jnp.minimum((j + 1) * hb, l // HALO - 1), 0)),
            pl.BlockSpec((1, N_MOD, d), lambda i, j: (i, 0, 0)),
            pl.BlockSpec((1, d), lambda i, j: (0, 0)),
            const(w_in), const(conv_w), const(w_out),
        ],
        out_specs=pl.BlockSpec((1, tm, d), lambda i, j: (i, j, 0)),
        out_shape=jax.ShapeDtypeStruct((b, l, d), F32),
        scratch_shapes=[pltpu.VMEM((tm + 2 * HALO, d), BF16), pltpu.VMEM((tm + 2 * HALO, d), F32)],
        compiler_params=_params(("arbitrary", "arbitrary")),
        name="conv_mixer",
    )(h, h, h, mod, gain.reshape(1, d), w_in, conv_w, w_out)


def _head_pad(w, per_head, start, width):
    r = w.shape[0]
    w = w.reshape(r, MLA_HEADS, per_head)[:, :, start:start + width]
    return jnp.pad(w, ((0, 0), (0, 0), (0, HEAD_PAD - width))).reshape(r, QK_PAD_DIM)


def _rope_tables(length):
    pos = jnp.arange(length)
    row = (pos // GRID_W).astype(F32)
    col = (pos % GRID_W).astype(F32)
    half = ROPE_AXIS_DIM // 2
    freqs = jnp.power(ROPE_THETA, -jnp.arange(0, ROPE_AXIS_DIM, 2, dtype=F32) / ROPE_AXIS_DIM)
    lane = jnp.arange(HEAD_PAD)
    o = lane - QK_NOPE_DIM
    rotary = jnp.logical_and(o >= 0, o < QK_ROPE_DIM)
    o = jnp.clip(o, 0, QK_ROPE_DIM - 1)
    ang = jnp.where((o // ROPE_AXIS_DIM == 0)[None, :], row[:, None], col[:, None]) * freqs[o % half][None, :]
    first = (o % ROPE_AXIS_DIM) < half
    cos = jnp.where(rotary[None, :], jnp.cos(ang), 1.0)
    sin = jnp.where(rotary[None, :], jnp.sin(ang), 0.0)
    return cos, jnp.where(first[None, :], -sin, 0.0), jnp.where(first[None, :], 0.0, sin)


def kernel(x, c, ctx, c_ctx, norm_g, w_mod, b_mod, ffn_w_gate, ffn_w_up, ffn_w_down, ab_w_in, pool_w,
           pool_scale, q_norm_g, w_uq, kv_norm_g, w_ukv, ab_w_out, conv_w_in, conv_w, conv_w_out,
           final_norm_g):
    b, l, d = x.shape
    t_g = ctx.shape[1]
    tm_h, tm_g, tq = 512, 256, 512

    cond = jnp.zeros((MOD_ROWS, d), F32).at[:b].set(c).at[b].set(c_ctx)
    m = _modulation(cond, w_mod, b_mod)
    mod_h = [m[i, :b].reshape(b, N_MOD, d) for i in range(2)]
    mod_g = jnp.broadcast_to(m[0, b].reshape(1, N_MOD, d), (b, N_MOD, d))

    ffn_w = (ffn_w_gate, ffn_w_up, ffn_w_down)

    h = x.reshape(b * l, d)
    g = ctx.reshape(b * t_g, d)

    h = _ffn_half(h, mod_h[0], norm_g[0, 0], *ffn_w, (0, 0), k=0, tm=tm_h)
    g = _ffn_half(g, mod_g, norm_g[0, 0], *ffn_w, (0, 0), k=0, tm=tm_g)

    w_in = jnp.pad(ab_w_in[0], ((0, 0), (0, W_IN_PAD - ab_w_in.shape[-1]))).astype(BF16)
    wq = _head_pad(w_uq[0], QK_HEAD_DIM, 0, QK_HEAD_DIM).astype(BF16)
    wk = _head_pad(w_ukv[0], QK_NOPE_DIM + V_HEAD_DIM, 0, QK_NOPE_DIM).astype(BF16)
    wv = w_ukv[0].reshape(KV_LORA_RANK, MLA_HEADS, QK_NOPE_DIM + V_HEAD_DIM)[:, :, QK_NOPE_DIM:]
    wv = wv.reshape(KV_LORA_RANK, ATTN_V_DIM).astype(BF16)
    lane = jnp.arange(HEAD_PAD)
    rotary = jnp.logical_and(lane >= QK_NOPE_DIM, lane < QK_HEAD_DIM)
    place = jnp.tile(jnp.where(rotary[:, None], jnp.eye(HEAD_PAD, dtype=F32), 0.0), (1, MLA_HEADS)).astype(BF16)
    qg = q_norm_g[0].reshape(1, Q_LORA_RANK)
    kvg = kv_norm_g[0].reshape(1, KV_LORA_RANK)
    tabs = _rope_tables(l)

    pool, q, k_h, v_h = _mix_in(h, mod_h[0], norm_g[0, 1], w_in, kvg, wk, wv, place, tm_h,
                                latent_args=(qg, wq, tabs))
    k_g, v_g = _mix_in(g, mod_g, norm_g[0, 1], w_in, kvg, wk, wv, place, tm_g)
    attn = _attention(q.reshape(b, l, -1), k_h.reshape(b, l, -1), k_g.reshape(b, t_g, -1),
                      v_h.reshape(b, l, -1), v_g.reshape(b, t_g, -1), tq)
    wo = ab_w_out[0].astype(BF16)
    h = _mix_out(h.reshape(b, l, d), mod_h[0], pool.reshape(b, l, -1), attn, pool_w[0].astype(BF16),
                 pool_scale[0].reshape(1, POOL_DIM), wo[:POOL_DIM], wo[POOL_DIM:], tm_h)
    h = _ffn_half(h.reshape(b * l, d), mod_h[0], norm_g[0, 2], *ffn_w, (0, 1), k=2, tm=tm_h)

    h = _ffn_half(h, mod_h[1], norm_g[1, 0], *ffn_w, (1, 0), k=0, tm=tm_h)
    h = _conv_mixer(h.reshape(b, l, d), mod_h[1], norm_g[1, 1], conv_w_in[0].astype(BF16), conv_w[0],
                    conv_w_out[0].astype(BF16), tm_h)
    h = _ffn_half(h.reshape(b * l, d), mod_h[1], norm_g[1, 2], *ffn_w, (1, 1), k=2, tm=tm_h,
                  final_gain=final_norm_g)
    return h.reshape(b, l, d)
```

```python
import functools
import math

import jax
import jax.numpy as jnp
from jax import lax
from jax.experimental import pallas as pl
from jax.experimental.pallas import tpu as pltpu

D_MODEL = 1024
SEQ = 2048
GRID_W = 64
CTX_LEN = 256
RMS_EPS = 1e-6
N_MOD = 9
D_FF = 2816
POOL_WINDOWS = (2, 4, 8, 16)
POOL_DIM = D_MODEL // 2
POOL_GROUP_DIM = POOL_DIM // len(POOL_WINDOWS)
MLA_HEADS = D_MODEL // 128
QK_NOPE_DIM = 64
QK_ROPE_DIM = 32
QK_HEAD_DIM = QK_NOPE_DIM + QK_ROPE_DIM
V_HEAD_DIM = 64
Q_LORA_RANK = 768
KV_LORA_RANK = 256
ROPE_AXIS_DIM = QK_ROPE_DIM // 2
ROPE_THETA = 10000.0
ATTN_SCALE = 1.0 / math.sqrt(QK_HEAD_DIM)
ATTN_V_DIM = MLA_HEADS * V_HEAD_DIM

LANES = 128
SUBLANES = 8
HEAD_PAD = LANES
QK_PAD_DIM = MLA_HEADS * HEAD_PAD
MOD_ROWS = 16
MOD_TILE_N = 1536
FF_CHUNK = 256
N_FF_CHUNKS = D_FF // FF_CHUNK
FF_HEAD_ROWS = 256
FF_TAIL_ROWS = 256
W_IN_PAD = 1664
HALO = SUBLANES
PV_WIDTH = 256
PV_HEADS = PV_WIDTH // V_HEAD_DIM
ATTN_SUB_Q = 256
VMEM_LIMIT = 56 * 1024 * 1024

BF16 = jnp.bfloat16
F32 = jnp.float32


def _sigmoid(x):
    return 1.0 / (1.0 + jnp.exp(-x))


def _rms(x):
    return x * lax.rsqrt(jnp.mean(x * x, axis=-1, keepdims=True) + RMS_EPS)


def _adaln(x, gain, mod_ref, k):
    shift = mod_ref[0, 3 * k:3 * k + 1, :]
    scale = mod_ref[0, 3 * k + 1:3 * k + 2, :]
    return _rms(x) * gain * (1.0 + scale) + shift


def _mm(a, w):
    return lax.dot_general(a, w, (((1,), (0,)), ((), ())), preferred_element_type=F32)


def _params(semantics):
    return pltpu.CompilerParams(dimension_semantics=semantics, vmem_limit_bytes=VMEM_LIMIT)


def _mod_kernel(cond_ref, w_ref, b_ref, o_ref):
    cond = cond_ref[...]
    a = (cond * _sigmoid(cond)).astype(BF16)
    w = w_ref[0].astype(BF16)
    o_ref[0] = jnp.dot(a, w, preferred_element_type=F32) + b_ref[0]


def _modulation(cond, w_mod, b_mod):
    depth, d, n = w_mod.shape
    return pl.pallas_call(
        _mod_kernel,
        grid=(depth, n // MOD_TILE_N),
        in_specs=[
            pl.BlockSpec((MOD_ROWS, d), lambda i, j: (0, 0)),
            pl.BlockSpec((1, d, MOD_TILE_N), lambda i, j: (i, 0, j)),
            pl.BlockSpec((1, 1, MOD_TILE_N), lambda i, j: (i, 0, j)),
        ],
        out_specs=pl.BlockSpec((1, MOD_ROWS, MOD_TILE_N), lambda i, j: (i, 0, j)),
        out_shape=jax.ShapeDtypeStruct((depth, MOD_ROWS, n), F32),
        compiler_params=_params(("arbitrary", "arbitrary")),
        name="modulation",
    )(cond, w_mod, b_mod.reshape(depth, 1, n))


def _ffn_kernel(*refs, k, final):
    if final:
        s_ref, mod_ref, gain_ref, wg_ref, wu_ref, wd_ref, fg_ref, o_ref, u_scr, a_scr = refs
    else:
        s_ref, mod_ref, gain_ref, wg_ref, wu_ref, wd_ref, o_ref, u_scr, a_scr = refs
    tm = s_ref.shape[0]

    def hidden(rows, f):
        cols = slice(f * FF_CHUNK, (f + 1) * FF_CHUNK)
        g = _mm(u_scr[rows, :], wg_ref[:, cols])
        up = _mm(u_scr[rows, :], wu_ref[:, cols])
        a_scr[rows, cols] = (g * _sigmoid(g) * up).astype(BF16)

    for p in range(tm // FF_HEAD_ROWS):
        rows = slice(p * FF_HEAD_ROWS, (p + 1) * FF_HEAD_ROWS)
        u_scr[rows, :] = _adaln(s_ref[rows, :], gain_ref[...], mod_ref, k).astype(BF16)
        hidden(rows, 0)
    for f in range(1, N_FF_CHUNKS):
        hidden(slice(None), f)
    gate = mod_ref[0, 3 * k + 2:3 * k + 3, :]
    for p in range(tm // FF_TAIL_ROWS):
        rows = slice(p * FF_TAIL_ROWS, (p + 1) * FF_TAIL_ROWS)
        y = s_ref[rows, :] + 0.5 * gate * _mm(a_scr[rows, :], wd_ref[...])
        if final:
            y = _rms(y) * fg_ref[...]
        o_ref[rows, :] = y


def _ffn_half(s, mod, gain, wg, wu, wd, sel, k, tm, final_gain=None):
    n_tok, d = s.shape
    tiles_per_batch = n_tok // mod.shape[0] // tm
    final = final_gain is not None
    resident = lambda w: pl.BlockSpec((None, None) + w.shape[2:], lambda i: sel + (0, 0),
                                      pipeline_mode=pl.Buffered(1))
    in_specs = [
        pl.BlockSpec((tm, d), lambda i: (i, 0)),
        pl.BlockSpec((1, N_MOD, d), lambda i: (i // tiles_per_batch, 0, 0)),
        pl.BlockSpec((1, d), lambda i: (0, 0)),
        resident(wg), resident(wu), resident(wd),
    ]
    args = [s, mod, gain.reshape(1, d), wg, wu, wd]
    if final:
        in_specs.append(pl.BlockSpec((1, d), lambda i: (0, 0)))
        args.append(final_gain.reshape(1, d))
    return pl.pallas_call(
        functools.partial(_ffn_kernel, k=k, final=final),
        grid=(n_tok // tm,),
        in_specs=in_specs,
        out_specs=pl.BlockSpec((tm, d), lambda i: (i, 0)),
        out_shape=jax.ShapeDtypeStruct((n_tok, d), F32),
        scratch_shapes=[pltpu.VMEM((tm, d), BF16), pltpu.VMEM((tm, D_FF), BF16)],
        compiler_params=_params(("arbitrary",)),
        name="ffn_half",
    )(*args)


def _rope(z, c_ref, s1_ref, s2_ref):
    fwd = pltpu.roll(z, HEAD_PAD - ROPE_AXIS_DIM // 2, axis=1)
    bwd = pltpu.roll(z, ROPE_AXIS_DIM // 2, axis=1)
    return z * c_ref[...] + fwd * s1_ref[...] + bwd * s2_ref[...]


def _mix_in_kernel(*refs, latent):
    if latent:
        (s_ref, mod_ref, gain_ref, w_in_ref, qg_ref, wq_ref, kvg_ref, wk_ref, wv_ref, place_ref,
         c_ref, s1_ref, s2_ref, pool_ref, q_ref, k_ref, v_ref) = refs
    else:
        (s_ref, mod_ref, gain_ref, w_in_ref, kvg_ref, wk_ref, wv_ref, place_ref,
         k_ref, v_ref) = refs
    u = _adaln(s_ref[...], gain_ref[...], mod_ref, 1).astype(BF16)
    z = jnp.dot(u, w_in_ref[...], preferred_element_type=F32)
    cuts = (POOL_DIM, POOL_DIM + Q_LORA_RANK, POOL_DIM + Q_LORA_RANK + KV_LORA_RANK)
    ckv = (_rms(z[:, cuts[1]:cuts[2]]) * kvg_ref[...]).astype(BF16)
    kr = pltpu.roll(z[:, cuts[2]:], QK_NOPE_DIM, axis=1)
    if latent:
        pool_ref[...] = z[:, :cuts[0]]
        cq = (_rms(z[:, cuts[0]:cuts[1]]) * qg_ref[...]).astype(BF16)
        q = jnp.dot(cq, wq_ref[...], preferred_element_type=F32)
        for h in range(MLA_HEADS):
            sl = slice(h * HEAD_PAD, (h + 1) * HEAD_PAD)
            q_ref[:, sl] = _rope(q[:, sl], c_ref, s1_ref, s2_ref).astype(BF16)
        kr = _rope(kr, c_ref, s1_ref, s2_ref)
    k = jnp.dot(ckv, wk_ref[...], preferred_element_type=F32)
    k = k + jnp.dot(kr.astype(BF16), place_ref[...], preferred_element_type=F32)
    k_ref[...] = k.astype(BF16)
    v_ref[...] = jnp.dot(ckv, wv_ref[...], preferred_element_type=F32).astype(BF16)


def _mix_in(s, mod, gain, w_in, kvg, wk, wv, place, tm, latent_args=None):
    n_tok, d = s.shape
    tiles_per_batch = n_tok // mod.shape[0] // tm
    latent = latent_args is not None
    const = lambda a: pl.BlockSpec(a.shape, lambda i: (0,) * a.ndim)
    tok = lambda w: pl.BlockSpec((tm, w), lambda i: (i, 0))
    in_specs = [tok(d), pl.BlockSpec((1, N_MOD, d), lambda i: (i // tiles_per_batch, 0, 0)),
                pl.BlockSpec((1, d), lambda i: (0, 0)), const(w_in)]
    args = [s, mod, gain.reshape(1, d), w_in]
    if latent:
        qg, wq, tabs = latent_args
        in_specs += [const(qg), const(wq)]
        args += [qg, wq]
    in_specs += [const(kvg), const(wk), const(wv), const(place)]
    args += [kvg, wk, wv, place]
    out_specs = [tok(QK_PAD_DIM), tok(ATTN_V_DIM)]
    out_shape = [jax.ShapeDtypeStruct((n_tok, QK_PAD_DIM), BF16),
                 jax.ShapeDtypeStruct((n_tok, ATTN_V_DIM), BF16)]
    if latent:
        rope_spec = pl.BlockSpec((tm, HEAD_PAD), lambda i: (i % tiles_per_batch, 0))
        in_specs += [rope_spec] * 3
        args += list(tabs)
        out_specs = [tok(POOL_DIM), tok(QK_PAD_DIM)] + out_specs
        out_shape = [jax.ShapeDtypeStruct((n_tok, POOL_DIM), F32),
                     jax.ShapeDtypeStruct((n_tok, QK_PAD_DIM), BF16)] + out_shape
    return pl.pallas_call(
        functools.partial(_mix_in_kernel, latent=latent),
        grid=(n_tok // tm,),
        in_specs=in_specs,
        out_specs=out_specs,
        out_shape=out_shape,
        compiler_params=_params(("arbitrary",)),
        name="mix_in_latent" if latent else "mix_in_context",
    )(*args)


def _attn_kernel(q_ref, kh_ref, kg_ref, vh_ref, vg_ref, o_ref):
    nt = (((1,), (1,)), ((), ()))
    c = ATTN_SCALE * math.log2(math.e)
    lane_head = lax.broadcasted_iota(jnp.int32, (1, PV_WIDTH), 1) // V_HEAD_DIM
    for sub in range(q_ref.shape[1] // ATTN_SUB_Q):
        rows = slice(sub * ATTN_SUB_Q, (sub + 1) * ATTN_SUB_Q)
        for grp in range(MLA_HEADS // PV_HEADS):
            vcols = slice(grp * PV_WIDTH, (grp + 1) * PV_WIDTH)
            acc = None
            for hh in range(PV_HEADS):
                h = grp * PV_HEADS + hh
                qk = slice(h * HEAD_PAD, (h + 1) * HEAD_PAD)
                q = q_ref[0, rows, qk]
                s_h = lax.dot_general(q, kh_ref[0, :, qk], nt, preferred_element_type=F32)
                s_g = lax.dot_general(q, kg_ref[0, :, qk], nt, preferred_element_type=F32)
                m = jnp.maximum(jnp.max(s_h, axis=-1, keepdims=True),
                                jnp.max(s_g, axis=-1, keepdims=True))
                e_h = jnp.exp2((s_h - m) * c)
                e_g = jnp.exp2((s_g - m) * c)
                denom = jnp.sum(e_h, axis=-1, keepdims=True) + jnp.sum(e_g, axis=-1, keepdims=True)
                res = jnp.dot(e_h.astype(BF16), vh_ref[0, :, vcols], preferred_element_type=F32)
                res = res + jnp.dot(e_g.astype(BF16), vg_ref[0, :, vcols], preferred_element_type=F32)
                term = jnp.where(lane_head == hh, res * (1.0 / denom), 0.0)
                acc = term if acc is None else acc + term
            o_ref[0, rows, vcols] = acc.astype(BF16)


def _attention(q, k_h, k_g, v_h, v_g, tq):
    b, l, _ = q.shape
    t_g = k_g.shape[1]
    return pl.pallas_call(
        _attn_kernel,
        grid=(b, l // tq),
        in_specs=[
            pl.BlockSpec((1, tq, QK_PAD_DIM), lambda i, j: (i, j, 0)),
            pl.BlockSpec((1, l, QK_PAD_DIM), lambda i, j: (i, 0, 0)),
            pl.BlockSpec((1, t_g, QK_PAD_DIM), lambda i, j: (i, 0, 0)),
            pl.BlockSpec((1, l, ATTN_V_DIM), lambda i, j: (i, 0, 0)),
            pl.BlockSpec((1, t_g, ATTN_V_DIM), lambda i, j: (i, 0, 0)),
        ],
        out_specs=pl.BlockSpec((1, tq, ATTN_V_DIM), lambda i, j: (i, j, 0)),
        out_shape=jax.ShapeDtypeStruct((b, l, ATTN_V_DIM), BF16),
        compiler_params=_params(("arbitrary", "arbitrary")),
        name="latent_attention",
    )(q, k_h, k_g, v_h, v_g)


def _mix_out_kernel(h_ref, mod_ref, pool_ref, attn_ref, pw_ref, ps_ref, wo_ref, o_ref, win_scr, y_scr,
                    *, tm, seq):
    t0 = pl.multiple_of(pl.program_id(1) * tm, tm)
    prev = pool_ref[0, pl.ds(pl.multiple_of(jnp.maximum(t0 - HALO, 0), HALO), HALO), :]
    nxt = pool_ref[0, pl.ds(pl.multiple_of(jnp.minimum(t0 + tm, seq - HALO), HALO), HALO), :]
    win_scr[0:HALO, :] = jnp.where(t0 > 0, prev, 0.0)
    win_scr[HALO:HALO + tm, :] = pool_ref[0, pl.ds(t0, tm), :]
    win_scr[HALO + tm:, :] = jnp.where(t0 + tm < seq, nxt, 0.0)
    t = (t0 + lax.broadcasted_iota(jnp.int32, (tm, 1), 0)).astype(F32)
    n = tm + 2 * HALO
    for g, w in enumerate(POOL_WINDOWS):
        lanes = slice(g * POOL_GROUP_DIM, (g + 1) * POOL_GROUP_DIM)
        fwd = win_scr[:, lanes]
        span = 1
        while span < w:
            fwd = fwd + pltpu.roll(fwd, n - span, axis=0)
            span *= 2
        total = pltpu.roll(fwd, w // 2, axis=0)[HALO:HALO + tm]
        cnt = jnp.minimum(t, float(w // 2)) + jnp.minimum(float(seq - 1) - t, float(w - w // 2 - 1)) + 1.0
        p = total / cnt - win_scr[HALO:HALO + tm, lanes]
        y = _mm(p.astype(BF16), pw_ref[g])
        y_scr[:, lanes] = (y * ps_ref[:, lanes]).astype(BF16)
    out = _mm(y_scr[...], wo_ref[:POOL_DIM, :]) + _mm(attn_ref[0], wo_ref[POOL_DIM:, :])
    gate = mod_ref[0, 5:6, :]
    o_ref[0] = h_ref[0] + gate * out


def _mix_out(h, mod, pool, attn, pool_w, pool_scale, w_out, tm):
    b, l, d = h.shape
    const = lambda a: pl.BlockSpec((None,) + a.shape[1:], lambda i, j: (0,) * a.ndim,
                                   pipeline_mode=pl.Buffered(1))
    return pl.pallas_call(
        functools.partial(_mix_out_kernel, tm=tm, seq=l),
        grid=(b, l // tm),
        in_specs=[
            pl.BlockSpec((1, tm, d), lambda i, j: (i, j, 0)),
            pl.BlockSpec((1, N_MOD, d), lambda i, j: (i, 0, 0)),
            pl.BlockSpec((1, l, POOL_DIM), lambda i, j: (i, 0, 0)),
            pl.BlockSpec((1, tm, ATTN_V_DIM), lambda i, j: (i, j, 0)),
            const(pool_w), const(pool_scale), const(w_out),
        ],
        out_specs=pl.BlockSpec((1, tm, d), lambda i, j: (i, j, 0)),
        out_shape=jax.ShapeDtypeStruct((b, l, d), F32),
        scratch_shapes=[pltpu.VMEM((tm + 2 * HALO, POOL_DIM), F32), pltpu.VMEM((tm, POOL_DIM), BF16)],
        compiler_params=_params(("arbitrary", "arbitrary")),
        name="mix_out",
    )(h, mod, pool, attn, pool_w, pool_scale, w_out)


def _conv_kernel(h_ref, hp_ref, hn_ref, mod_ref, gain_ref, w_in_ref, cw_ref, w_out_ref, o_ref,
                 u_scr, z_scr, *, tm, seq):
    t0 = pl.program_id(1) * tm
    gain = gain_ref[...]
    u_scr[0:HALO, :] = _adaln(hp_ref[0], gain, mod_ref, 1).astype(BF16)
    u_scr[HALO:HALO + tm, :] = _adaln(h_ref[0], gain, mod_ref, 1).astype(BF16)
    u_scr[HALO + tm:, :] = _adaln(hn_ref[0], gain, mod_ref, 1).astype(BF16)
    d = h_ref.shape[-1]
    cv = _mm(u_scr[...], w_in_ref[:, d:])
    t = t0 - HALO + lax.broadcasted_iota(jnp.int32, (tm + 2 * HALO, 1), 0)
    inside = jnp.logical_and(t >= 0, t < seq)
    z_scr[...] = jnp.where(inside, cv[:, :d] * cv[:, d:], 0.0)
    y = (cw_ref[0:1, :] * z_scr[HALO - 1:HALO - 1 + tm, :]
         + cw_ref[1:2, :] * z_scr[HALO:HALO + tm, :]
         + cw_ref[2:3, :] * z_scr[HALO + 1:HALO + 1 + tm, :])
    bg = _mm(u_scr[HALO:HALO + tm, :], w_in_ref[:, :d])
    out = _mm((bg * y).astype(BF16), w_out_ref[...])
    gate = mod_ref[0, 5:6, :]
    o_ref[0] = h_ref[0] + gate * out


def _conv_mixer(h, mod, gain, w_in, conv_w, w_out, tm):
    b, l, d = h.shape
    hb = tm // HALO
    const = lambda a: pl.BlockSpec((None,) + a.shape[1:], lambda i, j: (0,) * a.ndim,
                                   pipeline_mode=pl.Buffered(1))
    return pl.pallas_call(
        functools.partial(_conv_kernel, tm=tm, seq=l),
        grid=(b, l // tm),
        in_specs=[
            pl.BlockSpec((1, tm, d), lambda i, j: (i, j, 0)),
            pl.BlockSpec((1, HALO, d), lambda i, j: (i, jnp.maximum(j * hb - 1, 0), 0)),
            pl.BlockSpec((1, HALO, d), lambda i, j: (i, jnp.minimum((j + 1) * hb, l // HALO - 1), 0)),
            pl.BlockSpec((1, N_MOD, d), lambda i, j: (i, 0, 0)),
            pl.BlockSpec((1, d), lambda i, j: (0, 0)),
            const(w_in), const(conv_w), const(w_out),
        ],
        out_specs=pl.BlockSpec((1, tm, d), lambda i, j: (i, j, 0)),
        out_shape=jax.ShapeDtypeStruct((b, l, d), F32),
        scratch_shapes=[pltpu.VMEM((tm + 2 * HALO, d), BF16), pltpu.VMEM((tm + 2 * HALO, d), F32)],
        compiler_params=_params(("arbitrary", "arbitrary")),
        name="conv_mixer",
    )(h, h, h, mod, gain.reshape(1, d), w_in, conv_w, w_out)


def _head_pad(w, per_head, start, width):
    r = w.shape[0]
    w = w.reshape(r, MLA_HEADS, per_head)[:, :, start:start + width]
    return jnp.pad(w, ((0, 0), (0, 0), (0, HEAD_PAD - width))).reshape(r, QK_PAD_DIM)


def _rope_tables(length):
    pos = jnp.arange(length)
    row = (pos // GRID_W).astype(F32)
    col = (pos % GRID_W).astype(F32)
    half = ROPE_AXIS_DIM // 2
    freqs = jnp.power(ROPE_THETA, -jnp.arange(0, ROPE_AXIS_DIM, 2, dtype=F32) / ROPE_AXIS_DIM)
    lane = jnp.arange(HEAD_PAD)
    o = lane - QK_NOPE_DIM
    rotary = jnp.logical_and(o >= 0, o < QK_ROPE_DIM)
    o = jnp.clip(o, 0, QK_ROPE_DIM - 1)
    ang = jnp.where((o // ROPE_AXIS_DIM == 0)[None, :], row[:, None], col[:, None]) * freqs[o % half][None, :]
    first = (o % ROPE_AXIS_DIM) < half
    cos = jnp.where(rotary[None, :], jnp.cos(ang), 1.0)
    sin = jnp.where(rotary[None, :], jnp.sin(ang), 0.0)
    return cos, jnp.where(first[None, :], -sin, 0.0), jnp.where(first[None, :], 0.0, sin)


def kernel(x, c, ctx, c_ctx, norm_g, w_mod, b_mod, ffn_w_gate, ffn_w_up, ffn_w_down, ab_w_in, pool_w,
           pool_scale, q_norm_g, w_uq, kv_norm_g, w_ukv, ab_w_out, conv_w_in, conv_w, conv_w_out,
           final_norm_g):
    b, l, d = x.shape
    t_g = ctx.shape[1]
    tm_h, tm_g, tq = 512, 256, 512

    cond = jnp.zeros((MOD_ROWS, d), F32).at[:b].set(c).at[b].set(c_ctx)
    m = _modulation(cond, w_mod, b_mod)
    mod_h = [m[i, :b].reshape(b, N_MOD, d) for i in range(2)]
    mod_g = jnp.broadcast_to(m[0, b].reshape(1, N_MOD, d), (b, N_MOD, d))

    ffn_w = (ffn_w_gate, ffn_w_up, ffn_w_down)

    h = x.reshape(b * l, d)
    g = ctx.reshape(b * t_g, d)

    h = _ffn_half(h, mod_h[0], norm_g[0, 0], *ffn_w, (0, 0), k=0, tm=tm_h)
    g = _ffn_half(g, mod_g, norm_g[0, 0], *ffn_w, (0, 0), k=0, tm=tm_g)

    w_in = jnp.pad(ab_w_in[0], ((0, 0), (0, W_IN_PAD - ab_w_in.shape[-1]))).astype(BF16)
    wq = _head_pad(w_uq[0], QK_HEAD_DIM, 0, QK_HEAD_DIM).astype(BF16)
    wk = _head_pad(w_ukv[0], QK_NOPE_DIM + V_HEAD_DIM, 0, QK_NOPE_DIM).astype(BF16)
    wv = w_ukv[0].reshape(KV_LORA_RANK, MLA_HEADS, QK_NOPE_DIM + V_HEAD_DIM)[:, :, QK_NOPE_DIM:]
    wv = wv.reshape(KV_LORA_RANK, ATTN_V_DIM).astype(BF16)
    lane = jnp.arange(HEAD_PAD)
    rotary = jnp.logical_and(lane >= QK_NOPE_DIM, lane < QK_HEAD_DIM)
    place = jnp.tile(jnp.where(rotary[:, None], jnp.eye(HEAD_PAD, dtype=F32), 0.0), (1, MLA_HEADS)).astype(BF16)
    qg = q_norm_g[0].reshape(1, Q_LORA_RANK)
    kvg = kv_norm_g[0].reshape(1, KV_LORA_RANK)
    tabs = _rope_tables(l)

    pool, q, k_h, v_h = _mix_in(h, mod_h[0], norm_g[0, 1], w_in, kvg, wk, wv, place, tm_h,
                                latent_args=(qg, wq, tabs))
    k_g, v_g = _mix_in(g, mod_g, norm_g[0, 1], w_in, kvg, wk, wv, place, tm_g)
    attn = _attention(q.reshape(b, l, -1), k_h.reshape(b, l, -1), k_g.reshape(b, t_g, -1),
                      v_h.reshape(b, l, -1), v_g.reshape(b, t_g, -1), tq)
    h = _mix_out(h.reshape(b, l, d), mod_h[0], pool.reshape(b, l, -1), attn, pool_w,
                 pool_scale.reshape(-1, 1, POOL_DIM), ab_w_out, tm_h)
    h = _ffn_half(h.reshape(b * l, d), mod_h[0], norm_g[0, 2], *ffn_w, (0, 1), k=2, tm=tm_h)

    h = _ffn_half(h, mod_h[1], norm_g[1, 0], *ffn_w, (1, 0), k=0, tm=tm_h)
    h = _conv_mixer(h.reshape(b, l, d), mod_h[1], norm_g[1, 1], conv_w_in, conv_w, conv_w_out, tm_h)
    h = _ffn_half(h.reshape(b * l, d), mod_h[1], norm_g[1, 2], *ffn_w, (1, 1), k=2, tm=tm_h,
                  final_gain=final_norm_g)
    return h.reshape(b, l, d)
```

```python
import functools
import math

import jax
import jax.numpy as jnp
from jax import lax
from jax.experimental import pallas as pl
from jax.experimental.pallas import tpu as pltpu

D_MODEL = 1024
SEQ = 2048
GRID_W = 64
CTX_LEN = 256
RMS_EPS = 1e-6
N_MOD = 9
D_FF = 2816
POOL_WINDOWS = (2, 4, 8, 16)
POOL_DIM = D_MODEL // 2
POOL_GROUP_DIM = POOL_DIM // len(POOL_WINDOWS)
MLA_HEADS = D_MODEL // 128
QK_NOPE_DIM = 64
QK_ROPE_DIM = 32
QK_HEAD_DIM = QK_NOPE_DIM + QK_ROPE_DIM
V_HEAD_DIM = 64
Q_LORA_RANK = 768
KV_LORA_RANK = 256
ROPE_AXIS_DIM = QK_ROPE_DIM // 2
ROPE_THETA = 10000.0
ATTN_SCALE = 1.0 / math.sqrt(QK_HEAD_DIM)
ATTN_V_DIM = MLA_HEADS * V_HEAD_DIM

LANES = 128
SUBLANES = 8
HEAD_PAD = LANES
QK_PAD_DIM = MLA_HEADS * HEAD_PAD
MOD_ROWS = 16
MOD_TILE_N = 1536
FF_CHUNK = 256
N_FF_CHUNKS = D_FF // FF_CHUNK
FF_HEAD_ROWS = 256
FF_TAIL_ROWS = 256
W_IN_PAD = 1664
HALO = SUBLANES
PV_WIDTH = 256
PV_HEADS = PV_WIDTH // V_HEAD_DIM
ATTN_SUB_Q = 256
VMEM_LIMIT = 56 * 1024 * 1024

BF16 = jnp.bfloat16
F32 = jnp.float32


def _sigmoid(x):
    return 1.0 / (1.0 + jnp.exp(-x))


def _rms(x):
    return x * lax.rsqrt(jnp.mean(x * x, axis=-1, keepdims=True) + RMS_EPS)


def _adaln(x, gain, mod_ref, k):
    shift = mod_ref[0, 3 * k:3 * k + 1, :]
    scale = mod_ref[0, 3 * k + 1:3 * k + 2, :]
    return _rms(x) * gain * (1.0 + scale) + shift


_NT = (((1,), (1,)), ((), ()))


def _mm(a, w):
    return lax.dot_general(a, w, (((1,), (0,)), ((), ())), preferred_element_type=F32)


def _params(semantics):
    return pltpu.CompilerParams(dimension_semantics=semantics, vmem_limit_bytes=VMEM_LIMIT)


def _mod_kernel(cond_ref, w_ref, b_ref, o_ref):
    cond = cond_ref[...]
    a = (cond * _sigmoid(cond)).astype(BF16)
    w = w_ref[0].astype(BF16)
    o_ref[0] = jnp.dot(a, w, preferred_element_type=F32) + b_ref[0]


def _modulation(cond, w_mod, b_mod):
    depth, d, n = w_mod.shape
    return pl.pallas_call(
        _mod_kernel,
        grid=(depth, n // MOD_TILE_N),
        in_specs=[
            pl.BlockSpec((MOD_ROWS, d), lambda i, j: (0, 0)),
            pl.BlockSpec((1, d, MOD_TILE_N), lambda i, j: (i, 0, j)),
            pl.BlockSpec((1, 1, MOD_TILE_N), lambda i, j: (i, 0, j)),
        ],
        out_specs=pl.BlockSpec((1, MOD_ROWS, MOD_TILE_N), lambda i, j: (i, 0, j)),
        out_shape=jax.ShapeDtypeStruct((depth, MOD_ROWS, n), F32),
        compiler_params=_params(("arbitrary", "arbitrary")),
        name="modulation",
    )(cond, w_mod, b_mod.reshape(depth, 1, n))


def _ffn_kernel(*refs, k, final, sel):
    if final:
        (s_ref, mod_ref, gain_ref, wg_hbm, wu_hbm, wd_hbm, fg_ref, o_ref,
         u_scr, a_scr, wg_ref, wu_ref, wd_ref, sem) = refs
    else:
        (s_ref, mod_ref, gain_ref, wg_hbm, wu_hbm, wd_hbm, o_ref,
         u_scr, a_scr, wg_ref, wu_ref, wd_ref, sem) = refs
    tm = s_ref.shape[0]

    def chunk_copies(f):
        cols = slice(f * FF_CHUNK, (f + 1) * FF_CHUNK)
        return (pltpu.make_async_copy(wg_hbm.at[sel[0], sel[1], :, cols], wg_ref.at[:, cols], sem.at[2 * f]),
                pltpu.make_async_copy(wu_hbm.at[sel[0], sel[1], :, cols], wu_ref.at[:, cols], sem.at[2 * f + 1]))

    def down_copy():
        return pltpu.make_async_copy(wd_hbm.at[sel[0], sel[1]], wd_ref, sem.at[2 * N_FF_CHUNKS])

    def hidden(rows, f):
        cols = slice(f * FF_CHUNK, (f + 1) * FF_CHUNK)
        g = _mm(u_scr[rows, :], wg_ref[:, cols])
        up = _mm(u_scr[rows, :], wu_ref[:, cols])
        a_scr[rows, cols] = (g * _sigmoid(g) * up).astype(BF16)

    def body(first_step):
        if first_step:
            for f in range(N_FF_CHUNKS):
                for cp in chunk_copies(f):
                    cp.start()
            down_copy().start()
        for p in range(tm // FF_HEAD_ROWS):
            rows = slice(p * FF_HEAD_ROWS, (p + 1) * FF_HEAD_ROWS)
            u_scr[rows, :] = _adaln(s_ref[rows, :], gain_ref[...], mod_ref, k).astype(BF16)
            if first_step and p == 0:
                for cp in chunk_copies(0):
                    cp.wait()
            hidden(rows, 0)
        for f in range(1, N_FF_CHUNKS):
            if first_step:
                for cp in chunk_copies(f):
                    cp.wait()
            hidden(slice(None), f)
        if first_step:
            down_copy().wait()
        gate = mod_ref[0, 3 * k + 2:3 * k + 3, :]
        for p in range(tm // FF_TAIL_ROWS):
            rows = slice(p * FF_TAIL_ROWS, (p + 1) * FF_TAIL_ROWS)
            y = s_ref[rows, :] + 0.5 * gate * _mm(a_scr[rows, :], wd_ref[...])
            if final:
                y = _rms(y) * fg_ref[...]
            o_ref[rows, :] = y

    pl.when(pl.program_id(0) == 0)(functools.partial(body, True))
    pl.when(pl.program_id(0) != 0)(functools.partial(body, False))


def _ffn_half(s, mod, gain, wg, wu, wd, sel, k, tm, final_gain=None):
    n_tok, d = s.shape
    tiles_per_batch = n_tok // mod.shape[0] // tm
    final = final_gain is not None
    hbm = pl.BlockSpec(memory_space=pl.ANY)
    in_specs = [
        pl.BlockSpec((tm, d), lambda i: (i, 0)),
        pl.BlockSpec((1, N_MOD, d), lambda i: (i // tiles_per_batch, 0, 0)),
        pl.BlockSpec((1, d), lambda i: (0, 0)),
        hbm, hbm, hbm,
    ]
    args = [s, mod, gain.reshape(1, d), wg, wu, wd]
    if final:
        in_specs.append(pl.BlockSpec((1, d), lambda i: (0, 0)))
        args.append(final_gain.reshape(1, d))
    return pl.pallas_call(
        functools.partial(_ffn_kernel, k=k, final=final, sel=sel),
        grid=(n_tok // tm,),
        in_specs=in_specs,
        out_specs=pl.BlockSpec((tm, d), lambda i: (i, 0)),
        out_shape=jax.ShapeDtypeStruct((n_tok, d), F32),
        scratch_shapes=[pltpu.VMEM((tm, d), BF16), pltpu.VMEM((tm, D_FF), BF16),
                        pltpu.VMEM(wg.shape[2:], F32), pltpu.VMEM(wu.shape[2:], F32),
                        pltpu.VMEM(wd.shape[2:], F32),
                        pltpu.SemaphoreType.DMA((2 * N_FF_CHUNKS + 1,))],
        compiler_params=_params(("arbitrary",)),
        name="ffn_half",
    )(*args)


def _rope(z, c_ref, s1_ref, s2_ref):
    fwd = pltpu.roll(z, HEAD_PAD - ROPE_AXIS_DIM // 2, axis=1)
    bwd = pltpu.roll(z, ROPE_AXIS_DIM // 2, axis=1)
    return z * c_ref[...] + fwd * s1_ref[...] + bwd * s2_ref[...]


def _mix_in_kernel(*refs, latent):
    if latent:
        (s_ref, mod_ref, gain_ref, w_in_ref, qg_ref, wq_ref, kvg_ref, wk_ref, wv_ref, place_ref,
         c_ref, s1_ref, s2_ref, pool_ref, q_ref, kt_ref, v_ref) = refs
    else:
        (s_ref, mod_ref, gain_ref, w_in_ref, kvg_ref, wk_ref, wv_ref, place_ref,
         kt_ref, v_ref) = refs
    u = _adaln(s_ref[...], gain_ref[...], mod_ref, 1).astype(BF16)
    z = jnp.dot(u, w_in_ref[...], preferred_element_type=F32)
    cuts = (POOL_DIM, POOL_DIM + Q_LORA_RANK, POOL_DIM + Q_LORA_RANK + KV_LORA_RANK)
    ckv = (_rms(z[:, cuts[1]:cuts[2]]) * kvg_ref[...]).astype(BF16)
    kr = pltpu.roll(z[:, cuts[2]:], QK_NOPE_DIM, axis=1)
    if latent:
        pool_ref[...] = z[:, :cuts[0]]
        cq = (_rms(z[:, cuts[0]:cuts[1]]) * qg_ref[...]).astype(BF16)
        q = jnp.dot(cq, wq_ref[...], preferred_element_type=F32)
        for h in range(MLA_HEADS):
            sl = slice(h * HEAD_PAD, (h + 1) * HEAD_PAD)
            q_ref[:, sl] = _rope(q[:, sl], c_ref, s1_ref, s2_ref).astype(BF16)
        kr = _rope(kr, c_ref, s1_ref, s2_ref)
    kt = lax.dot_general(wk_ref[...], ckv, _NT, preferred_element_type=F32)
    kt = kt + lax.dot_general(place_ref[...], kr.astype(BF16), _NT, preferred_element_type=F32)
    kt_ref[0] = kt.astype(BF16)
    v_ref[...] = jnp.dot(ckv, wv_ref[...], preferred_element_type=F32).astype(BF16)


def _mix_in(s, mod, gain, w_in, kvg, wk, wv, place, tm, latent_args=None):
    n_tok, d = s.shape
    tiles_per_batch = n_tok // mod.shape[0] // tm
    latent = latent_args is not None
    const = lambda a: pl.BlockSpec(a.shape, lambda i: (0,) * a.ndim)
    tok = lambda w: pl.BlockSpec((tm, w), lambda i: (i, 0))
    in_specs = [tok(d), pl.BlockSpec((1, N_MOD, d), lambda i: (i // tiles_per_batch, 0, 0)),
                pl.BlockSpec((1, d), lambda i: (0, 0)), const(w_in)]
    args = [s, mod, gain.reshape(1, d), w_in]
    if latent:
        qg, wq, tabs = latent_args
        in_specs += [const(qg), const(wq)]
        args += [qg, wq]
    in_specs += [const(kvg), const(wk), const(wv), const(place)]
    args += [kvg, wk, wv, place]
    kt_spec = pl.BlockSpec((1, QK_PAD_DIM, tm), lambda i: (i // tiles_per_batch, 0, i % tiles_per_batch))
    out_specs = [kt_spec, tok(ATTN_V_DIM)]
    out_shape = [jax.ShapeDtypeStruct((mod.shape[0], QK_PAD_DIM, n_tok // mod.shape[0]), BF16),
                 jax.ShapeDtypeStruct((n_tok, ATTN_V_DIM), BF16)]
    if latent:
        rope_spec = pl.BlockSpec((tm, HEAD_PAD), lambda i: (i % tiles_per_batch, 0))
        in_specs += [rope_spec] * 3
        args += list(tabs)
        out_specs = [tok(POOL_DIM), tok(QK_PAD_DIM)] + out_specs
        out_shape = [jax.ShapeDtypeStruct((n_tok, POOL_DIM), F32),
                     jax.ShapeDtypeStruct((n_tok, QK_PAD_DIM), BF16)] + out_shape
    return pl.pallas_call(
        functools.partial(_mix_in_kernel, latent=latent),
        grid=(n_tok // tm,),
        in_specs=in_specs,
        out_specs=out_specs,
        out_shape=out_shape,
        compiler_params=_params(("arbitrary",)),
        name="mix_in_latent" if latent else "mix_in_context",
    )(*args)


def _attn_kernel(q_ref, kth_ref, ktg_ref, vh_ref, vg_ref, o_ref):
    c = ATTN_SCALE * math.log2(math.e)
    lane_head = lax.broadcasted_iota(jnp.int32, (1, PV_WIDTH), 1) // V_HEAD_DIM
    for sub in range(q_ref.shape[1] // ATTN_SUB_Q):
        rows = slice(sub * ATTN_SUB_Q, (sub + 1) * ATTN_SUB_Q)
        for grp in range(MLA_HEADS // PV_HEADS):
            vcols = slice(grp * PV_WIDTH, (grp + 1) * PV_WIDTH)
            acc = None
            for hh in range(PV_HEADS):
                h = grp * PV_HEADS + hh
                qk = slice(h * HEAD_PAD, (h + 1) * HEAD_PAD)
                q = q_ref[0, rows, qk]
                s_h = jnp.dot(q, kth_ref[0, qk, :], preferred_element_type=F32)
                s_g = jnp.dot(q, ktg_ref[0, qk, :], preferred_element_type=F32)
                m = jnp.maximum(jnp.max(s_h, axis=-1, keepdims=True),
                                jnp.max(s_g, axis=-1, keepdims=True))
                e_h = jnp.exp2((s_h - m) * c)
                e_g = jnp.exp2((s_g - m) * c)
                denom = jnp.sum(e_h, axis=-1, keepdims=True) + jnp.sum(e_g, axis=-1, keepdims=True)
                res = jnp.dot(e_h.astype(BF16), vh_ref[0, :, vcols], preferred_element_type=F32)
                res = res + jnp.dot(e_g.astype(BF16), vg_ref[0, :, vcols], preferred_element_type=F32)
                term = jnp.where(lane_head == hh, res * (1.0 / denom), 0.0)
                acc = term if acc is None else acc + term
            o_ref[0, rows, vcols] = acc.astype(BF16)


def _attention(q, kt_h, kt_g, v_h, v_g, tq):
    b, l, _ = q.shape
    t_g = kt_g.shape[2]
    return pl.pallas_call(
        _attn_kernel,
        grid=(b, l // tq),
        in_specs=[
            pl.BlockSpec((1, tq, QK_PAD_DIM), lambda i, j: (i, j, 0)),
            pl.BlockSpec((1, QK_PAD_DIM, l), lambda i, j: (i, 0, 0)),
            pl.BlockSpec((1, QK_PAD_DIM, t_g), lambda i, j: (i, 0, 0)),
            pl.BlockSpec((1, l, ATTN_V_DIM), lambda i, j: (i, 0, 0)),
            pl.BlockSpec((1, t_g, ATTN_V_DIM), lambda i, j: (i, 0, 0)),
        ],
        out_specs=pl.BlockSpec((1, tq, ATTN_V_DIM), lambda i, j: (i, j, 0)),
        out_shape=jax.ShapeDtypeStruct((b, l, ATTN_V_DIM), BF16),
        compiler_params=_params(("arbitrary", "arbitrary")),
        name="latent_attention",
    )(q, kt_h, kt_g, v_h, v_g)


def _mix_out_kernel(h_ref, mod_ref, pool_ref, attn_ref, pw_ref, ps_ref, wo_ref, o_ref, win_scr, y_scr,
                    *, tm, seq):
    t0 = pl.multiple_of(pl.program_id(1) * tm, tm)
    prev = pool_ref[0, pl.ds(pl.multiple_of(jnp.maximum(t0 - HALO, 0), HALO), HALO), :]
    nxt = pool_ref[0, pl.ds(pl.multiple_of(jnp.minimum(t0 + tm, seq - HALO), HALO), HALO), :]
    win_scr[0:HALO, :] = jnp.where(t0 > 0, prev, 0.0)
    win_scr[HALO:HALO + tm, :] = pool_ref[0, pl.ds(t0, tm), :]
    win_scr[HALO + tm:, :] = jnp.where(t0 + tm < seq, nxt, 0.0)
    t = (t0 + lax.broadcasted_iota(jnp.int32, (tm, 1), 0)).astype(F32)
    n = tm + 2 * HALO
    for g, w in enumerate(POOL_WINDOWS):
        lanes = slice(g * POOL_GROUP_DIM, (g + 1) * POOL_GROUP_DIM)
        fwd = win_scr[:, lanes]
        span = 1
        while span < w:
            fwd = fwd + pltpu.roll(fwd, n - span, axis=0)
            span *= 2
        total = pltpu.roll(fwd, w // 2, axis=0)[HALO:HALO + tm]
        cnt = jnp.minimum(t, float(w // 2)) + jnp.minimum(float(seq - 1) - t, float(w - w // 2 - 1)) + 1.0
        p = total / cnt - win_scr[HALO:HALO + tm, lanes]
        y = _mm(p.astype(BF16), pw_ref[g])
        y_scr[:, lanes] = (y * ps_ref[:, lanes]).astype(BF16)
    out = _mm(y_scr[...], wo_ref[:POOL_DIM, :]) + _mm(attn_ref[0], wo_ref[POOL_DIM:, :])
    gate = mod_ref[0, 5:6, :]
    o_ref[0] = h_ref[0] + gate * out


def _mix_out(h, mod, pool, attn, pool_w, pool_scale, w_out, tm):
    b, l, d = h.shape
    const = lambda a: pl.BlockSpec((None,) + a.shape[1:], lambda i, j: (0,) * a.ndim,
                                   pipeline_mode=pl.Buffered(1))
    return pl.pallas_call(
        functools.partial(_mix_out_kernel, tm=tm, seq=l),
        grid=(b, l // tm),
        in_specs=[
            pl.BlockSpec((1, tm, d), lambda i, j: (i, j, 0)),
            pl.BlockSpec((1, N_MOD, d), lambda i, j: (i, 0, 0)),
            pl.BlockSpec((1, l, POOL_DIM), lambda i, j: (i, 0, 0)),
            pl.BlockSpec((1, tm, ATTN_V_DIM), lambda i, j: (i, j, 0)),
            const(pool_w), const(pool_scale), const(w_out),
        ],
        out_specs=pl.BlockSpec((1, tm, d), lambda i, j: (i, j, 0)),
        out_shape=jax.ShapeDtypeStruct((b, l, d), F32),
        scratch_shapes=[pltpu.VMEM((tm + 2 * HALO, POOL_DIM), F32), pltpu.VMEM((tm, POOL_DIM), BF16)],
        compiler_params=_params(("arbitrary", "arbitrary")),
        name="mix_out",
    )(h, mod, pool, attn, pool_w, pool_scale, w_out)


def _conv_kernel(h_ref, hp_ref, hn_ref, mod_ref, gain_ref, w_in_ref, cw_ref, w_out_ref, o_ref,
                 u_scr, z_scr, *, tm, seq):
    t0 = pl.program_id(1) * tm
    gain = gain_ref[...]
    u_scr[0:HALO, :] = _adaln(hp_ref[0], gain, mod_ref, 1).astype(BF16)
    u_scr[HALO:HALO + tm, :] = _adaln(h_ref[0], gain, mod_ref, 1).astype(BF16)
    u_scr[HALO + tm:, :] = _adaln(hn_ref[0], gain, mod_ref, 1).astype(BF16)
    d = h_ref.shape[-1]
    cv = _mm(u_scr[...], w_in_ref[:, d:])
    t = t0 - HALO + lax.broadcasted_iota(jnp.int32, (tm + 2 * HALO, 1), 0)
    inside = jnp.logical_and(t >= 0, t < seq)
    z_scr[...] = jnp.where(inside, cv[:, :d] * cv[:, d:], 0.0)
    y = (cw_ref[0:1, :] * z_scr[HALO - 1:HALO - 1 + tm, :]
         + cw_ref[1:2, :] * z_scr[HALO:HALO + tm, :]
         + cw_ref[2:3, :] * z_scr[HALO + 1:HALO + 1 + tm, :])
    bg = _mm(u_scr[HALO:HALO + tm, :], w_in_ref[:, :d])
    out = _mm((bg * y).astype(BF16), w_out_ref[...])
    gate = mod_ref[0, 5:6, :]
    o_ref[0] = h_ref[0] + gate * out


def _conv_mixer(h, mod, gain, w_in, conv_w, w_out, tm):
    b, l, d = h.shape
    hb = tm // HALO
    const = lambda a: pl.BlockSpec((None,) + a.shape[1:], lambda i, j: (0,) * a.ndim,
                                   pipeline_mode=pl.Buffered(1))
    return pl.pallas_call(
        functools.partial(_conv_kernel, tm=tm, seq=l),
        grid=(b, l // tm),
        in_specs=[
            pl.BlockSpec((1, tm, d), lambda i, j: (i, j, 0)),
            pl.BlockSpec((1, HALO, d), lambda i, j: (i, jnp.maximum(j * hb - 1, 0), 0)),
            pl.BlockSpec((1, HALO, d), lambda i, j: (i, jnp.minimum((j + 1) * hb, l // HALO - 1), 0)),
            pl.BlockSpec((1, N_MOD, d), lambda i, j: (i, 0, 0)),
            pl.BlockSpec((1, d), lambda i, j: (0, 0)),
            const(w_in), const(conv_w), const(w_out),
        ],
        out_specs=pl.BlockSpec((1, tm, d), lambda i, j: (i, j, 0)),
        out_shape=jax.ShapeDtypeStruct((b, l, d), F32),
        scratch_shapes=[pltpu.VMEM((tm + 2 * HALO, d), BF16), pltpu.VMEM((tm + 2 * HALO, d), F32)],
        compiler_params=_params(("arbitrary", "arbitrary")),
        name="conv_mixer",
    )(h, h, h, mod, gain.reshape(1, d), w_in, conv_w, w_out)


def _head_pad(w, per_head, start, width):
    r = w.shape[0]
    w = w.reshape(r, MLA_HEADS, per_head)[:, :, start:start + width]
    return jnp.pad(w, ((0, 0), (0, 0), (0, HEAD_PAD - width))).reshape(r, QK_PAD_DIM)


def _rope_tables(length):
    pos = jnp.arange(length)
    row = (pos // GRID_W).astype(F32)
    col = (pos % GRID_W).astype(F32)
    half = ROPE_AXIS_DIM // 2
    freqs = jnp.power(ROPE_THETA, -jnp.arange(0, ROPE_AXIS_DIM, 2, dtype=F32) / ROPE_AXIS_DIM)
    lane = jnp.arange(HEAD_PAD)
    o = lane - QK_NOPE_DIM
    rotary = jnp.logical_and(o >= 0, o < QK_ROPE_DIM)
    o = jnp.clip(o, 0, QK_ROPE_DIM - 1)
    ang = jnp.where((o // ROPE_AXIS_DIM == 0)[None, :], row[:, None], col[:, None]) * freqs[o % half][None, :]
    first = (o % ROPE_AXIS_DIM) < half
    cos = jnp.where(rotary[None, :], jnp.cos(ang), 1.0)
    sin = jnp.where(rotary[None, :], jnp.sin(ang), 0.0)
    return cos, jnp.where(first[None, :], -sin, 0.0), jnp.where(first[None, :], 0.0, sin)


def kernel(x, c, ctx, c_ctx, norm_g, w_mod, b_mod, ffn_w_gate, ffn_w_up, ffn_w_down, ab_w_in, pool_w,
           pool_scale, q_norm_g, w_uq, kv_norm_g, w_ukv, ab_w_out, conv_w_in, conv_w, conv_w_out,
           final_norm_g):
    b, l, d = x.shape
    t_g = ctx.shape[1]
    tm_h, tm_g, tq = 512, 256, 512

    cond = jnp.zeros((MOD_ROWS, d), F32).at[:b].set(c).at[b].set(c_ctx)
    m = _modulation(cond, w_mod, b_mod)
    mod_h = [m[i, :b].reshape(b, N_MOD, d) for i in range(2)]
    mod_g = jnp.broadcast_to(m[0, b].reshape(1, N_MOD, d), (b, N_MOD, d))

    ffn_w = (ffn_w_gate, ffn_w_up, ffn_w_down)

    h = x.reshape(b * l, d)
    g = ctx.reshape(b * t_g, d)

    h = _ffn_half(h, mod_h[0], norm_g[0, 0], *ffn_w, (0, 0), k=0, tm=tm_h)
    g = _ffn_half(g, mod_g, norm_g[0, 0], *ffn_w, (0, 0), k=0, tm=tm_g)

    w_in = jnp.pad(ab_w_in[0], ((0, 0), (0, W_IN_PAD - ab_w_in.shape[-1]))).astype(BF16)
    wq = _head_pad(w_uq[0], QK_HEAD_DIM, 0, QK_HEAD_DIM).astype(BF16)
    wk = _head_pad(w_ukv[0], QK_NOPE_DIM + V_HEAD_DIM, 0, QK_NOPE_DIM).T.astype(BF16)
    wv = w_ukv[0].reshape(KV_LORA_RANK, MLA_HEADS, QK_NOPE_DIM + V_HEAD_DIM)[:, :, QK_NOPE_DIM:]
    wv = wv.reshape(KV_LORA_RANK, ATTN_V_DIM).astype(BF16)
    lane = jnp.arange(HEAD_PAD)
    rotary = jnp.logical_and(lane >= QK_NOPE_DIM, lane < QK_HEAD_DIM)
    place = jnp.tile(jnp.where(rotary[:, None], jnp.eye(HEAD_PAD, dtype=F32), 0.0), (MLA_HEADS, 1)).astype(BF16)
    qg = q_norm_g[0].reshape(1, Q_LORA_RANK)
    kvg = kv_norm_g[0].reshape(1, KV_LORA_RANK)
    tabs = _rope_tables(l)

    pool, q, k_h, v_h = _mix_in(h, mod_h[0], norm_g[0, 1], w_in, kvg, wk, wv, place, tm_h,
                                latent_args=(qg, wq, tabs))
    k_g, v_g = _mix_in(g, mod_g, norm_g[0, 1], w_in, kvg, wk, wv, place, tm_g)
    attn = _attention(q.reshape(b, l, -1), k_h, k_g, v_h.reshape(b, l, -1), v_g.reshape(b, t_g, -1), tq)
    h = _mix_out(h.reshape(b, l, d), mod_h[0], pool.reshape(b, l, -1), attn, pool_w,
                 pool_scale.reshape(-1, 1, POOL_DIM), ab_w_out, tm_h)
    h = _ffn_half(h.reshape(b * l, d), mod_h[0], norm_g[0, 2], *ffn_w, (0, 1), k=2, tm=tm_h)

    h = _ffn_half(h, mod_h[1], norm_g[1, 0], *ffn_w, (1, 0), k=0, tm=tm_h)
    h = _conv_mixer(h.reshape(b, l, d), mod_h[1], norm_g[1, 1], conv_w_in, conv_w, conv_w_out, tm_h)
    h = _ffn_half(h.reshape(b * l, d), mod_h[1], norm_g[1, 2], *ffn_w, (1, 1), k=2, tm=tm_h,
                  final_gain=final_norm_g)
    return h.reshape(b, l, d)
```

```python
import functools
import math

import jax
import jax.numpy as jnp
import numpy as np
from jax import lax
from jax.experimental import pallas as pl
from jax.experimental.pallas import tpu as pltpu

D_MODEL = 1024
SEQ = 2048
GRID_W = 64
CTX_LEN = 256
RMS_EPS = 1e-6
N_MOD = 9
D_FF = 2816
POOL_WINDOWS = (2, 4, 8, 16)
POOL_DIM = D_MODEL // 2
POOL_GROUP_DIM = POOL_DIM // len(POOL_WINDOWS)
MLA_HEADS = D_MODEL // 128
QK_NOPE_DIM = 64
QK_ROPE_DIM = 32
QK_HEAD_DIM = QK_NOPE_DIM + QK_ROPE_DIM
V_HEAD_DIM = 64
Q_LORA_RANK = 768
KV_LORA_RANK = 256
ROPE_AXIS_DIM = QK_ROPE_DIM // 2
ROPE_THETA = 10000.0
ATTN_SCALE = 1.0 / math.sqrt(QK_HEAD_DIM)
ATTN_V_DIM = MLA_HEADS * V_HEAD_DIM

LANES = 128
SUBLANES = 8
HEAD_PAD = LANES
QK_PAD_DIM = MLA_HEADS * HEAD_PAD
MOD_ROWS = 16
MOD_TILE_N = 1536
FF_CHUNK = 256
N_FF_CHUNKS = D_FF // FF_CHUNK
FF_HEAD_ROWS = 256
FF_TAIL_ROWS = 256
HALO = SUBLANES
PV_WIDTH = 256
PV_HEADS = PV_WIDTH // V_HEAD_DIM
ATTN_SUB_Q = 256
VMEM_LIMIT = 56 * 1024 * 1024

BF16 = jnp.bfloat16
F32 = jnp.float32


def _sigmoid(x):
    return 1.0 / (1.0 + jnp.exp(-x))


def _rms(x):
    return x * lax.rsqrt(jnp.mean(x * x, axis=-1, keepdims=True) + RMS_EPS)


def _adaln(x, gain, mod_ref, k):
    shift = mod_ref[0, 3 * k:3 * k + 1, :]
    scale = mod_ref[0, 3 * k + 1:3 * k + 2, :]
    return _rms(x) * gain * (1.0 + scale) + shift


_NT = (((1,), (1,)), ((), ()))


def _mm(a, w):
    return lax.dot_general(a, w, (((1,), (0,)), ((), ())), preferred_element_type=F32)


def _params(semantics):
    return pltpu.CompilerParams(dimension_semantics=semantics, vmem_limit_bytes=VMEM_LIMIT)


def _mod_kernel(cond_ref, w_ref, b_ref, o_ref):
    cond = cond_ref[...]
    a = (cond * _sigmoid(cond)).astype(BF16)
    w = w_ref[0].astype(BF16)
    o_ref[0] = jnp.dot(a, w, preferred_element_type=F32) + b_ref[0]


def _modulation(cond, w_mod, b_mod):
    depth, d, n = w_mod.shape
    return pl.pallas_call(
        _mod_kernel,
        grid=(depth, n // MOD_TILE_N),
        in_specs=[
            pl.BlockSpec((MOD_ROWS, d), lambda i, j: (0, 0)),
            pl.BlockSpec((1, d, MOD_TILE_N), lambda i, j: (i, 0, j)),
            pl.BlockSpec((1, 1, MOD_TILE_N), lambda i, j: (i, 0, j)),
        ],
        out_specs=pl.BlockSpec((1, MOD_ROWS, MOD_TILE_N), lambda i, j: (i, 0, j)),
        out_shape=jax.ShapeDtypeStruct((depth, MOD_ROWS, n), F32),
        compiler_params=_params(("arbitrary", "arbitrary")),
        name="modulation",
    )(cond, w_mod, b_mod.reshape(depth, 1, n))


def _ffn_kernel(*refs, k, final, sel):
    if final:
        (s_ref, mod_ref, gain_ref, wg_hbm, wu_hbm, wd_hbm, fg_ref, o_ref,
         u_scr, a_scr, wg_ref, wu_ref, wd_ref, sem) = refs
    else:
        (s_ref, mod_ref, gain_ref, wg_hbm, wu_hbm, wd_hbm, o_ref,
         u_scr, a_scr, wg_ref, wu_ref, wd_ref, sem) = refs
    tm = s_ref.shape[0]

    def chunk_copies(f):
        cols = slice(f * FF_CHUNK, (f + 1) * FF_CHUNK)
        return (pltpu.make_async_copy(wg_hbm.at[sel[0], sel[1], :, cols], wg_ref.at[:, cols], sem.at[2 * f]),
                pltpu.make_async_copy(wu_hbm.at[sel[0], sel[1], :, cols], wu_ref.at[:, cols], sem.at[2 * f + 1]))

    def down_copy():
        return pltpu.make_async_copy(wd_hbm.at[sel[0], sel[1]], wd_ref, sem.at[2 * N_FF_CHUNKS])

    def hidden(rows, f):
        cols = slice(f * FF_CHUNK, (f + 1) * FF_CHUNK)
        g = _mm(u_scr[rows, :], wg_ref[:, cols])
        up = _mm(u_scr[rows, :], wu_ref[:, cols])
        a_scr[rows, cols] = (g * _sigmoid(g) * up).astype(BF16)

    def body(first_step):
        if first_step:
            for f in range(N_FF_CHUNKS):
                for cp in chunk_copies(f):
                    cp.start()
            down_copy().start()
        for p in range(tm // FF_HEAD_ROWS):
            rows = slice(p * FF_HEAD_ROWS, (p + 1) * FF_HEAD_ROWS)
            u_scr[rows, :] = _adaln(s_ref[rows, :], gain_ref[...], mod_ref, k).astype(BF16)
            if first_step and p == 0:
                for cp in chunk_copies(0):
                    cp.wait()
            hidden(rows, 0)
        for f in range(1, N_FF_CHUNKS):
            if first_step:
                for cp in chunk_copies(f):
                    cp.wait()
            hidden(slice(None), f)
        if first_step:
            down_copy().wait()
        gate = mod_ref[0, 3 * k + 2:3 * k + 3, :]
        for p in range(tm // FF_TAIL_ROWS):
            rows = slice(p * FF_TAIL_ROWS, (p + 1) * FF_TAIL_ROWS)
            y = s_ref[rows, :] + 0.5 * gate * _mm(a_scr[rows, :], wd_ref[...])
            if final:
                y = _rms(y) * fg_ref[...]
            o_ref[rows, :] = y

    pl.when(pl.program_id(0) == 0)(functools.partial(body, True))
    pl.when(pl.program_id(0) != 0)(functools.partial(body, False))


def _ffn_half(s, mod, gain, wg, wu, wd, sel, k, tm, final_gain=None):
    n_tok, d = s.shape
    tiles_per_batch = n_tok // mod.shape[0] // tm
    final = final_gain is not None
    hbm = pl.BlockSpec(memory_space=pl.ANY)
    in_specs = [
        pl.BlockSpec((tm, d), lambda i: (i, 0)),
        pl.BlockSpec((1, N_MOD, d), lambda i: (i // tiles_per_batch, 0, 0)),
        pl.BlockSpec((1, d), lambda i: (0, 0)),
        hbm, hbm, hbm,
    ]
    args = [s, mod, gain.reshape(1, d), wg, wu, wd]
    if final:
        in_specs.append(pl.BlockSpec((1, d), lambda i: (0, 0)))
        args.append(final_gain.reshape(1, d))
    return pl.pallas_call(
        functools.partial(_ffn_kernel, k=k, final=final, sel=sel),
        grid=(n_tok // tm,),
        in_specs=in_specs,
        out_specs=pl.BlockSpec((tm, d), lambda i: (i, 0)),
        out_shape=jax.ShapeDtypeStruct((n_tok, d), F32),
        scratch_shapes=[pltpu.VMEM((tm, d), BF16), pltpu.VMEM((tm, D_FF), BF16),
                        pltpu.VMEM(wg.shape[2:], F32), pltpu.VMEM(wu.shape[2:], F32),
                        pltpu.VMEM(wd.shape[2:], F32),
                        pltpu.SemaphoreType.DMA((2 * N_FF_CHUNKS + 1,))],
        compiler_params=_params(("arbitrary",)),
        name="ffn_half",
    )(*args)


def _rope(z, c_ref, s1_ref, s2_ref):
    fwd = pltpu.roll(z, HEAD_PAD - ROPE_AXIS_DIM // 2, axis=1)
    bwd = pltpu.roll(z, ROPE_AXIS_DIM // 2, axis=1)
    return z * c_ref[...] + fwd * s1_ref[...] + bwd * s2_ref[...]


def _mix_in_kernel(*refs, latent):
    if latent:
        (s_ref, mod_ref, gain_ref, w_in_ref, w_kr_ref, qg_ref, wq_ref, kvg_ref, wk_ref, wv_ref, place_ref,
         c_ref, s1_ref, s2_ref, pool_ref, q_ref, kt_ref, v_ref) = refs
    else:
        (s_ref, mod_ref, gain_ref, w_in_ref, w_kr_ref, kvg_ref, wk_ref, wv_ref, place_ref,
         kt_ref, v_ref) = refs
    u = _adaln(s_ref[...], gain_ref[...], mod_ref, 1).astype(BF16)
    cuts = (POOL_DIM, POOL_DIM + Q_LORA_RANK, POOL_DIM + Q_LORA_RANK + KV_LORA_RANK)
    z = _mm(u, w_in_ref[:, :cuts[2]])
    ckv = (_rms(z[:, cuts[1]:cuts[2]]) * kvg_ref[...]).astype(BF16)
    kr = pltpu.roll(_mm(u, w_kr_ref[...]), QK_NOPE_DIM, axis=1)
    if latent:
        pool_ref[...] = z[:, :cuts[0]]
        cq = (_rms(z[:, cuts[0]:cuts[1]]) * qg_ref[...]).astype(BF16)
        q = jnp.dot(cq, wq_ref[...], preferred_element_type=F32)
        for h in range(MLA_HEADS):
            sl = slice(h * HEAD_PAD, (h + 1) * HEAD_PAD)
            q_ref[:, sl] = _rope(q[:, sl], c_ref, s1_ref, s2_ref).astype(BF16)
        kr = _rope(kr, c_ref, s1_ref, s2_ref)
    kt = lax.dot_general(wk_ref[...], ckv, _NT, preferred_element_type=F32)
    kt = kt + lax.dot_general(place_ref[...], kr.astype(BF16), _NT, preferred_element_type=F32)
    kt_ref[0] = kt.astype(BF16)
    v_ref[...] = jnp.dot(ckv, wv_ref[...], preferred_element_type=F32).astype(BF16)


def _mix_in(s, mod, gain, w_in, w_kr, kvg, wk, wv, place, tm, latent_args=None):
    n_tok, d = s.shape
    tiles_per_batch = n_tok // mod.shape[0] // tm
    latent = latent_args is not None
    const = lambda a: pl.BlockSpec(a.shape, lambda i: (0,) * a.ndim)
    tok = lambda w: pl.BlockSpec((tm, w), lambda i: (i, 0))
    in_specs = [tok(d), pl.BlockSpec((1, N_MOD, d), lambda i: (i // tiles_per_batch, 0, 0)),
                pl.BlockSpec((1, d), lambda i: (0, 0)),
                pl.BlockSpec((None,) + w_in.shape[1:], lambda i: (0, 0, 0), pipeline_mode=pl.Buffered(1)),
                const(w_kr)]
    args = [s, mod, gain.reshape(1, d), w_in, w_kr]
    if latent:
        qg, wq, tabs = latent_args
        in_specs += [const(qg), const(wq)]
        args += [qg, wq]
    in_specs += [const(kvg), const(wk), const(wv), const(place)]
    args += [kvg, wk, wv, place]
    kt_spec = pl.BlockSpec((1, QK_PAD_DIM, tm), lambda i: (i // tiles_per_batch, 0, i % tiles_per_batch))
    out_specs = [kt_spec, tok(ATTN_V_DIM)]
    out_shape = [jax.ShapeDtypeStruct((mod.shape[0], QK_PAD_DIM, n_tok // mod.shape[0]), BF16),
                 jax.ShapeDtypeStruct((n_tok, ATTN_V_DIM), BF16)]
    if latent:
        rope_spec = pl.BlockSpec((tm, HEAD_PAD), lambda i: (i % tiles_per_batch, 0))
        in_specs += [rope_spec] * 3
        args += list(tabs)
        out_specs = [tok(POOL_DIM), tok(QK_PAD_DIM)] + out_specs
        out_shape = [jax.ShapeDtypeStruct((n_tok, POOL_DIM), F32),
                     jax.ShapeDtypeStruct((n_tok, QK_PAD_DIM), BF16)] + out_shape
    return pl.pallas_call(
        functools.partial(_mix_in_kernel, latent=latent),
        grid=(n_tok // tm,),
        in_specs=in_specs,
        out_specs=out_specs,
        out_shape=out_shape,
        compiler_params=_params(("arbitrary",)),
        name="mix_in_latent" if latent else "mix_in_context",
    )(*args)


def _attn_kernel(q_ref, kth_ref, ktg_ref, vh_ref, vg_ref, o_ref):
    c = ATTN_SCALE * math.log2(math.e)
    lane_head = lax.broadcasted_iota(jnp.int32, (1, PV_WIDTH), 1) // V_HEAD_DIM
    for sub in range(q_ref.shape[1] // ATTN_SUB_Q):
        rows = slice(sub * ATTN_SUB_Q, (sub + 1) * ATTN_SUB_Q)
        for grp in range(MLA_HEADS // PV_HEADS):
            vcols = slice(grp * PV_WIDTH, (grp + 1) * PV_WIDTH)
            acc = None
            for hh in range(PV_HEADS):
                h = grp * PV_HEADS + hh
                qk = slice(h * HEAD_PAD, (h + 1) * HEAD_PAD)
                q = q_ref[0, rows, qk]
                s_h = jnp.dot(q, kth_ref[0, qk, :], preferred_element_type=F32)
                s_g = jnp.dot(q, ktg_ref[0, qk, :], preferred_element_type=F32)
                m = jnp.maximum(jnp.max(s_h, axis=-1, keepdims=True),
                                jnp.max(s_g, axis=-1, keepdims=True))
                e_h = jnp.exp2((s_h - m) * c)
                e_g = jnp.exp2((s_g - m) * c)
                denom = jnp.sum(e_h, axis=-1, keepdims=True) + jnp.sum(e_g, axis=-1, keepdims=True)
                res = jnp.dot(e_h.astype(BF16), vh_ref[0, :, vcols], preferred_element_type=F32)
                res = res + jnp.dot(e_g.astype(BF16), vg_ref[0, :, vcols], preferred_element_type=F32)
                term = jnp.where(lane_head == hh, res * (1.0 / denom), 0.0)
                acc = term if acc is None else acc + term
            o_ref[0, rows, vcols] = acc.astype(BF16)


def _attention(q, kt_h, kt_g, v_h, v_g, tq):
    b, l, _ = q.shape
    t_g = kt_g.shape[2]
    return pl.pallas_call(
        _attn_kernel,
        grid=(b, l // tq),
        in_specs=[
            pl.BlockSpec((1, tq, QK_PAD_DIM), lambda i, j: (i, j, 0)),
            pl.BlockSpec((1, QK_PAD_DIM, l), lambda i, j: (i, 0, 0)),
            pl.BlockSpec((1, QK_PAD_DIM, t_g), lambda i, j: (i, 0, 0)),
            pl.BlockSpec((1, l, ATTN_V_DIM), lambda i, j: (i, 0, 0)),
            pl.BlockSpec((1, t_g, ATTN_V_DIM), lambda i, j: (i, 0, 0)),
        ],
        out_specs=pl.BlockSpec((1, tq, ATTN_V_DIM), lambda i, j: (i, j, 0)),
        out_shape=jax.ShapeDtypeStruct((b, l, ATTN_V_DIM), BF16),
        compiler_params=_params(("arbitrary", "arbitrary")),
        name="latent_attention",
    )(q, kt_h, kt_g, v_h, v_g)


def _mix_out_kernel(h_ref, mod_ref, pool_ref, attn_ref, pw_ref, ps_ref, wo_ref, o_ref, win_scr, y_scr,
                    *, tm, seq):
    t0 = pl.multiple_of(pl.program_id(1) * tm, tm)
    prev = pool_ref[0, pl.ds(pl.multiple_of(jnp.maximum(t0 - HALO, 0), HALO), HALO), :]
    nxt = pool_ref[0, pl.ds(pl.multiple_of(jnp.minimum(t0 + tm, seq - HALO), HALO), HALO), :]
    win_scr[0:HALO, :] = jnp.where(t0 > 0, prev, 0.0)
    win_scr[HALO:HALO + tm, :] = pool_ref[0, pl.ds(t0, tm), :]
    win_scr[HALO + tm:, :] = jnp.where(t0 + tm < seq, nxt, 0.0)
    t = (t0 + lax.broadcasted_iota(jnp.int32, (tm, 1), 0)).astype(F32)
    n = tm + 2 * HALO
    for g, w in enumerate(POOL_WINDOWS):
        lanes = slice(g * POOL_GROUP_DIM, (g + 1) * POOL_GROUP_DIM)
        fwd = win_scr[:, lanes]
        span = 1
        while span < w:
            fwd = fwd + pltpu.roll(fwd, n - span, axis=0)
            span *= 2
        total = pltpu.roll(fwd, w // 2, axis=0)[HALO:HALO + tm]
        cnt = jnp.minimum(t, float(w // 2)) + jnp.minimum(float(seq - 1) - t, float(w - w // 2 - 1)) + 1.0
        p = total / cnt - win_scr[HALO:HALO + tm, lanes]
        y = _mm(p.astype(BF16), pw_ref[g])
        y_scr[:, lanes] = (y * ps_ref[:, lanes]).astype(BF16)
    out = _mm(y_scr[...], wo_ref[:POOL_DIM, :]) + _mm(attn_ref[0], wo_ref[POOL_DIM:, :])
    gate = mod_ref[0, 5:6, :]
    o_ref[0] = h_ref[0] + gate * out


def _mix_out(h, mod, pool, attn, pool_w, pool_scale, w_out, tm):
    b, l, d = h.shape
    const = lambda a: pl.BlockSpec((None,) + a.shape[1:], lambda i, j: (0,) * a.ndim,
                                   pipeline_mode=pl.Buffered(1))
    return pl.pallas_call(
        functools.partial(_mix_out_kernel, tm=tm, seq=l),
        grid=(b, l // tm),
        in_specs=[
            pl.BlockSpec((1, tm, d), lambda i, j: (i, j, 0)),
            pl.BlockSpec((1, N_MOD, d), lambda i, j: (i, 0, 0)),
            pl.BlockSpec((1, l, POOL_DIM), lambda i, j: (i, 0, 0)),
            pl.BlockSpec((1, tm, ATTN_V_DIM), lambda i, j: (i, j, 0)),
            const(pool_w), const(pool_scale), const(w_out),
        ],
        out_specs=pl.BlockSpec((1, tm, d), lambda i, j: (i, j, 0)),
        out_shape=jax.ShapeDtypeStruct((b, l, d), F32),
        scratch_shapes=[pltpu.VMEM((tm + 2 * HALO, POOL_DIM), F32), pltpu.VMEM((tm, POOL_DIM), BF16)],
        compiler_params=_params(("arbitrary", "arbitrary")),
        name="mix_out",
    )(h, mod, pool, attn, pool_w, pool_scale, w_out)


def _conv_kernel(h_ref, hp_ref, hn_ref, mod_ref, gain_ref, w_in_ref, cw_ref, w_out_ref, o_ref,
                 u_scr, z_scr, *, tm, seq):
    t0 = pl.program_id(1) * tm
    gain = gain_ref[...]
    u_scr[0:HALO, :] = _adaln(hp_ref[0], gain, mod_ref, 1).astype(BF16)
    u_scr[HALO:HALO + tm, :] = _adaln(h_ref[0], gain, mod_ref, 1).astype(BF16)
    u_scr[HALO + tm:, :] = _adaln(hn_ref[0], gain, mod_ref, 1).astype(BF16)
    d = h_ref.shape[-1]
    cv = _mm(u_scr[...], w_in_ref[:, d:])
    t = t0 - HALO + lax.broadcasted_iota(jnp.int32, (tm + 2 * HALO, 1), 0)
    inside = jnp.logical_and(t >= 0, t < seq)
    z_scr[...] = jnp.where(inside, cv[:, :d] * cv[:, d:], 0.0)
    y = (cw_ref[0:1, :] * z_scr[HALO - 1:HALO - 1 + tm, :]
         + cw_ref[1:2, :] * z_scr[HALO:HALO + tm, :]
         + cw_ref[2:3, :] * z_scr[HALO + 1:HALO + 1 + tm, :])
    bg = _mm(u_scr[HALO:HALO + tm, :], w_in_ref[:, :d])
    out = _mm((bg * y).astype(BF16), w_out_ref[...])
    gate = mod_ref[0, 5:6, :]
    o_ref[0] = h_ref[0] + gate * out


def _conv_mixer(h, mod, gain, w_in, conv_w, w_out, tm):
    b, l, d = h.shape
    hb = tm // HALO
    const = lambda a: pl.BlockSpec((None,) + a.shape[1:], lambda i, j: (0,) * a.ndim,
                                   pipeline_mode=pl.Buffered(1))
    return pl.pallas_call(
        functools.partial(_conv_kernel, tm=tm, seq=l),
        grid=(b, l // tm),
        in_specs=[
            pl.BlockSpec((1, tm, d), lambda i, j: (i, j, 0)),
            pl.BlockSpec((1, HALO, d), lambda i, j: (i, jnp.maximum(j * hb - 1, 0), 0)),
            pl.BlockSpec((1, HALO, d), lambda i, j: (i, jnp.minimum((j + 1) * hb, l // HALO - 1), 0)),
            pl.BlockSpec((1, N_MOD, d), lambda i, j: (i, 0, 0)),
            pl.BlockSpec((1, d), lambda i, j: (0, 0)),
            const(w_in), const(conv_w), const(w_out),
        ],
        out_specs=pl.BlockSpec((1, tm, d), lambda i, j: (i, j, 0)),
        out_shape=jax.ShapeDtypeStruct((b, l, d), F32),
        scratch_shapes=[pltpu.VMEM((tm + 2 * HALO, d), BF16), pltpu.VMEM((tm + 2 * HALO, d), F32)],
        compiler_params=_params(("arbitrary", "arbitrary")),
        name="conv_mixer",
    )(h, h, h, mod, gain.reshape(1, d), w_in, conv_w, w_out)


def _head_pad(w, per_head, start, width):
    r = w.shape[0]
    w = w.reshape(r, MLA_HEADS, per_head)[:, :, start:start + width]
    return jnp.pad(w, ((0, 0), (0, 0), (0, HEAD_PAD - width))).reshape(r, QK_PAD_DIM)


def _rope_tables(length):
    pos = np.arange(length)
    row = (pos // GRID_W).astype(np.float32)
    col = (pos % GRID_W).astype(np.float32)
    half = ROPE_AXIS_DIM // 2
    freqs = np.power(np.float32(ROPE_THETA),
                     -np.arange(0, ROPE_AXIS_DIM, 2, dtype=np.float32) / np.float32(ROPE_AXIS_DIM))
    lane = np.arange(HEAD_PAD)
    o = lane - QK_NOPE_DIM
    rotary = np.logical_and(o >= 0, o < QK_ROPE_DIM)
    o = np.clip(o, 0, QK_ROPE_DIM - 1)
    ang = np.where((o // ROPE_AXIS_DIM == 0)[None, :], row[:, None], col[:, None]) * freqs[o % half][None, :]
    ang = ang.astype(np.float32)
    first = (o % ROPE_AXIS_DIM) < half
    cos = np.where(rotary[None, :], np.cos(ang), 1.0).astype(np.float32)
    sin = np.where(rotary[None, :], np.sin(ang), 0.0).astype(np.float32)
    return (jnp.asarray(cos), jnp.asarray(np.where(first[None, :], -sin, 0.0).astype(np.float32)),
            jnp.asarray(np.where(first[None, :], 0.0, sin).astype(np.float32)))


def kernel(x, c, ctx, c_ctx, norm_g, w_mod, b_mod, ffn_w_gate, ffn_w_up, ffn_w_down, ab_w_in, pool_w,
           pool_scale, q_norm_g, w_uq, kv_norm_g, w_ukv, ab_w_out, conv_w_in, conv_w, conv_w_out,
           final_norm_g):
    b, l, d = x.shape
    t_g = ctx.shape[1]
    tm_h, tm_g, tq = 512, 256, 512
    tm_mix = 1024

    cond = jnp.zeros((MOD_ROWS, d), F32).at[:b].set(c).at[b].set(c_ctx)
    m = _modulation(cond, w_mod, b_mod)
    mod_h = [m[i, :b].reshape(b, N_MOD, d) for i in range(2)]
    mod_g = jnp.broadcast_to(m[0, b].reshape(1, N_MOD, d), (b, N_MOD, d))

    ffn_w = (ffn_w_gate, ffn_w_up, ffn_w_down)

    h = x.reshape(b * l, d)
    g = ctx.reshape(b * t_g, d)

    h = _ffn_half(h, mod_h[0], norm_g[0, 0], *ffn_w, (0, 0), k=0, tm=tm_h)
    g = _ffn_half(g, mod_g, norm_g[0, 0], *ffn_w, (0, 0), k=0, tm=tm_g)

    kr_start = POOL_DIM + Q_LORA_RANK + KV_LORA_RANK
    w_kr = jnp.pad(ab_w_in[0, :, kr_start:], ((0, 0), (0, HEAD_PAD - QK_ROPE_DIM)))
    wq = _head_pad(w_uq[0], QK_HEAD_DIM, 0, QK_HEAD_DIM).astype(BF16)
    wk = _head_pad(w_ukv[0], QK_NOPE_DIM + V_HEAD_DIM, 0, QK_NOPE_DIM).T.astype(BF16)
    wv = w_ukv[0].reshape(KV_LORA_RANK, MLA_HEADS, QK_NOPE_DIM + V_HEAD_DIM)[:, :, QK_NOPE_DIM:]
    wv = wv.reshape(KV_LORA_RANK, ATTN_V_DIM).astype(BF16)
    lane = jnp.arange(HEAD_PAD)
    rotary = jnp.logical_and(lane >= QK_NOPE_DIM, lane < QK_HEAD_DIM)
    place = jnp.tile(jnp.where(rotary[:, None], jnp.eye(HEAD_PAD, dtype=F32), 0.0), (MLA_HEADS, 1)).astype(BF16)
    qg = q_norm_g[0].reshape(1, Q_LORA_RANK)
    kvg = kv_norm_g[0].reshape(1, KV_LORA_RANK)
    tabs = _rope_tables(l)

    pool, q, k_h, v_h = _mix_in(h, mod_h[0], norm_g[0, 1], ab_w_in, w_kr, kvg, wk, wv, place, tm_mix,
                                latent_args=(qg, wq, tabs))
    k_g, v_g = _mix_in(g, mod_g, norm_g[0, 1], ab_w_in, w_kr, kvg, wk, wv, place, tm_g)
    attn = _attention(q.reshape(b, l, -1), k_h, k_g, v_h.reshape(b, l, -1), v_g.reshape(b, t_g, -1), tq)
    h = _mix_out(h.reshape(b, l, d), mod_h[0], pool.reshape(b, l, -1), attn, pool_w,
                 pool_scale.reshape(-1, 1, POOL_DIM), ab_w_out, tm_mix)
    h = _ffn_half(h.reshape(b * l, d), mod_h[0], norm_g[0, 2], *ffn_w, (0, 1), k=2, tm=tm_h)

    h = _ffn_half(h, mod_h[1], norm_g[1, 0], *ffn_w, (1, 0), k=0, tm=tm_h)
    h = _conv_mixer(h.reshape(b, l, d), mod_h[1], norm_g[1, 1], conv_w_in, conv_w, conv_w_out, tm_mix)
    h = _ffn_half(h.reshape(b * l, d), mod_h[1], norm_g[1, 2], *ffn_w, (1, 1), k=2, tm=tm_h,
                  final_gain=final_norm_g)
    return h.reshape(b, l, d)
```

```python
import functools
import math

import jax
import jax.numpy as jnp
import numpy as np
from jax import lax
from jax.experimental import pallas as pl
from jax.experimental.pallas import tpu as pltpu

D_MODEL = 1024
SEQ = 2048
GRID_W = 64
CTX_LEN = 256
RMS_EPS = 1e-6
N_MOD = 9
D_FF = 2816
POOL_WINDOWS = (2, 4, 8, 16)
POOL_DIM = D_MODEL // 2
POOL_GROUP_DIM = POOL_DIM // len(POOL_WINDOWS)
MLA_HEADS = D_MODEL // 128
QK_NOPE_DIM = 64
QK_ROPE_DIM = 32
QK_HEAD_DIM = QK_NOPE_DIM + QK_ROPE_DIM
V_HEAD_DIM = 64
Q_LORA_RANK = 768
KV_LORA_RANK = 256
ROPE_AXIS_DIM = QK_ROPE_DIM // 2
ROPE_THETA = 10000.0
ATTN_SCALE = 1.0 / math.sqrt(QK_HEAD_DIM)
ATTN_V_DIM = MLA_HEADS * V_HEAD_DIM

LANES = 128
SUBLANES = 8
HEAD_PAD = LANES
QK_PAD_DIM = MLA_HEADS * HEAD_PAD
MOD_ROWS = 16
MOD_TILE_N = 3072
FF_CHUNK = 256
N_FF_CHUNKS = D_FF // FF_CHUNK
FF_HEAD_ROWS = 256
FF_TAIL_ROWS = 256
HALO = SUBLANES
PV_WIDTH = 256
PV_HEADS = PV_WIDTH // V_HEAD_DIM
ATTN_SUB_Q = 256
VMEM_LIMIT = 56 * 1024 * 1024

BF16 = jnp.bfloat16
F32 = jnp.float32


def _sigmoid(x):
    return 1.0 / (1.0 + jnp.exp(-x))


def _rms(x):
    return x * lax.rsqrt(jnp.mean(x * x, axis=-1, keepdims=True) + RMS_EPS)


def _adaln(x, gain, mod_ref, k):
    shift = mod_ref[0, 3 * k:3 * k + 1, :]
    scale = mod_ref[0, 3 * k + 1:3 * k + 2, :]
    return _rms(x) * gain * (1.0 + scale) + shift


_NT = (((1,), (1,)), ((), ()))


def _mm(a, w):
    return lax.dot_general(a, w, (((1,), (0,)), ((), ())), preferred_element_type=F32)


def _params(semantics):
    return pltpu.CompilerParams(dimension_semantics=semantics, vmem_limit_bytes=VMEM_LIMIT)


def _mod_kernel(cond_ref, w_ref, b_ref, o_ref):
    cond = cond_ref[...]
    a = (cond * _sigmoid(cond)).astype(BF16)
    o_ref[0] = _mm(a, w_ref[0]) + b_ref[0]


def _modulation(cond, w_mod, b_mod):
    depth, d, n = w_mod.shape
    return pl.pallas_call(
        _mod_kernel,
        grid=(depth, n // MOD_TILE_N),
        in_specs=[
            pl.BlockSpec((MOD_ROWS, d), lambda i, j: (0, 0)),
            pl.BlockSpec((1, d, MOD_TILE_N), lambda i, j: (i, 0, j)),
            pl.BlockSpec((1, 1, MOD_TILE_N), lambda i, j: (i, 0, j)),
        ],
        out_specs=pl.BlockSpec((1, MOD_ROWS, MOD_TILE_N), lambda i, j: (i, 0, j)),
        out_shape=jax.ShapeDtypeStruct((depth, MOD_ROWS, n), F32),
        compiler_params=_params(("arbitrary", "arbitrary")),
        name="modulation",
    )(cond, w_mod, b_mod.reshape(depth, 1, n))


def _ffn_kernel(*refs, k, final, sel):
    if final:
        (s_ref, mod_ref, gain_ref, wg_hbm, wu_hbm, wd_hbm, fg_ref, o_ref,
         u_scr, a_scr, wg_ref, wu_ref, wd_ref, sem) = refs
    else:
        (s_ref, mod_ref, gain_ref, wg_hbm, wu_hbm, wd_hbm, o_ref,
         u_scr, a_scr, wg_ref, wu_ref, wd_ref, sem) = refs
    tm = s_ref.shape[0]

    def chunk_copies(f):
        cols = slice(f * FF_CHUNK, (f + 1) * FF_CHUNK)
        return (pltpu.make_async_copy(wg_hbm.at[sel[0], sel[1], :, cols], wg_ref.at[:, cols], sem.at[2 * f]),
                pltpu.make_async_copy(wu_hbm.at[sel[0], sel[1], :, cols], wu_ref.at[:, cols], sem.at[2 * f + 1]))

    def down_copy():
        return pltpu.make_async_copy(wd_hbm.at[sel[0], sel[1]], wd_ref, sem.at[2 * N_FF_CHUNKS])

    def hidden(rows, f):
        cols = slice(f * FF_CHUNK, (f + 1) * FF_CHUNK)
        g = _mm(u_scr[rows, :], wg_ref[:, cols])
        up = _mm(u_scr[rows, :], wu_ref[:, cols])
        a_scr[rows, cols] = (g * _sigmoid(g) * up).astype(BF16)

    def body(first_step):
        if first_step:
            for f in range(N_FF_CHUNKS):
                for cp in chunk_copies(f):
                    cp.start()
            down_copy().start()
        for p in range(tm // FF_HEAD_ROWS):
            rows = slice(p * FF_HEAD_ROWS, (p + 1) * FF_HEAD_ROWS)
            u_scr[rows, :] = _adaln(s_ref[rows, :], gain_ref[...], mod_ref, k).astype(BF16)
            if first_step and p == 0:
                for cp in chunk_copies(0):
                    cp.wait()
            hidden(rows, 0)
        for f in range(1, N_FF_CHUNKS):
            if first_step:
                for cp in chunk_copies(f):
                    cp.wait()
            hidden(slice(None), f)
        if first_step:
            down_copy().wait()
        gate = mod_ref[0, 3 * k + 2:3 * k + 3, :]
        for p in range(tm // FF_TAIL_ROWS):
            rows = slice(p * FF_TAIL_ROWS, (p + 1) * FF_TAIL_ROWS)
            y = s_ref[rows, :] + 0.5 * gate * _mm(a_scr[rows, :], wd_ref[...])
            if final:
                y = _rms(y) * fg_ref[...]
            o_ref[rows, :] = y

    pl.when(pl.program_id(0) == 0)(functools.partial(body, True))
    pl.when(pl.program_id(0) != 0)(functools.partial(body, False))


def _ffn_half(s, mod, gain, wg, wu, wd, sel, k, tm, final_gain=None):
    n_tok, d = s.shape
    rows_per_mod = n_tok // mod.shape[0]
    assert rows_per_mod % tm == 0
    final = final_gain is not None
    hbm = pl.BlockSpec(memory_space=pl.ANY)
    in_specs = [
        pl.BlockSpec((tm, d), lambda i: (i, 0)),
        pl.BlockSpec((1, N_MOD, d), lambda i: (i * tm // rows_per_mod, 0, 0)),
        pl.BlockSpec((1, d), lambda i: (0, 0)),
        hbm, hbm, hbm,
    ]
    args = [s, mod, gain.reshape(1, d), wg, wu, wd]
    if final:
        in_specs.append(pl.BlockSpec((1, d), lambda i: (0, 0)))
        args.append(final_gain.reshape(1, d))
    return pl.pallas_call(
        functools.partial(_ffn_kernel, k=k, final=final, sel=sel),
        grid=(n_tok // tm,),
        in_specs=in_specs,
        out_specs=pl.BlockSpec((tm, d), lambda i: (i, 0)),
        out_shape=jax.ShapeDtypeStruct((n_tok, d), F32),
        scratch_shapes=[pltpu.VMEM((tm, d), BF16), pltpu.VMEM((tm, D_FF), BF16),
                        pltpu.VMEM(wg.shape[2:], F32), pltpu.VMEM(wu.shape[2:], F32),
                        pltpu.VMEM(wd.shape[2:], F32),
                        pltpu.SemaphoreType.DMA((2 * N_FF_CHUNKS + 1,))],
        compiler_params=_params(("arbitrary",)),
        name="ffn_half",
    )(*args)


def _rope(z, c_ref, s1_ref, s2_ref):
    fwd = pltpu.roll(z, HEAD_PAD - ROPE_AXIS_DIM // 2, axis=1)
    bwd = pltpu.roll(z, ROPE_AXIS_DIM // 2, axis=1)
    return z * c_ref[...] + fwd * s1_ref[...] + bwd * s2_ref[...]


def _mix_in_kernel(*refs, latent):
    if latent:
        (s_ref, mod_ref, gain_ref, w_in_ref, w_kr_ref, qg_ref, wq_ref, kvg_ref, wk_ref, wv_ref, place_ref,
         c_ref, s1_ref, s2_ref, pool_ref, q_ref, kt_ref, v_ref) = refs
    else:
        (s_ref, mod_ref, gain_ref, w_in_ref, w_kr_ref, kvg_ref, wk_ref, wv_ref, place_ref,
         kt_ref, v_ref) = refs
    u = _adaln(s_ref[...], gain_ref[...], mod_ref, 1).astype(BF16)
    cuts = (POOL_DIM, POOL_DIM + Q_LORA_RANK, POOL_DIM + Q_LORA_RANK + KV_LORA_RANK)
    z = _mm(u, w_in_ref[:, :cuts[2]])
    ckv = (_rms(z[:, cuts[1]:cuts[2]]) * kvg_ref[...]).astype(BF16)
    kr = pltpu.roll(_mm(u, w_kr_ref[...]), QK_NOPE_DIM, axis=1)
    if latent:
        pool_ref[...] = z[:, :cuts[0]]
        cq = (_rms(z[:, cuts[0]:cuts[1]]) * qg_ref[...]).astype(BF16)
        q = jnp.dot(cq, wq_ref[...], preferred_element_type=F32)
        for h in range(MLA_HEADS):
            sl = slice(h * HEAD_PAD, (h + 1) * HEAD_PAD)
            q_ref[:, sl] = _rope(q[:, sl], c_ref, s1_ref, s2_ref).astype(BF16)
        kr = _rope(kr, c_ref, s1_ref, s2_ref)
    kt = lax.dot_general(wk_ref[...], ckv, _NT, preferred_element_type=F32)
    kt = kt + lax.dot_general(place_ref[...], kr.astype(BF16), _NT, preferred_element_type=F32)
    kt_ref[0] = kt.astype(BF16)
    v_ref[...] = jnp.dot(ckv, wv_ref[...], preferred_element_type=F32).astype(BF16)


def _mix_in(s, mod, gain, w_in, w_kr, kvg, wk, wv, place, tm, latent_args=None):
    n_tok, d = s.shape
    tiles_per_batch = n_tok // mod.shape[0] // tm
    latent = latent_args is not None
    const = lambda a: pl.BlockSpec(a.shape, lambda i: (0,) * a.ndim)
    tok = lambda w: pl.BlockSpec((tm, w), lambda i: (i, 0))
    in_specs = [tok(d), pl.BlockSpec((1, N_MOD, d), lambda i: (i // tiles_per_batch, 0, 0)),
                pl.BlockSpec((1, d), lambda i: (0, 0)),
                pl.BlockSpec((None,) + w_in.shape[1:], lambda i: (0, 0, 0), pipeline_mode=pl.Buffered(1)),
                const(w_kr)]
    args = [s, mod, gain.reshape(1, d), w_in, w_kr]
    if latent:
        qg, wq, tabs = latent_args
        in_specs += [const(qg), const(wq)]
        args += [qg, wq]
    in_specs += [const(kvg), const(wk), const(wv), const(place)]
    args += [kvg, wk, wv, place]
    kt_spec = pl.BlockSpec((1, QK_PAD_DIM, tm), lambda i: (i // tiles_per_batch, 0, i % tiles_per_batch))
    out_specs = [kt_spec, tok(ATTN_V_DIM)]
    out_shape = [jax.ShapeDtypeStruct((mod.shape[0], QK_PAD_DIM, n_tok // mod.shape[0]), BF16),
                 jax.ShapeDtypeStruct((n_tok, ATTN_V_DIM), BF16)]
    if latent:
        rope_spec = pl.BlockSpec((tm, HEAD_PAD), lambda i: (i % tiles_per_batch, 0))
        in_specs += [rope_spec] * 3
        args += list(tabs)
        out_specs = [tok(POOL_DIM), tok(QK_PAD_DIM)] + out_specs
        out_shape = [jax.ShapeDtypeStruct((n_tok, POOL_DIM), F32),
                     jax.ShapeDtypeStruct((n_tok, QK_PAD_DIM), BF16)] + out_shape
    return pl.pallas_call(
        functools.partial(_mix_in_kernel, latent=latent),
        grid=(n_tok // tm,),
        in_specs=in_specs,
        out_specs=out_specs,
        out_shape=out_shape,
        compiler_params=_params(("arbitrary",)),
        name="mix_in_latent" if latent else "mix_in_context",
    )(*args)


def _attn_kernel(q_ref, kth_ref, ktg_ref, vh_ref, vg_ref, o_ref):
    c = ATTN_SCALE * math.log2(math.e)
    lane_head = lax.broadcasted_iota(jnp.int32, (1, PV_WIDTH), 1) // V_HEAD_DIM
    for sub in range(q_ref.shape[1] // ATTN_SUB_Q):
        rows = slice(sub * ATTN_SUB_Q, (sub + 1) * ATTN_SUB_Q)
        for grp in range(MLA_HEADS // PV_HEADS):
            vcols = slice(grp * PV_WIDTH, (grp + 1) * PV_WIDTH)
            acc = None
            for hh in range(PV_HEADS):
                h = grp * PV_HEADS + hh
                qk = slice(h * HEAD_PAD, (h + 1) * HEAD_PAD)
                q = q_ref[0, rows, qk]
                s_h = jnp.dot(q, kth_ref[0, qk, :], preferred_element_type=F32)
                s_g = jnp.dot(q, ktg_ref[0, qk, :], preferred_element_type=F32)
                m = jnp.maximum(jnp.max(s_h, axis=-1, keepdims=True),
                                jnp.max(s_g, axis=-1, keepdims=True))
                e_h = jnp.exp2((s_h - m) * c)
                e_g = jnp.exp2((s_g - m) * c)
                denom = jnp.sum(e_h, axis=-1, keepdims=True) + jnp.sum(e_g, axis=-1, keepdims=True)
                res = jnp.dot(e_h.astype(BF16), vh_ref[0, :, vcols], preferred_element_type=F32)
                res = res + jnp.dot(e_g.astype(BF16), vg_ref[0, :, vcols], preferred_element_type=F32)
                term = jnp.where(lane_head == hh, res * (1.0 / denom), 0.0)
                acc = term if acc is None else acc + term
            o_ref[0, rows, vcols] = acc.astype(BF16)


def _attention(q, kt_h, kt_g, v_h, v_g, tq):
    b, l, _ = q.shape
    t_g = kt_g.shape[2]
    return pl.pallas_call(
        _attn_kernel,
        grid=(b, l // tq),
        in_specs=[
            pl.BlockSpec((1, tq, QK_PAD_DIM), lambda i, j: (i, j, 0)),
            pl.BlockSpec((1, QK_PAD_DIM, l), lambda i, j: (i, 0, 0)),
            pl.BlockSpec((1, QK_PAD_DIM, t_g), lambda i, j: (i, 0, 0)),
            pl.BlockSpec((1, l, ATTN_V_DIM), lambda i, j: (i, 0, 0)),
            pl.BlockSpec((1, t_g, ATTN_V_DIM), lambda i, j: (i, 0, 0)),
        ],
        out_specs=pl.BlockSpec((1, tq, ATTN_V_DIM), lambda i, j: (i, j, 0)),
        out_shape=jax.ShapeDtypeStruct((b, l, ATTN_V_DIM), BF16),
        compiler_params=_params(("arbitrary", "arbitrary")),
        name="latent_attention",
    )(q, kt_h, kt_g, v_h, v_g)


def _mix_out_kernel(h_ref, mod_ref, pool_ref, attn_ref, pw_ref, ps_ref, wo_ref, o_ref, win_scr, y_scr,
                    *, tm, seq):
    t0 = pl.multiple_of(pl.program_id(1) * tm, tm)
    prev = pool_ref[0, pl.ds(pl.multiple_of(jnp.maximum(t0 - HALO, 0), HALO), HALO), :]
    nxt = pool_ref[0, pl.ds(pl.multiple_of(jnp.minimum(t0 + tm, seq - HALO), HALO), HALO), :]
    win_scr[0:HALO, :] = jnp.where(t0 > 0, prev, 0.0)
    win_scr[HALO:HALO + tm, :] = pool_ref[0, pl.ds(t0, tm), :]
    win_scr[HALO + tm:, :] = jnp.where(t0 + tm < seq, nxt, 0.0)
    t = (t0 + lax.broadcasted_iota(jnp.int32, (tm, 1), 0)).astype(F32)
    n = tm + 2 * HALO
    for g, w in enumerate(POOL_WINDOWS):
        lanes = slice(g * POOL_GROUP_DIM, (g + 1) * POOL_GROUP_DIM)
        fwd = win_scr[:, lanes]
        span = 1
        while span < w:
            fwd = fwd + pltpu.roll(fwd, n - span, axis=0)
            span *= 2
        total = pltpu.roll(fwd, w // 2, axis=0)[HALO:HALO + tm]
        cnt = jnp.minimum(t, float(w // 2)) + jnp.minimum(float(seq - 1) - t, float(w - w // 2 - 1)) + 1.0
        p = total / cnt - win_scr[HALO:HALO + tm, lanes]
        y = _mm(p.astype(BF16), pw_ref[g])
        y_scr[:, lanes] = (y * ps_ref[:, lanes]).astype(BF16)
    out = _mm(y_scr[...], wo_ref[:POOL_DIM, :]) + _mm(attn_ref[0], wo_ref[POOL_DIM:, :])
    gate = mod_ref[0, 5:6, :]
    o_ref[0] = h_ref[0] + gate * out


def _mix_out(h, mod, pool, attn, pool_w, pool_scale, w_out, tm):
    b, l, d = h.shape
    const = lambda a: pl.BlockSpec((None,) + a.shape[1:], lambda i, j: (0,) * a.ndim,
                                   pipeline_mode=pl.Buffered(1))
    return pl.pallas_call(
        functools.partial(_mix_out_kernel, tm=tm, seq=l),
        grid=(b, l // tm),
        in_specs=[
            pl.BlockSpec((1, tm, d), lambda i, j: (i, j, 0)),
            pl.BlockSpec((1, N_MOD, d), lambda i, j: (i, 0, 0)),
            pl.BlockSpec((1, l, POOL_DIM), lambda i, j: (i, 0, 0)),
            pl.BlockSpec((1, tm, ATTN_V_DIM), lambda i, j: (i, j, 0)),
            const(pool_w), const(pool_scale), const(w_out),
        ],
        out_specs=pl.BlockSpec((1, tm, d), lambda i, j: (i, j, 0)),
        out_shape=jax.ShapeDtypeStruct((b, l, d), F32),
        scratch_shapes=[pltpu.VMEM((tm + 2 * HALO, POOL_DIM), F32), pltpu.VMEM((tm, POOL_DIM), BF16)],
        compiler_params=_params(("arbitrary", "arbitrary")),
        name="mix_out",
    )(h, mod, pool, attn, pool_w, pool_scale, w_out)


def _conv_kernel(h_ref, hp_ref, hn_ref, mod_ref, gain_ref, w_in_ref, cw_ref, w_out_ref, o_ref,
                 u_scr, z_scr, *, tm, seq):
    t0 = pl.program_id(1) * tm
    gain = gain_ref[...]
    u_scr[0:HALO, :] = _adaln(hp_ref[0], gain, mod_ref, 1).astype(BF16)
    u_scr[HALO:HALO + tm, :] = _adaln(h_ref[0], gain, mod_ref, 1).astype(BF16)
    u_scr[HALO + tm:, :] = _adaln(hn_ref[0], gain, mod_ref, 1).astype(BF16)
    d = h_ref.shape[-1]
    cv = _mm(u_scr[...], w_in_ref[:, d:])
    t = t0 - HALO + lax.broadcasted_iota(jnp.int32, (tm + 2 * HALO, 1), 0)
    inside = jnp.logical_and(t >= 0, t < seq)
    z_scr[...] = jnp.where(inside, cv[:, :d] * cv[:, d:], 0.0)
    y = (cw_ref[0:1, :] * z_scr[HALO - 1:HALO - 1 + tm, :]
         + cw_ref[1:2, :] * z_scr[HALO:HALO + tm, :]
         + cw_ref[2:3, :] * z_scr[HALO + 1:HALO + 1 + tm, :])
    bg = _mm(u_scr[HALO:HALO + tm, :], w_in_ref[:, :d])
    out = _mm((bg * y).astype(BF16), w_out_ref[...])
    gate = mod_ref[0, 5:6, :]
    o_ref[0] = h_ref[0] + gate * out


def _conv_mixer(h, mod, gain, w_in, conv_w, w_out, tm):
    b, l, d = h.shape
    hb = tm // HALO
    const = lambda a: pl.BlockSpec((None,) + a.shape[1:], lambda i, j: (0,) * a.ndim,
                                   pipeline_mode=pl.Buffered(1))
    return pl.pallas_call(
        functools.partial(_conv_kernel, tm=tm, seq=l),
        grid=(b, l // tm),
        in_specs=[
            pl.BlockSpec((1, tm, d), lambda i, j: (i, j, 0)),
            pl.BlockSpec((1, HALO, d), lambda i, j: (i, jnp.maximum(j * hb - 1, 0), 0)),
            pl.BlockSpec((1, HALO, d), lambda i, j: (i, jnp.minimum((j + 1) * hb, l // HALO - 1), 0)),
            pl.BlockSpec((1, N_MOD, d), lambda i, j: (i, 0, 0)),
            pl.BlockSpec((1, d), lambda i, j: (0, 0)),
            const(w_in), const(conv_w), const(w_out),
        ],
        out_specs=pl.BlockSpec((1, tm, d), lambda i, j: (i, j, 0)),
        out_shape=jax.ShapeDtypeStruct((b, l, d), F32),
        scratch_shapes=[pltpu.VMEM((tm + 2 * HALO, d), BF16), pltpu.VMEM((tm + 2 * HALO, d), F32)],
        compiler_params=_params(("arbitrary", "arbitrary")),
        name="conv_mixer",
    )(h, h, h, mod, gain.reshape(1, d), w_in, conv_w, w_out)


def _head_pad(w, per_head, start, width):
    r = w.shape[0]
    w = w.reshape(r, MLA_HEADS, per_head)[:, :, start:start + width]
    return jnp.pad(w, ((0, 0), (0, 0), (0, HEAD_PAD - width))).reshape(r, QK_PAD_DIM)


def _rope_tables(length):
    pos = np.arange(length)
    row = (pos // GRID_W).astype(np.float32)
    col = (pos % GRID_W).astype(np.float32)
    half = ROPE_AXIS_DIM // 2
    freqs = np.power(np.float32(ROPE_THETA),
                     -np.arange(0, ROPE_AXIS_DIM, 2, dtype=np.float32) / np.float32(ROPE_AXIS_DIM))
    lane = np.arange(HEAD_PAD)
    o = lane - QK_NOPE_DIM
    rotary = np.logical_and(o >= 0, o < QK_ROPE_DIM)
    o = np.clip(o, 0, QK_ROPE_DIM - 1)
    ang = np.where((o // ROPE_AXIS_DIM == 0)[None, :], row[:, None], col[:, None]) * freqs[o % half][None, :]
    ang = ang.astype(np.float32)
    first = (o % ROPE_AXIS_DIM) < half
    cos = np.where(rotary[None, :], np.cos(ang), 1.0).astype(np.float32)
    sin = np.where(rotary[None, :], np.sin(ang), 0.0).astype(np.float32)
    return (jnp.asarray(cos), jnp.asarray(np.where(first[None, :], -sin, 0.0).astype(np.float32)),
            jnp.asarray(np.where(first[None, :], 0.0, sin).astype(np.float32)))


def kernel(x, c, ctx, c_ctx, norm_g, w_mod, b_mod, ffn_w_gate, ffn_w_up, ffn_w_down, ab_w_in, pool_w,
           pool_scale, q_norm_g, w_uq, kv_norm_g, w_ukv, ab_w_out, conv_w_in, conv_w, conv_w_out,
           final_norm_g):
    b, l, d = x.shape
    t_g = ctx.shape[1]
    tm_h, tm_g, tq = 512, 256, 512
    tm_mix = 1024

    cond = jnp.zeros((MOD_ROWS, d), F32).at[:b].set(c).at[b].set(c_ctx)
    m = _modulation(cond, w_mod, b_mod)
    mod_h = [m[i, :b].reshape(b, N_MOD, d) for i in range(2)]
    mod_g = jnp.broadcast_to(m[0, b].reshape(1, N_MOD, d), (b, N_MOD, d))

    ffn_w = (ffn_w_gate, ffn_w_up, ffn_w_down)

    h = x.reshape(b * l, d)
    g = ctx.reshape(b * t_g, d)

    h = _ffn_half(h, mod_h[0], norm_g[0, 0], *ffn_w, (0, 0), k=0, tm=tm_h)
    g = _ffn_half(g, m[0, b].reshape(1, N_MOD, d), norm_g[0, 0], *ffn_w, (0, 0), k=0, tm=tm_h)

    kr_start = POOL_DIM + Q_LORA_RANK + KV_LORA_RANK
    w_kr = jnp.pad(ab_w_in[0, :, kr_start:], ((0, 0), (0, HEAD_PAD - QK_ROPE_DIM)))
    wq = _head_pad(w_uq[0], QK_HEAD_DIM, 0, QK_HEAD_DIM).astype(BF16)
    wk = _head_pad(w_ukv[0], QK_NOPE_DIM + V_HEAD_DIM, 0, QK_NOPE_DIM).T.astype(BF16)
    wv = w_ukv[0].reshape(KV_LORA_RANK, MLA_HEADS, QK_NOPE_DIM + V_HEAD_DIM)[:, :, QK_NOPE_DIM:]
    wv = wv.reshape(KV_LORA_RANK, ATTN_V_DIM).astype(BF16)
    lane = jnp.arange(HEAD_PAD)
    rotary = jnp.logical_and(lane >= QK_NOPE_DIM, lane < QK_HEAD_DIM)
    place = jnp.tile(jnp.where(rotary[:, None], jnp.eye(HEAD_PAD, dtype=F32), 0.0), (MLA_HEADS, 1)).astype(BF16)
    qg = q_norm_g[0].reshape(1, Q_LORA_RANK)
    kvg = kv_norm_g[0].reshape(1, KV_LORA_RANK)
    tabs = _rope_tables(l)

    pool, q, k_h, v_h = _mix_in(h, mod_h[0], norm_g[0, 1], ab_w_in, w_kr, kvg, wk, wv, place, tm_mix,
                                latent_args=(qg, wq, tabs))
    k_g, v_g = _mix_in(g, mod_g, norm_g[0, 1], ab_w_in, w_kr, kvg, wk, wv, place, tm_g)
    attn = _attention(q.reshape(b, l, -1), k_h, k_g, v_h.reshape(b, l, -1), v_g.reshape(b, t_g, -1), tq)
    h = _mix_out(h.reshape(b, l, d), mod_h[0], pool.reshape(b, l, -1), attn, pool_w,
                 pool_scale.reshape(-1, 1, POOL_DIM), ab_w_out, tm_mix)
    h = _ffn_half(h.reshape(b * l, d), mod_h[0], norm_g[0, 2], *ffn_w, (0, 1), k=2, tm=tm_h)

    h = _ffn_half(h, mod_h[1], norm_g[1, 0], *ffn_w, (1, 0), k=0, tm=tm_h)
    h = _conv_mixer(h.reshape(b, l, d), mod_h[1], norm_g[1, 1], conv_w_in, conv_w, conv_w_out, tm_mix)
    h = _ffn_half(h.reshape(b * l, d), mod_h[1], norm_g[1, 2], *ffn_w, (1, 1), k=2, tm=tm_h,
                  final_gain=final_norm_g)
    return h.reshape(b, l, d)
```

```python
import functools
import math

import jax
import jax.numpy as jnp
import numpy as np
from jax import lax
from jax.experimental import pallas as pl
from jax.experimental.pallas import tpu as pltpu

D_MODEL = 1024
SEQ = 2048
GRID_W = 64
CTX_LEN = 256
RMS_EPS = 1e-6
N_MOD = 9
D_FF = 2816
POOL_WINDOWS = (2, 4, 8, 16)
POOL_DIM = D_MODEL // 2
POOL_GROUP_DIM = POOL_DIM // len(POOL_WINDOWS)
MLA_HEADS = D_MODEL // 128
QK_NOPE_DIM = 64
QK_ROPE_DIM = 32
QK_HEAD_DIM = QK_NOPE_DIM + QK_ROPE_DIM
V_HEAD_DIM = 64
Q_LORA_RANK = 768
KV_LORA_RANK = 256
ROPE_AXIS_DIM = QK_ROPE_DIM // 2
ROPE_THETA = 10000.0
ATTN_SCALE = 1.0 / math.sqrt(QK_HEAD_DIM)
ATTN_V_DIM = MLA_HEADS * V_HEAD_DIM

LANES = 128
SUBLANES = 8
HEAD_PAD = LANES
QK_PAD_DIM = MLA_HEADS * HEAD_PAD
MOD_ROWS = 16
MOD_TILE_N = 3072
FF_CHUNK = 256
N_FF_CHUNKS = D_FF // FF_CHUNK
FF_HEAD_ROWS = 256
FF_TAIL_ROWS = 256
HALO = SUBLANES
PV_WIDTH = 256
PV_HEADS = PV_WIDTH // V_HEAD_DIM
ATTN_SUB_Q = 256
V7X_VMEM_BYTES = 64 * 1024 * 1024
VMEM_LIMIT = V7X_VMEM_BYTES * 7 // 8
FFN_TILE_ROWS = 512
ATTN_TILE_ROWS = 512
MIX_TILE_ROWS = 1024
CTX_MIX_TILE_ROWS = CTX_LEN

BF16 = jnp.bfloat16
F32 = jnp.float32


def _sigmoid(x):
    return 1.0 / (1.0 + jnp.exp(-x))


def _rms(x):
    return x * lax.rsqrt(jnp.mean(x * x, axis=-1, keepdims=True) + RMS_EPS)


def _adaln(x, gain, mod_ref, k):
    shift = mod_ref[0, 3 * k:3 * k + 1, :]
    scale = mod_ref[0, 3 * k + 1:3 * k + 2, :]
    return _rms(x) * gain * (1.0 + scale) + shift


_NT = (((1,), (1,)), ((), ()))


def _mm(a, w):
    return lax.dot_general(a, w, (((1,), (0,)), ((), ())), preferred_element_type=F32)


def _params(semantics):
    return pltpu.CompilerParams(dimension_semantics=semantics, vmem_limit_bytes=VMEM_LIMIT)


def _mod_kernel(cond_ref, w_ref, b_ref, o_ref):
    cond = cond_ref[...]
    a = (cond * _sigmoid(cond)).astype(BF16)
    o_ref[0] = _mm(a, w_ref[0]) + b_ref[0]


def _modulation(cond, w_mod, b_mod):
    depth, d, n = w_mod.shape
    return pl.pallas_call(
        _mod_kernel,
        grid=(depth, n // MOD_TILE_N),
        in_specs=[
            pl.BlockSpec((MOD_ROWS, d), lambda i, j: (0, 0)),
            pl.BlockSpec((1, d, MOD_TILE_N), lambda i, j: (i, 0, j)),
            pl.BlockSpec((1, 1, MOD_TILE_N), lambda i, j: (i, 0, j)),
        ],
        out_specs=pl.BlockSpec((1, MOD_ROWS, MOD_TILE_N), lambda i, j: (i, 0, j)),
        out_shape=jax.ShapeDtypeStruct((depth, MOD_ROWS, n), F32),
        compiler_params=_params(("arbitrary", "arbitrary")),
        name="modulation",
    )(cond, w_mod, b_mod.reshape(depth, 1, n))


def _ffn_kernel(*refs, k, final, sel):
    if final:
        (s_ref, mod_ref, gain_ref, wg_hbm, wu_hbm, wd_hbm, fg_ref, o_ref,
         u_scr, a_scr, wg_ref, wu_ref, wd_ref, sem) = refs
    else:
        (s_ref, mod_ref, gain_ref, wg_hbm, wu_hbm, wd_hbm, o_ref,
         u_scr, a_scr, wg_ref, wu_ref, wd_ref, sem) = refs
    tm = s_ref.shape[0]

    def chunk_copies(f):
        cols = slice(f * FF_CHUNK, (f + 1) * FF_CHUNK)
        return (pltpu.make_async_copy(wg_hbm.at[sel[0], sel[1], :, cols], wg_ref.at[:, cols], sem.at[2 * f]),
                pltpu.make_async_copy(wu_hbm.at[sel[0], sel[1], :, cols], wu_ref.at[:, cols], sem.at[2 * f + 1]))

    def down_copy():
        return pltpu.make_async_copy(wd_hbm.at[sel[0], sel[1]], wd_ref, sem.at[2 * N_FF_CHUNKS])

    def hidden(rows, f):
        cols = slice(f * FF_CHUNK, (f + 1) * FF_CHUNK)
        g = _mm(u_scr[rows, :], wg_ref[:, cols])
        up = _mm(u_scr[rows, :], wu_ref[:, cols])
        a_scr[rows, cols] = (g * _sigmoid(g) * up).astype(BF16)

    def body(first_step):
        if first_step:
            for f in range(N_FF_CHUNKS):
                for cp in chunk_copies(f):
                    cp.start()
            down_copy().start()
        for p in range(tm // FF_HEAD_ROWS):
            rows = slice(p * FF_HEAD_ROWS, (p + 1) * FF_HEAD_ROWS)
            u_scr[rows, :] = _adaln(s_ref[rows, :], gain_ref[...], mod_ref, k).astype(BF16)
            if first_step and p == 0:
                for cp in chunk_copies(0):
                    cp.wait()
            hidden(rows, 0)
        for f in range(1, N_FF_CHUNKS):
            if first_step:
                for cp in chunk_copies(f):
                    cp.wait()
            hidden(slice(None), f)
        if first_step:
            down_copy().wait()
        gate = mod_ref[0, 3 * k + 2:3 * k + 3, :]
        for p in range(tm // FF_TAIL_ROWS):
            rows = slice(p * FF_TAIL_ROWS, (p + 1) * FF_TAIL_ROWS)
            y = s_ref[rows, :] + 0.5 * gate * _mm(a_scr[rows, :], wd_ref[...])
            if final:
                y = _rms(y) * fg_ref[...]
            o_ref[rows, :] = y

    pl.when(pl.program_id(0) == 0)(functools.partial(body, True))
    pl.when(pl.program_id(0) != 0)(functools.partial(body, False))


def _ffn_half(s, mod, gain, wg, wu, wd, sel, k, tm, final_gain=None):
    n_tok, d = s.shape
    rows_per_mod = n_tok // mod.shape[0]
    assert rows_per_mod % tm == 0
    final = final_gain is not None
    hbm = pl.BlockSpec(memory_space=pl.ANY)
    in_specs = [
        pl.BlockSpec((tm, d), lambda i: (i, 0)),
        pl.BlockSpec((1, N_MOD, d), lambda i: (i * tm // rows_per_mod, 0, 0)),
        pl.BlockSpec((1, d), lambda i: (0, 0)),
        hbm, hbm, hbm,
    ]
    args = [s, mod, gain.reshape(1, d), wg, wu, wd]
    if final:
        in_specs.append(pl.BlockSpec((1, d), lambda i: (0, 0)))
        args.append(final_gain.reshape(1, d))
    return pl.pallas_call(
        functools.partial(_ffn_kernel, k=k, final=final, sel=sel),
        grid=(n_tok // tm,),
        in_specs=in_specs,
        out_specs=pl.BlockSpec((tm, d), lambda i: (i, 0)),
        out_shape=jax.ShapeDtypeStruct((n_tok, d), F32),
        scratch_shapes=[pltpu.VMEM((tm, d), BF16), pltpu.VMEM((tm, D_FF), BF16),
                        pltpu.VMEM(wg.shape[2:], F32), pltpu.VMEM(wu.shape[2:], F32),
                        pltpu.VMEM(wd.shape[2:], F32),
                        pltpu.SemaphoreType.DMA((2 * N_FF_CHUNKS + 1,))],
        compiler_params=_params(("arbitrary",)),
        name="ffn_half",
    )(*args)


def _rope(z, c_ref, s1_ref, s2_ref):
    fwd = pltpu.roll(z, HEAD_PAD - ROPE_AXIS_DIM // 2, axis=1)
    bwd = pltpu.roll(z, ROPE_AXIS_DIM // 2, axis=1)
    return z * c_ref[...] + fwd * s1_ref[...] + bwd * s2_ref[...]


def _mix_in_kernel(*refs, latent):
    if latent:
        (s_ref, mod_ref, gain_ref, w_in_ref, w_kr_ref, qg_ref, wq_ref, kvg_ref, wk_ref, wv_ref, place_ref,
         c_ref, s1_ref, s2_ref, pool_ref, q_ref, kt_ref, v_ref) = refs
    else:
        (s_ref, mod_ref, gain_ref, w_in_ref, w_kr_ref, kvg_ref, wk_ref, wv_ref, place_ref,
         kt_ref, v_ref) = refs
    u = _adaln(s_ref[...], gain_ref[...], mod_ref, 1).astype(BF16)
    cuts = (POOL_DIM, POOL_DIM + Q_LORA_RANK, POOL_DIM + Q_LORA_RANK + KV_LORA_RANK)
    z = _mm(u, w_in_ref[:, :cuts[2]])
    ckv = (_rms(z[:, cuts[1]:cuts[2]]) * kvg_ref[...]).astype(BF16)
    kr = pltpu.roll(_mm(u, w_kr_ref[...]), QK_NOPE_DIM, axis=1)
    if latent:
        pool_ref[...] = z[:, :cuts[0]]
        cq = (_rms(z[:, cuts[0]:cuts[1]]) * qg_ref[...]).astype(BF16)
        q = jnp.dot(cq, wq_ref[...], preferred_element_type=F32)
        for h in range(MLA_HEADS):
            sl = slice(h * HEAD_PAD, (h + 1) * HEAD_PAD)
            q_ref[:, sl] = _rope(q[:, sl], c_ref, s1_ref, s2_ref).astype(BF16)
        kr = _rope(kr, c_ref, s1_ref, s2_ref)
    kt = lax.dot_general(wk_ref[...], ckv, _NT, preferred_element_type=F32)
    kt = kt + lax.dot_general(place_ref[...], kr.astype(BF16), _NT, preferred_element_type=F32)
    kt_ref[0] = kt.astype(BF16)
    v_ref[...] = jnp.dot(ckv, wv_ref[...], preferred_element_type=F32).astype(BF16)


def _mix_in(s, mod, gain, w_in, w_kr, kvg, wk, wv, place, tm, latent_args=None):
    n_tok, d = s.shape
    tiles_per_batch = n_tok // mod.shape[0] // tm
    latent = latent_args is not None
    const = lambda a: pl.BlockSpec(a.shape, lambda i: (0,) * a.ndim)
    tok = lambda w: pl.BlockSpec((tm, w), lambda i: (i, 0))
    in_specs = [tok(d), pl.BlockSpec((1, N_MOD, d), lambda i: (i // tiles_per_batch, 0, 0)),
                pl.BlockSpec((1, d), lambda i: (0, 0)),
                pl.BlockSpec((None,) + w_in.shape[1:], lambda i: (0, 0, 0), pipeline_mode=pl.Buffered(1)),
                const(w_kr)]
    args = [s, mod, gain.reshape(1, d), w_in, w_kr]
    if latent:
        qg, wq, tabs = latent_args
        in_specs += [const(qg), const(wq)]
        args += [qg, wq]
    in_specs += [const(kvg), const(wk), const(wv), const(place)]
    args += [kvg, wk, wv, place]
    kt_spec = pl.BlockSpec((1, QK_PAD_DIM, tm), lambda i: (i // tiles_per_batch, 0, i % tiles_per_batch))
    out_specs = [kt_spec, tok(ATTN_V_DIM)]
    out_shape = [jax.ShapeDtypeStruct((mod.shape[0], QK_PAD_DIM, n_tok // mod.shape[0]), BF16),
                 jax.ShapeDtypeStruct((n_tok, ATTN_V_DIM), BF16)]
    if latent:
        rope_spec = pl.BlockSpec((tm, HEAD_PAD), lambda i: (i % tiles_per_batch, 0))
        in_specs += [rope_spec] * 3
        args += list(tabs)
        out_specs = [tok(POOL_DIM), tok(QK_PAD_DIM)] + out_specs
        out_shape = [jax.ShapeDtypeStruct((n_tok, POOL_DIM), F32),
                     jax.ShapeDtypeStruct((n_tok, QK_PAD_DIM), BF16)] + out_shape
    return pl.pallas_call(
        functools.partial(_mix_in_kernel, latent=latent),
        grid=(n_tok // tm,),
        in_specs=in_specs,
        out_specs=out_specs,
        out_shape=out_shape,
        compiler_params=_params(("arbitrary",)),
        name="mix_in_latent" if latent else "mix_in_context",
    )(*args)


def _attn_kernel(q_ref, kth_ref, ktg_ref, vh_ref, vg_ref, o_ref):
    c = ATTN_SCALE * math.log2(math.e)
    lane_head = lax.broadcasted_iota(jnp.int32, (1, PV_WIDTH), 1) // V_HEAD_DIM
    for sub in range(q_ref.shape[1] // ATTN_SUB_Q):
        rows = slice(sub * ATTN_SUB_Q, (sub + 1) * ATTN_SUB_Q)
        for grp in range(MLA_HEADS // PV_HEADS):
            vcols = slice(grp * PV_WIDTH, (grp + 1) * PV_WIDTH)
            acc = None
            for hh in range(PV_HEADS):
                h = grp * PV_HEADS + hh
                qk = slice(h * HEAD_PAD, (h + 1) * HEAD_PAD)
                q = q_ref[0, rows, qk]
                s_h = jnp.dot(q, kth_ref[0, qk, :], preferred_element_type=F32)
                s_g = jnp.dot(q, ktg_ref[0, qk, :], preferred_element_type=F32)
                m = jnp.maximum(jnp.max(s_h, axis=-1, keepdims=True),
                                jnp.max(s_g, axis=-1, keepdims=True))
                e_h = jnp.exp2((s_h - m) * c)
                e_g = jnp.exp2((s_g - m) * c)
                denom = jnp.sum(e_h, axis=-1, keepdims=True) + jnp.sum(e_g, axis=-1, keepdims=True)
                res = jnp.dot(e_h.astype(BF16), vh_ref[0, :, vcols], preferred_element_type=F32)
                res = res + jnp.dot(e_g.astype(BF16), vg_ref[0, :, vcols], preferred_element_type=F32)
                term = jnp.where(lane_head == hh, res * (1.0 / denom), 0.0)
                acc = term if acc is None else acc + term
            o_ref[0, rows, vcols] = acc.astype(BF16)


def _attention(q, kt_h, kt_g, v_h, v_g, tq):
    b, l, _ = q.shape
    t_g = kt_g.shape[2]
    return pl.pallas_call(
        _attn_kernel,
        grid=(b, l // tq),
        in_specs=[
            pl.BlockSpec((1, tq, QK_PAD_DIM), lambda i, j: (i, j, 0)),
            pl.BlockSpec((1, QK_PAD_DIM, l), lambda i, j: (i, 0, 0)),
            pl.BlockSpec((1, QK_PAD_DIM, t_g), lambda i, j: (i, 0, 0)),
            pl.BlockSpec((1, l, ATTN_V_DIM), lambda i, j: (i, 0, 0)),
            pl.BlockSpec((1, t_g, ATTN_V_DIM), lambda i, j: (i, 0, 0)),
        ],
        out_specs=pl.BlockSpec((1, tq, ATTN_V_DIM), lambda i, j: (i, j, 0)),
        out_shape=jax.ShapeDtypeStruct((b, l, ATTN_V_DIM), BF16),
        compiler_params=_params(("arbitrary", "arbitrary")),
        name="latent_attention",
    )(q, kt_h, kt_g, v_h, v_g)


def _mix_out_kernel(h_ref, mod_ref, pool_ref, attn_ref, pw_ref, ps_ref, wo_ref, o_ref, win_scr, y_scr,
                    *, tm, seq):
    t0 = pl.multiple_of(pl.program_id(1) * tm, tm)
    prev = pool_ref[0, pl.ds(pl.multiple_of(jnp.maximum(t0 - HALO, 0), HALO), HALO), :]
    nxt = pool_ref[0, pl.ds(pl.multiple_of(jnp.minimum(t0 + tm, seq - HALO), HALO), HALO), :]
    win_scr[0:HALO, :] = jnp.where(t0 > 0, prev, 0.0)
    win_scr[HALO:HALO + tm, :] = pool_ref[0, pl.ds(t0, tm), :]
    win_scr[HALO + tm:, :] = jnp.where(t0 + tm < seq, nxt, 0.0)
    t = (t0 + lax.broadcasted_iota(jnp.int32, (tm, 1), 0)).astype(F32)
    n = tm + 2 * HALO
    for g, w in enumerate(POOL_WINDOWS):
        lanes = slice(g * POOL_GROUP_DIM, (g + 1) * POOL_GROUP_DIM)
        fwd = win_scr[:, lanes]
        span = 1
        while span < w:
            fwd = fwd + pltpu.roll(fwd, n - span, axis=0)
            span *= 2
        total = pltpu.roll(fwd, w // 2, axis=0)[HALO:HALO + tm]
        cnt = jnp.minimum(t, float(w // 2)) + jnp.minimum(float(seq - 1) - t, float(w - w // 2 - 1)) + 1.0
        p = total / cnt - win_scr[HALO:HALO + tm, lanes]
        y = _mm(p.astype(BF16), pw_ref[g])
        y_scr[:, lanes] = (y * ps_ref[:, lanes]).astype(BF16)
    out = _mm(y_scr[...], wo_ref[:POOL_DIM, :]) + _mm(attn_ref[0], wo_ref[POOL_DIM:, :])
    gate = mod_ref[0, 3 * 1 + 2:3 * 1 + 3, :]
    o_ref[0] = h_ref[0] + gate * out


def _mix_out(h, mod, pool, attn, pool_w, pool_scale, w_out, tm):
    b, l, d = h.shape
    const = lambda a: pl.BlockSpec((None,) + a.shape[1:], lambda i, j: (0,) * a.ndim,
                                   pipeline_mode=pl.Buffered(1))
    return pl.pallas_call(
        functools.partial(_mix_out_kernel, tm=tm, seq=l),
        grid=(b, l // tm),
        in_specs=[
            pl.BlockSpec((1, tm, d), lambda i, j: (i, j, 0)),
            pl.BlockSpec((1, N_MOD, d), lambda i, j: (i, 0, 0)),
            pl.BlockSpec((1, l, POOL_DIM), lambda i, j: (i, 0, 0)),
            pl.BlockSpec((1, tm, ATTN_V_DIM), lambda i, j: (i, j, 0)),
            const(pool_w), const(pool_scale), const(w_out),
        ],
        out_specs=pl.BlockSpec((1, tm, d), lambda i, j: (i, j, 0)),
        out_shape=jax.ShapeDtypeStruct((b, l, d), F32),
        scratch_shapes=[pltpu.VMEM((tm + 2 * HALO, POOL_DIM), F32), pltpu.VMEM((tm, POOL_DIM), BF16)],
        compiler_params=_params(("arbitrary", "arbitrary")),
        name="mix_out",
    )(h, mod, pool, attn, pool_w, pool_scale, w_out)


def _conv_kernel(h_ref, hp_ref, hn_ref, mod_ref, gain_ref, w_in_ref, cw_ref, w_out_ref, o_ref,
                 u_scr, z_scr, *, tm, seq):
    t0 = pl.program_id(1) * tm
    gain = gain_ref[...]
    u_scr[0:HALO, :] = _adaln(hp_ref[0], gain, mod_ref, 1).astype(BF16)
    u_scr[HALO:HALO + tm, :] = _adaln(h_ref[0], gain, mod_ref, 1).astype(BF16)
    u_scr[HALO + tm:, :] = _adaln(hn_ref[0], gain, mod_ref, 1).astype(BF16)
    d = h_ref.shape[-1]
    cv = _mm(u_scr[...], w_in_ref[:, d:])
    t = t0 - HALO + lax.broadcasted_iota(jnp.int32, (tm + 2 * HALO, 1), 0)
    inside = jnp.logical_and(t >= 0, t < seq)
    z_scr[...] = jnp.where(inside, cv[:, :d] * cv[:, d:], 0.0)
    y = (cw_ref[0:1, :] * z_scr[HALO - 1:HALO - 1 + tm, :]
         + cw_ref[1:2, :] * z_scr[HALO:HALO + tm, :]
         + cw_ref[2:3, :] * z_scr[HALO + 1:HALO + 1 + tm, :])
    bg = _mm(u_scr[HALO:HALO + tm, :], w_in_ref[:, :d])
    out = _mm((bg * y).astype(BF16), w_out_ref[...])
    gate = mod_ref[0, 3 * 1 + 2:3 * 1 + 3, :]
    o_ref[0] = h_ref[0] + gate * out


def _conv_mixer(h, mod, gain, w_in, conv_w, w_out, tm):
    b, l, d = h.shape
    hb = tm // HALO
    const = lambda a: pl.BlockSpec((None,) + a.shape[1:], lambda i, j: (0,) * a.ndim,
                                   pipeline_mode=pl.Buffered(1))
    return pl.pallas_call(
        functools.partial(_conv_kernel, tm=tm, seq=l),
        grid=(b, l // tm),
        in_specs=[
            pl.BlockSpec((1, tm, d), lambda i, j: (i, j, 0)),
            pl.BlockSpec((1, HALO, d), lambda i, j: (i, jnp.maximum(j * hb - 1, 0), 0)),
            pl.BlockSpec((1, HALO, d), lambda i, j: (i, jnp.minimum((j + 1) * hb, l // HALO - 1), 0)),
            pl.BlockSpec((1, N_MOD, d), lambda i, j: (i, 0, 0)),
            pl.BlockSpec((1, d), lambda i, j: (0, 0)),
            const(w_in), const(conv_w), const(w_out),
        ],
        out_specs=pl.BlockSpec((1, tm, d), lambda i, j: (i, j, 0)),
        out_shape=jax.ShapeDtypeStruct((b, l, d), F32),
        scratch_shapes=[pltpu.VMEM((tm + 2 * HALO, d), BF16), pltpu.VMEM((tm + 2 * HALO, d), F32)],
        compiler_params=_params(("arbitrary", "arbitrary")),
        name="conv_mixer",
    )(h, h, h, mod, gain.reshape(1, d), w_in, conv_w, w_out)


def _head_pad(w, per_head, start, width):
    r = w.shape[0]
    w = w.reshape(r, MLA_HEADS, per_head)[:, :, start:start + width]
    return jnp.pad(w, ((0, 0), (0, 0), (0, HEAD_PAD - width))).reshape(r, QK_PAD_DIM)


def _rope_tables(length):
    pos = np.arange(length)
    row = (pos // GRID_W).astype(np.float32)
    col = (pos % GRID_W).astype(np.float32)
    half = ROPE_AXIS_DIM // 2
    freqs = np.power(np.float32(ROPE_THETA),
                     -np.arange(0, ROPE_AXIS_DIM, 2, dtype=np.float32) / np.float32(ROPE_AXIS_DIM))
    lane = np.arange(HEAD_PAD)
    o = lane - QK_NOPE_DIM
    rotary = np.logical_and(o >= 0, o < QK_ROPE_DIM)
    o = np.clip(o, 0, QK_ROPE_DIM - 1)
    ang = np.where((o // ROPE_AXIS_DIM == 0)[None, :], row[:, None], col[:, None]) * freqs[o % half][None, :]
    ang = ang.astype(np.float32)
    first = (o % ROPE_AXIS_DIM) < half
    cos = np.where(rotary[None, :], np.cos(ang), 1.0).astype(np.float32)
    sin = np.where(rotary[None, :], np.sin(ang), 0.0).astype(np.float32)
    return (jnp.asarray(cos), jnp.asarray(np.where(first[None, :], -sin, 0.0).astype(np.float32)),
            jnp.asarray(np.where(first[None, :], 0.0, sin).astype(np.float32)))


def kernel(x, c, ctx, c_ctx, norm_g, w_mod, b_mod, ffn_w_gate, ffn_w_up, ffn_w_down, ab_w_in, pool_w,
           pool_scale, q_norm_g, w_uq, kv_norm_g, w_ukv, ab_w_out, conv_w_in, conv_w, conv_w_out,
           final_norm_g):
    b, l, d = x.shape
    t_g = ctx.shape[1]
    tm_h, tm_g, tq, tm_mix = FFN_TILE_ROWS, CTX_MIX_TILE_ROWS, ATTN_TILE_ROWS, MIX_TILE_ROWS

    cond = jnp.zeros((MOD_ROWS, d), F32).at[:b].set(c).at[b].set(c_ctx)
    m = _modulation(cond, w_mod, b_mod)
    mod_h = [m[i, :b].reshape(b, N_MOD, d) for i in range(2)]
    mod_g = jnp.broadcast_to(m[0, b].reshape(1, N_MOD, d), (b, N_MOD, d))

    ffn_w = (ffn_w_gate, ffn_w_up, ffn_w_down)

    h = x.reshape(b * l, d)
    g = ctx.reshape(b * t_g, d)

    h = _ffn_half(h, mod_h[0], norm_g[0, 0], *ffn_w, (0, 0), k=0, tm=tm_h)
    g = _ffn_half(g, m[0, b].reshape(1, N_MOD, d), norm_g[0, 0], *ffn_w, (0, 0), k=0, tm=tm_h)

    kr_start = POOL_DIM + Q_LORA_RANK + KV_LORA_RANK
    w_kr = jnp.pad(ab_w_in[0, :, kr_start:], ((0, 0), (0, HEAD_PAD - QK_ROPE_DIM)))
    wq = _head_pad(w_uq[0], QK_HEAD_DIM, 0, QK_HEAD_DIM).astype(BF16)
    wk = _head_pad(w_ukv[0], QK_NOPE_DIM + V_HEAD_DIM, 0, QK_NOPE_DIM).T.astype(BF16)
    wv = w_ukv[0].reshape(KV_LORA_RANK, MLA_HEADS, QK_NOPE_DIM + V_HEAD_DIM)[:, :, QK_NOPE_DIM:]
    wv = wv.reshape(KV_LORA_RANK, ATTN_V_DIM).astype(BF16)
    lane = jnp.arange(HEAD_PAD)
    rotary = jnp.logical_and(lane >= QK_NOPE_DIM, lane < QK_HEAD_DIM)
    place = jnp.tile(jnp.where(rotary[:, None], jnp.eye(HEAD_PAD, dtype=F32), 0.0), (MLA_HEADS, 1)).astype(BF16)
    qg = q_norm_g[0].reshape(1, Q_LORA_RANK)
    kvg = kv_norm_g[0].reshape(1, KV_LORA_RANK)
    tabs = _rope_tables(l)

    pool, q, k_h, v_h = _mix_in(h, mod_h[0], norm_g[0, 1], ab_w_in, w_kr, kvg, wk, wv, place, tm_mix,
                                latent_args=(qg, wq, tabs))
    k_g, v_g = _mix_in(g, mod_g, norm_g[0, 1], ab_w_in, w_kr, kvg, wk, wv, place, tm_g)
    attn = _attention(q.reshape(b, l, -1), k_h, k_g, v_h.reshape(b, l, -1), v_g.reshape(b, t_g, -1), tq)
    h = _mix_out(h.reshape(b, l, d), mod_h[0], pool.reshape(b, l, -1), attn, pool_w,
                 pool_scale.reshape(-1, 1, POOL_DIM), ab_w_out, tm_mix)
    h = _ffn_half(h.reshape(b * l, d), mod_h[0], norm_g[0, 2], *ffn_w, (0, 1), k=2, tm=tm_h)

    h = _ffn_half(h, mod_h[1], norm_g[1, 0], *ffn_w, (1, 0), k=0, tm=tm_h)
    h = _conv_mixer(h.reshape(b, l, d), mod_h[1], norm_g[1, 1], conv_w_in, conv_w, conv_w_out, tm_mix)
    h = _ffn_half(h.reshape(b * l, d), mod_h[1], norm_g[1, 2], *ffn_w, (1, 1), k=2, tm=tm_h,
                  final_gain=final_norm_g)
    return h.reshape(b, l, d)
```

```python
import functools
import math

import jax
import jax.numpy as jnp
import numpy as np
from jax import lax
from jax.experimental import pallas as pl
from jax.experimental.pallas import tpu as pltpu

D_MODEL = 1024
SEQ = 2048
GRID_W = 64
CTX_LEN = 256
RMS_EPS = 1e-6
N_MOD = 9
D_FF = 2816
POOL_WINDOWS = (2, 4, 8, 16)
POOL_DIM = D_MODEL // 2
POOL_GROUP_DIM = POOL_DIM // len(POOL_WINDOWS)
MLA_HEADS = D_MODEL // 128
QK_NOPE_DIM = 64
QK_ROPE_DIM = 32
QK_HEAD_DIM = QK_NOPE_DIM + QK_ROPE_DIM
V_HEAD_DIM = 64
Q_LORA_RANK = 768
KV_LORA_RANK = 256
ROPE_AXIS_DIM = QK_ROPE_DIM // 2
ROPE_THETA = 10000.0
ATTN_SCALE = 1.0 / math.sqrt(QK_HEAD_DIM)
ATTN_V_DIM = MLA_HEADS * V_HEAD_DIM

LANES = 128
SUBLANES = 8
HEAD_PAD = LANES
QK_PAD_DIM = MLA_HEADS * HEAD_PAD
MOD_ROWS = 16
MOD_TILE_N = 3072
FF_CHUNK = 256
N_FF_CHUNKS = D_FF // FF_CHUNK
FF_HEAD_ROWS = 256
FF_TAIL_ROWS = 256
HALO = SUBLANES
PV_WIDTH = 256
PV_HEADS = PV_WIDTH // V_HEAD_DIM
ATTN_SUB_Q = 256
V7X_VMEM_BYTES = 64 * 1024 * 1024
VMEM_LIMIT = V7X_VMEM_BYTES * 7 // 8
FFN_TILE_ROWS = 1024
ATTN_TILE_ROWS = 512
MIX_TILE_ROWS = 1024
CTX_MIX_TILE_ROWS = CTX_LEN

BF16 = jnp.bfloat16
F32 = jnp.float32


def _sigmoid(x):
    return 1.0 / (1.0 + jnp.exp(-x))


def _rms(x):
    return x * lax.rsqrt(jnp.mean(x * x, axis=-1, keepdims=True) + RMS_EPS)


def _adaln(x, gain, mod_ref, k):
    shift = mod_ref[0, 3 * k:3 * k + 1, :]
    scale = mod_ref[0, 3 * k + 1:3 * k + 2, :]
    return _rms(x) * gain * (1.0 + scale) + shift


_NT = (((1,), (1,)), ((), ()))


def _mm(a, w):
    return lax.dot_general(a, w, (((1,), (0,)), ((), ())), preferred_element_type=F32)


def _params(semantics):
    return pltpu.CompilerParams(dimension_semantics=semantics, vmem_limit_bytes=VMEM_LIMIT)


def _mod_kernel(cond_ref, w_ref, b_ref, o_ref):
    cond = cond_ref[...]
    a = (cond * _sigmoid(cond)).astype(BF16)
    o_ref[0] = _mm(a, w_ref[0]) + b_ref[0]


def _modulation(cond, w_mod, b_mod):
    depth, d, n = w_mod.shape
    return pl.pallas_call(
        _mod_kernel,
        grid=(depth, n // MOD_TILE_N),
        in_specs=[
            pl.BlockSpec((MOD_ROWS, d), lambda i, j: (0, 0)),
            pl.BlockSpec((1, d, MOD_TILE_N), lambda i, j: (i, 0, j)),
            pl.BlockSpec((1, 1, MOD_TILE_N), lambda i, j: (i, 0, j)),
        ],
        out_specs=pl.BlockSpec((1, MOD_ROWS, MOD_TILE_N), lambda i, j: (i, 0, j)),
        out_shape=jax.ShapeDtypeStruct((depth, MOD_ROWS, n), F32),
        compiler_params=_params(("arbitrary", "arbitrary")),
        name="modulation",
    )(cond, w_mod, b_mod.reshape(depth, 1, n))


def _ffn_kernel(*refs, k, final, sel):
    if final:
        (s_ref, mod_ref, gain_ref, wg_hbm, wu_hbm, wd_hbm, fg_ref, o_ref,
         u_scr, a_scr, wg_ref, wu_ref, wd_ref, sem) = refs
    else:
        (s_ref, mod_ref, gain_ref, wg_hbm, wu_hbm, wd_hbm, o_ref,
         u_scr, a_scr, wg_ref, wu_ref, wd_ref, sem) = refs
    tm = s_ref.shape[0]

    def chunk_copies(f):
        cols = slice(f * FF_CHUNK, (f + 1) * FF_CHUNK)
        return (pltpu.make_async_copy(wg_hbm.at[sel[0], sel[1], :, cols], wg_ref.at[:, cols], sem.at[2 * f]),
                pltpu.make_async_copy(wu_hbm.at[sel[0], sel[1], :, cols], wu_ref.at[:, cols], sem.at[2 * f + 1]))

    def down_copy():
        return pltpu.make_async_copy(wd_hbm.at[sel[0], sel[1]], wd_ref, sem.at[2 * N_FF_CHUNKS])

    def hidden(rows, f):
        cols = slice(f * FF_CHUNK, (f + 1) * FF_CHUNK)
        g = _mm(u_scr[rows, :], wg_ref[:, cols])
        up = _mm(u_scr[rows, :], wu_ref[:, cols])
        a_scr[rows, cols] = (g * _sigmoid(g) * up).astype(BF16)

    def body(first_step):
        if first_step:
            for f in range(N_FF_CHUNKS):
                for cp in chunk_copies(f):
                    cp.start()
            down_copy().start()
        for p in range(tm // FF_HEAD_ROWS):
            rows = slice(p * FF_HEAD_ROWS, (p + 1) * FF_HEAD_ROWS)
            u_scr[rows, :] = _adaln(s_ref[rows, :], gain_ref[...], mod_ref, k).astype(BF16)
            if first_step and p == 0:
                for cp in chunk_copies(0):
                    cp.wait()
            hidden(rows, 0)
        for f in range(1, N_FF_CHUNKS):
            if first_step:
                for cp in chunk_copies(f):
                    cp.wait()
            hidden(slice(None), f)
        if first_step:
            down_copy().wait()
        gate = mod_ref[0, 3 * k + 2:3 * k + 3, :]
        for p in range(tm // FF_TAIL_ROWS):
            rows = slice(p * FF_TAIL_ROWS, (p + 1) * FF_TAIL_ROWS)
            y = s_ref[rows, :] + 0.5 * gate * _mm(a_scr[rows, :], wd_ref[...])
            if final:
                y = _rms(y) * fg_ref[...]
            o_ref[rows, :] = y

    pl.when(pl.program_id(0) == 0)(functools.partial(body, True))
    pl.when(pl.program_id(0) != 0)(functools.partial(body, False))


def _ffn_half(s, mod, gain, wg, wu, wd, sel, k, tm, final_gain=None):
    n_tok, d = s.shape
    rows_per_mod = n_tok // mod.shape[0]
    assert rows_per_mod % tm == 0
    final = final_gain is not None
    hbm = pl.BlockSpec(memory_space=pl.ANY)
    in_specs = [
        pl.BlockSpec((tm, d), lambda i: (i, 0)),
        pl.BlockSpec((1, N_MOD, d), lambda i: (i * tm // rows_per_mod, 0, 0)),
        pl.BlockSpec((1, d), lambda i: (0, 0)),
        hbm, hbm, hbm,
    ]
    args = [s, mod, gain.reshape(1, d), wg, wu, wd]
    if final:
        in_specs.append(pl.BlockSpec((1, d), lambda i: (0, 0)))
        args.append(final_gain.reshape(1, d))
    return pl.pallas_call(
        functools.partial(_ffn_kernel, k=k, final=final, sel=sel),
        grid=(n_tok // tm,),
        in_specs=in_specs,
        out_specs=pl.BlockSpec((tm, d), lambda i: (i, 0)),
        out_shape=jax.ShapeDtypeStruct((n_tok, d), F32),
        scratch_shapes=[pltpu.VMEM((tm, d), BF16), pltpu.VMEM((tm, D_FF), BF16),
                        pltpu.VMEM(wg.shape[2:], F32), pltpu.VMEM(wu.shape[2:], F32),
                        pltpu.VMEM(wd.shape[2:], F32),
                        pltpu.SemaphoreType.DMA((2 * N_FF_CHUNKS + 1,))],
        compiler_params=pltpu.CompilerParams(dimension_semantics=("arbitrary",),
                                             vmem_limit_bytes=62 * 1024 * 1024),
        name="ffn_half",
    )(*args)


def _rope(z, c_ref, s1_ref, s2_ref):
    fwd = pltpu.roll(z, HEAD_PAD - ROPE_AXIS_DIM // 2, axis=1)
    bwd = pltpu.roll(z, ROPE_AXIS_DIM // 2, axis=1)
    return z * c_ref[...] + fwd * s1_ref[...] + bwd * s2_ref[...]


def _mix_in_kernel(*refs, latent):
    if latent:
        (s_ref, mod_ref, gain_ref, w_in_ref, w_kr_ref, qg_ref, wq_ref, kvg_ref, wk_ref, wv_ref, place_ref,
         c_ref, s1_ref, s2_ref, pool_ref, q_ref, kt_ref, v_ref) = refs
    else:
        (s_ref, mod_ref, gain_ref, w_in_ref, w_kr_ref, kvg_ref, wk_ref, wv_ref, place_ref,
         kt_ref, v_ref) = refs
    u = _adaln(s_ref[...], gain_ref[...], mod_ref, 1).astype(BF16)
    cuts = (POOL_DIM, POOL_DIM + Q_LORA_RANK, POOL_DIM + Q_LORA_RANK + KV_LORA_RANK)
    z = _mm(u, w_in_ref[:, :cuts[2]])
    ckv = (_rms(z[:, cuts[1]:cuts[2]]) * kvg_ref[...]).astype(BF16)
    kr = pltpu.roll(_mm(u, w_kr_ref[...]), QK_NOPE_DIM, axis=1)
    if latent:
        pool_ref[...] = z[:, :cuts[0]]
        cq = (_rms(z[:, cuts[0]:cuts[1]]) * qg_ref[...]).astype(BF16)
        q = jnp.dot(cq, wq_ref[...], preferred_element_type=F32)
        for h in range(MLA_HEADS):
            sl = slice(h * HEAD_PAD, (h + 1) * HEAD_PAD)
            q_ref[:, sl] = _rope(q[:, sl], c_ref, s1_ref, s2_ref).astype(BF16)
        kr = _rope(kr, c_ref, s1_ref, s2_ref)
    kt = lax.dot_general(wk_ref[...], ckv, _NT, preferred_element_type=F32)
    kt = kt + lax.dot_general(place_ref[...], kr.astype(BF16), _NT, preferred_element_type=F32)
    kt_ref[0] = kt.astype(BF16)
    v_ref[...] = jnp.dot(ckv, wv_ref[...], preferred_element_type=F32).astype(BF16)


def _mix_in(s, mod, gain, w_in, w_kr, kvg, wk, wv, place, tm, latent_args=None):
    n_tok, d = s.shape
    tiles_per_batch = n_tok // mod.shape[0] // tm
    latent = latent_args is not None
    const = lambda a: pl.BlockSpec(a.shape, lambda i: (0,) * a.ndim)
    tok = lambda w: pl.BlockSpec((tm, w), lambda i: (i, 0))
    in_specs = [tok(d), pl.BlockSpec((1, N_MOD, d), lambda i: (i // tiles_per_batch, 0, 0)),
                pl.BlockSpec((1, d), lambda i: (0, 0)),
                pl.BlockSpec((None,) + w_in.shape[1:], lambda i: (0, 0, 0), pipeline_mode=pl.Buffered(1)),
                const(w_kr)]
    args = [s, mod, gain.reshape(1, d), w_in, w_kr]
    if latent:
        qg, wq, tabs = latent_args
        in_specs += [const(qg), const(wq)]
        args += [qg, wq]
    in_specs += [const(kvg), const(wk), const(wv), const(place)]
    args += [kvg, wk, wv, place]
    kt_spec = pl.BlockSpec((1, QK_PAD_DIM, tm), lambda i: (i // tiles_per_batch, 0, i % tiles_per_batch))
    out_specs = [kt_spec, tok(ATTN_V_DIM)]
    out_shape = [jax.ShapeDtypeStruct((mod.shape[0], QK_PAD_DIM, n_tok // mod.shape[0]), BF16),
                 jax.ShapeDtypeStruct((n_tok, ATTN_V_DIM), BF16)]
    if latent:
        rope_spec = pl.BlockSpec((tm, HEAD_PAD), lambda i: (i % tiles_per_batch, 0))
        in_specs += [rope_spec] * 3
        args += list(tabs)
        out_specs = [tok(POOL_DIM), tok(QK_PAD_DIM)] + out_specs
        out_shape = [jax.ShapeDtypeStruct((n_tok, POOL_DIM), F32),
                     jax.ShapeDtypeStruct((n_tok, QK_PAD_DIM), BF16)] + out_shape
    return pl.pallas_call(
        functools.partial(_mix_in_kernel, latent=latent),
        grid=(n_tok // tm,),
        in_specs=in_specs,
        out_specs=out_specs,
        out_shape=out_shape,
        compiler_params=_params(("arbitrary",)),
        name="mix_in_latent" if latent else "mix_in_context",
    )(*args)


def _attn_kernel(q_ref, kth_ref, ktg_ref, vh_ref, vg_ref, o_ref):
    c = ATTN_SCALE * math.log2(math.e)
    lane_head = lax.broadcasted_iota(jnp.int32, (1, PV_WIDTH), 1) // V_HEAD_DIM
    for sub in range(q_ref.shape[1] // ATTN_SUB_Q):
        rows = slice(sub * ATTN_SUB_Q, (sub + 1) * ATTN_SUB_Q)
        for grp in range(MLA_HEADS // PV_HEADS):
            vcols = slice(grp * PV_WIDTH, (grp + 1) * PV_WIDTH)
            acc = None
            for hh in range(PV_HEADS):
                h = grp * PV_HEADS + hh
                qk = slice(h * HEAD_PAD, (h + 1) * HEAD_PAD)
                q = q_ref[0, rows, qk]
                s_h = jnp.dot(q, kth_ref[0, qk, :], preferred_element_type=F32)
                s_g = jnp.dot(q, ktg_ref[0, qk, :], preferred_element_type=F32)
                m = jnp.maximum(jnp.max(s_h, axis=-1, keepdims=True),
                                jnp.max(s_g, axis=-1, keepdims=True))
                e_h = jnp.exp2((s_h - m) * c)
                e_g = jnp.exp2((s_g - m) * c)
                denom = jnp.sum(e_h, axis=-1, keepdims=True) + jnp.sum(e_g, axis=-1, keepdims=True)
                res = jnp.dot(e_h.astype(BF16), vh_ref[0, :, vcols], preferred_element_type=F32)
                res = res + jnp.dot(e_g.astype(BF16), vg_ref[0, :, vcols], preferred_element_type=F32)
                term = jnp.where(lane_head == hh, res * (1.0 / denom), 0.0)
                acc = term if acc is None else acc + term
            o_ref[0, rows, vcols] = acc.astype(BF16)


def _attention(q, kt_h, kt_g, v_h, v_g, tq):
    b, l, _ = q.shape
    t_g = kt_g.shape[2]
    return pl.pallas_call(
        _attn_kernel,
        grid=(b, l // tq),
        in_specs=[
            pl.BlockSpec((1, tq, QK_PAD_DIM), lambda i, j: (i, j, 0)),
            pl.BlockSpec((1, QK_PAD_DIM, l), lambda i, j: (i, 0, 0)),
            pl.BlockSpec((1, QK_PAD_DIM, t_g), lambda i, j: (i, 0, 0)),
            pl.BlockSpec((1, l, ATTN_V_DIM), lambda i, j: (i, 0, 0)),
            pl.BlockSpec((1, t_g, ATTN_V_DIM), lambda i, j: (i, 0, 0)),
        ],
        out_specs=pl.BlockSpec((1, tq, ATTN_V_DIM), lambda i, j: (i, j, 0)),
        out_shape=jax.ShapeDtypeStruct((b, l, ATTN_V_DIM), BF16),
        compiler_params=_params(("arbitrary", "arbitrary")),
        name="latent_attention",
    )(q, kt_h, kt_g, v_h, v_g)


def _mix_out_kernel(h_ref, mod_ref, pool_ref, attn_ref, pw_ref, ps_ref, wo_ref, o_ref, win_scr, y_scr,
                    *, tm, seq):
    t0 = pl.multiple_of(pl.program_id(1) * tm, tm)
    prev = pool_ref[0, pl.ds(pl.multiple_of(jnp.maximum(t0 - HALO, 0), HALO), HALO), :]
    nxt = pool_ref[0, pl.ds(pl.multiple_of(jnp.minimum(t0 + tm, seq - HALO), HALO), HALO), :]
    win_scr[0:HALO, :] = jnp.where(t0 > 0, prev, 0.0)
    win_scr[HALO:HALO + tm, :] = pool_ref[0, pl.ds(t0, tm), :]
    win_scr[HALO + tm:, :] = jnp.where(t0 + tm < seq, nxt, 0.0)
    t = (t0 + lax.broadcasted_iota(jnp.int32, (tm, 1), 0)).astype(F32)
    n = tm + 2 * HALO
    for g, w in enumerate(POOL_WINDOWS):
        lanes = slice(g * POOL_GROUP_DIM, (g + 1) * POOL_GROUP_DIM)
        fwd = win_scr[:, lanes]
        span = 1
        while span < w:
            fwd = fwd + pltpu.roll(fwd, n - span, axis=0)
            span *= 2
        total = pltpu.roll(fwd, w // 2, axis=0)[HALO:HALO + tm]
        cnt = jnp.minimum(t, float(w // 2)) + jnp.minimum(float(seq - 1) - t, float(w - w // 2 - 1)) + 1.0
        p = total / cnt - win_scr[HALO:HALO + tm, lanes]
        y = _mm(p.astype(BF16), pw_ref[g])
        y_scr[:, lanes] = (y * ps_ref[:, lanes]).astype(BF16)
    out = _mm(y_scr[...], wo_ref[:POOL_DIM, :]) + _mm(attn_ref[0], wo_ref[POOL_DIM:, :])
    gate = mod_ref[0, 3 * 1 + 2:3 * 1 + 3, :]
    o_ref[0] = h_ref[0] + gate * out


def _mix_out(h, mod, pool, attn, pool_w, pool_scale, w_out, tm):
    b, l, d = h.shape
    const = lambda a: pl.BlockSpec((None,) + a.shape[1:], lambda i, j: (0,) * a.ndim,
                                   pipeline_mode=pl.Buffered(1))
    return pl.pallas_call(
        functools.partial(_mix_out_kernel, tm=tm, seq=l),
        grid=(b, l // tm),
        in_specs=[
            pl.BlockSpec((1, tm, d), lambda i, j: (i, j, 0)),
            pl.BlockSpec((1, N_MOD, d), lambda i, j: (i, 0, 0)),
            pl.BlockSpec((1, l, POOL_DIM), lambda i, j: (i, 0, 0)),
            pl.BlockSpec((1, tm, ATTN_V_DIM), lambda i, j: (i, j, 0)),
            const(pool_w), const(pool_scale), const(w_out),
        ],
        out_specs=pl.BlockSpec((1, tm, d), lambda i, j: (i, j, 0)),
        out_shape=jax.ShapeDtypeStruct((b, l, d), F32),
        scratch_shapes=[pltpu.VMEM((tm + 2 * HALO, POOL_DIM), F32), pltpu.VMEM((tm, POOL_DIM), BF16)],
        compiler_params=_params(("arbitrary", "arbitrary")),
        name="mix_out",
    )(h, mod, pool, attn, pool_w, pool_scale, w_out)


def _conv_kernel(h_ref, hp_ref, hn_ref, mod_ref, gain_ref, w_in_ref, cw_ref, w_out_ref, o_ref,
                 u_scr, z_scr, *, tm, seq):
    t0 = pl.program_id(1) * tm
    gain = gain_ref[...]
    u_scr[0:HALO, :] = _adaln(hp_ref[0], gain, mod_ref, 1).astype(BF16)
    u_scr[HALO:HALO + tm, :] = _adaln(h_ref[0], gain, mod_ref, 1).astype(BF16)
    u_scr[HALO + tm:, :] = _adaln(hn_ref[0], gain, mod_ref, 1).astype(BF16)
    d = h_ref.shape[-1]
    cv = _mm(u_scr[...], w_in_ref[:, d:])
    t = t0 - HALO + lax.broadcasted_iota(jnp.int32, (tm + 2 * HALO, 1), 0)
    inside = jnp.logical_and(t >= 0, t < seq)
    z_scr[...] = jnp.where(inside, cv[:, :d] * cv[:, d:], 0.0)
    y = (cw_ref[0:1, :] * z_scr[HALO - 1:HALO - 1 + tm, :]
         + cw_ref[1:2, :] * z_scr[HALO:HALO + tm, :]
         + cw_ref[2:3, :] * z_scr[HALO + 1:HALO + 1 + tm, :])
    bg = _mm(u_scr[HALO:HALO + tm, :], w_in_ref[:, :d])
    out = _mm((bg * y).astype(BF16), w_out_ref[...])
    gate = mod_ref[0, 3 * 1 + 2:3 * 1 + 3, :]
    o_ref[0] = h_ref[0] + gate * out


def _conv_mixer(h, mod, gain, w_in, conv_w, w_out, tm):
    b, l, d = h.shape
    hb = tm // HALO
    const = lambda a: pl.BlockSpec((None,) + a.shape[1:], lambda i, j: (0,) * a.ndim,
                                   pipeline_mode=pl.Buffered(1))
    return pl.pallas_call(
        functools.partial(_conv_kernel, tm=tm, seq=l),
        grid=(b, l // tm),
        in_specs=[
            pl.BlockSpec((1, tm, d), lambda i, j: (i, j, 0)),
            pl.BlockSpec((1, HALO, d), lambda i, j: (i, jnp.maximum(j * hb - 1, 0), 0)),
            pl.BlockSpec((1, HALO, d), lambda i, j: (i, jnp.minimum((j + 1) * hb, l // HALO - 1), 0)),
            pl.BlockSpec((1, N_MOD, d), lambda i, j: (i, 0, 0)),
            pl.BlockSpec((1, d), lambda i, j: (0, 0)),
            const(w_in), const(conv_w), const(w_out),
        ],
        out_specs=pl.BlockSpec((1, tm, d), lambda i, j: (i, j, 0)),
        out_shape=jax.ShapeDtypeStruct((b, l, d), F32),
        scratch_shapes=[pltpu.VMEM((tm + 2 * HALO, d), BF16), pltpu.VMEM((tm + 2 * HALO, d), F32)],
        compiler_params=_params(("arbitrary", "arbitrary")),
        name="conv_mixer",
    )(h, h, h, mod, gain.reshape(1, d), w_in, conv_w, w_out)


def _head_pad(w, per_head, start, width):
    r = w.shape[0]
    w = w.reshape(r, MLA_HEADS, per_head)[:, :, start:start + width]
    return jnp.pad(w, ((0, 0), (0, 0), (0, HEAD_PAD - width))).reshape(r, QK_PAD_DIM)


def _rope_tables(length):
    pos = np.arange(length)
    row = (pos // GRID_W).astype(np.float32)
    col = (pos % GRID_W).astype(np.float32)
    half = ROPE_AXIS_DIM // 2
    freqs = np.power(np.float32(ROPE_THETA),
                     -np.arange(0, ROPE_AXIS_DIM, 2, dtype=np.float32) / np.float32(ROPE_AXIS_DIM))
    lane = np.arange(HEAD_PAD)
    o = lane - QK_NOPE_DIM
    rotary = np.logical_and(o >= 0, o < QK_ROPE_DIM)
    o = np.clip(o, 0, QK_ROPE_DIM - 1)
    ang = np.where((o // ROPE_AXIS_DIM == 0)[None, :], row[:, None], col[:, None]) * freqs[o % half][None, :]
    ang = ang.astype(np.float32)
    first = (o % ROPE_AXIS_DIM) < half
    cos = np.where(rotary[None, :], np.cos(ang), 1.0).astype(np.float32)
    sin = np.where(rotary[None, :], np.sin(ang), 0.0).astype(np.float32)
    return (jnp.asarray(cos), jnp.asarray(np.where(first[None, :], -sin, 0.0).astype(np.float32)),
            jnp.asarray(np.where(first[None, :], 0.0, sin).astype(np.float32)))


def kernel(x, c, ctx, c_ctx, norm_g, w_mod, b_mod, ffn_w_gate, ffn_w_up, ffn_w_down, ab_w_in, pool_w,
           pool_scale, q_norm_g, w_uq, kv_norm_g, w_ukv, ab_w_out, conv_w_in, conv_w, conv_w_out,
           final_norm_g):
    b, l, d = x.shape
    t_g = ctx.shape[1]
    tm_h, tm_g, tq, tm_mix = FFN_TILE_ROWS, CTX_MIX_TILE_ROWS, ATTN_TILE_ROWS, MIX_TILE_ROWS

    cond = jnp.zeros((MOD_ROWS, d), F32).at[:b].set(c).at[b].set(c_ctx)
    m = _modulation(cond, w_mod, b_mod)
    mod_h = [m[i, :b].reshape(b, N_MOD, d) for i in range(2)]
    mod_g = jnp.broadcast_to(m[0, b].reshape(1, N_MOD, d), (b, N_MOD, d))

    ffn_w = (ffn_w_gate, ffn_w_up, ffn_w_down)

    h = x.reshape(b * l, d)
    g = ctx.reshape(b * t_g, d)

    h = _ffn_half(h, mod_h[0], norm_g[0, 0], *ffn_w, (0, 0), k=0, tm=tm_h)
    g = _ffn_half(g, m[0, b].reshape(1, N_MOD, d), norm_g[0, 0], *ffn_w, (0, 0), k=0, tm=tm_h)

    kr_start = POOL_DIM + Q_LORA_RANK + KV_LORA_RANK
    w_kr = jnp.pad(ab_w_in[0, :, kr_start:], ((0, 0), (0, HEAD_PAD - QK_ROPE_DIM)))
    wq = _head_pad(w_uq[0], QK_HEAD_DIM, 0, QK_HEAD_DIM).astype(BF16)
    wk = _head_pad(w_ukv[0], QK_NOPE_DIM + V_HEAD_DIM, 0, QK_NOPE_DIM).T.astype(BF16)
    wv = w_ukv[0].reshape(KV_LORA_RANK, MLA_HEADS, QK_NOPE_DIM + V_HEAD_DIM)[:, :, QK_NOPE_DIM:]
    wv = wv.reshape(KV_LORA_RANK, ATTN_V_DIM).astype(BF16)
    lane = jnp.arange(HEAD_PAD)
    rotary = jnp.logical_and(lane >= QK_NOPE_DIM, lane < QK_HEAD_DIM)
    place = jnp.tile(jnp.where(rotary[:, None], jnp.eye(HEAD_PAD, dtype=F32), 0.0), (MLA_HEADS, 1)).astype(BF16)
    qg = q_norm_g[0].reshape(1, Q_LORA_RANK)
    kvg = kv_norm_g[0].reshape(1, KV_LORA_RANK)
    tabs = _rope_tables(l)

    pool, q, k_h, v_h = _mix_in(h, mod_h[0], norm_g[0, 1], ab_w_in, w_kr, kvg, wk, wv, place, tm_mix,
                                latent_args=(qg, wq, tabs))
    k_g, v_g = _mix_in(g, mod_g, norm_g[0, 1], ab_w_in, w_kr, kvg, wk, wv, place, tm_g)
    attn = _attention(q.reshape(b, l, -1), k_h, k_g, v_h.reshape(b, l, -1), v_g.reshape(b, t_g, -1), tq)
    h = _mix_out(h.reshape(b, l, d), mod_h[0], pool.reshape(b, l, -1), attn, pool_w,
                 pool_scale.reshape(-1, 1, POOL_DIM), ab_w_out, tm_mix)
    h = _ffn_half(h.reshape(b * l, d), mod_h[0], norm_g[0, 2], *ffn_w, (0, 1), k=2, tm=tm_h)

    h = _ffn_half(h, mod_h[1], norm_g[1, 0], *ffn_w, (1, 0), k=0, tm=tm_h)
    h = _conv_mixer(h.reshape(b, l, d), mod_h[1], norm_g[1, 1], conv_w_in, conv_w, conv_w_out, tm_mix)
    h = _ffn_half(h.reshape(b * l, d), mod_h[1], norm_g[1, 2], *ffn_w, (1, 1), k=2, tm=tm_h,
                  final_gain=final_norm_g)
    return h.reshape(b, l, d)
```

```python
import functools
import math

import jax
import jax.numpy as jnp
import numpy as np
from jax import lax
from jax.experimental import pallas as pl
from jax.experimental.pallas import tpu as pltpu

D_MODEL = 1024
GRID_W = 64
CTX_LEN = 256
RMS_EPS = 1e-6
N_MOD = 9
D_FF = 2816
POOL_WINDOWS = (2, 4, 8, 16)
POOL_DIM = D_MODEL // 2
POOL_GROUP_DIM = POOL_DIM // len(POOL_WINDOWS)
MLA_HEADS = D_MODEL // 128
QK_NOPE_DIM = 64
QK_ROPE_DIM = 32
QK_HEAD_DIM = QK_NOPE_DIM + QK_ROPE_DIM
V_HEAD_DIM = 64
Q_LORA_RANK = 768
KV_LORA_RANK = 256
ROPE_AXIS_DIM = QK_ROPE_DIM // 2
ROPE_THETA = 10000.0
ATTN_SCALE = 1.0 / math.sqrt(QK_HEAD_DIM)
ATTN_V_DIM = MLA_HEADS * V_HEAD_DIM

LANES = 128
SUBLANES = 8
HEAD_PAD = LANES
QK_PAD_DIM = MLA_HEADS * HEAD_PAD
MOD_ROWS = 16
MOD_TILE_N = 3072
FF_CHUNK = 256
N_FF_CHUNKS = D_FF // FF_CHUNK
FF_HEAD_ROWS = 256
FF_TAIL_ROWS = 256
HALO = SUBLANES
PV_WIDTH = 256
PV_HEADS = PV_WIDTH // V_HEAD_DIM
ATTN_SUB_Q = 256
V7X_VMEM_BYTES = 64 * 1024 * 1024
VMEM_LIMIT = V7X_VMEM_BYTES * 7 // 8
FFN_TILE_ROWS = 512
ATTN_TILE_ROWS = 512
MIX_TILE_ROWS = 1024
CTX_MIX_TILE_ROWS = CTX_LEN

BF16 = jnp.bfloat16
F32 = jnp.float32


def _sigmoid(x):
    return 1.0 / (1.0 + jnp.exp(-x))


def _rms(x):
    return x * lax.rsqrt(jnp.mean(x * x, axis=-1, keepdims=True) + RMS_EPS)


def _adaln(x, gain, mod_ref, k):
    shift = mod_ref[0, 3 * k:3 * k + 1, :]
    scale = mod_ref[0, 3 * k + 1:3 * k + 2, :]
    return _rms(x) * gain * (1.0 + scale) + shift


_NT = (((1,), (1,)), ((), ()))


def _mm(a, w):
    return lax.dot_general(a, w, (((1,), (0,)), ((), ())), preferred_element_type=F32)


def _params(semantics):
    return pltpu.CompilerParams(dimension_semantics=semantics, vmem_limit_bytes=VMEM_LIMIT)


def _mod_kernel(cond_ref, w_ref, b_ref, o_ref):
    cond = cond_ref[...]
    a = (cond * _sigmoid(cond)).astype(BF16)
    o_ref[0] = _mm(a, w_ref[0]) + b_ref[0]


def _modulation(cond, w_mod, b_mod):
    depth, d, n = w_mod.shape
    return pl.pallas_call(
        _mod_kernel,
        grid=(depth, n // MOD_TILE_N),
        in_specs=[
            pl.BlockSpec((MOD_ROWS, d), lambda i, j: (0, 0)),
            pl.BlockSpec((1, d, MOD_TILE_N), lambda i, j: (i, 0, j)),
            pl.BlockSpec((1, 1, MOD_TILE_N), lambda i, j: (i, 0, j)),
        ],
        out_specs=pl.BlockSpec((1, MOD_ROWS, MOD_TILE_N), lambda i, j: (i, 0, j)),
        out_shape=jax.ShapeDtypeStruct((depth, MOD_ROWS, n), F32),
        compiler_params=_params(("arbitrary", "arbitrary")),
        name="modulation",
    )(cond, w_mod, b_mod.reshape(depth, 1, n))


def _ffn_kernel(*refs, k, final, sel):
    if final:
        (s_ref, mod_ref, gain_ref, wg_hbm, wu_hbm, wd_hbm, fg_ref, o_ref,
         u_scr, a_scr, wg_ref, wu_ref, wd_ref, sem) = refs
    else:
        (s_ref, mod_ref, gain_ref, wg_hbm, wu_hbm, wd_hbm, o_ref,
         u_scr, a_scr, wg_ref, wu_ref, wd_ref, sem) = refs
    tm = s_ref.shape[0]

    def chunk_copies(f):
        cols = slice(f * FF_CHUNK, (f + 1) * FF_CHUNK)
        return (pltpu.make_async_copy(wg_hbm.at[sel[0], sel[1], :, cols], wg_ref.at[:, cols], sem.at[2 * f]),
                pltpu.make_async_copy(wu_hbm.at[sel[0], sel[1], :, cols], wu_ref.at[:, cols], sem.at[2 * f + 1]))

    def down_copy():
        return pltpu.make_async_copy(wd_hbm.at[sel[0], sel[1]], wd_ref, sem.at[2 * N_FF_CHUNKS])

    def hidden(rows, f):
        cols = slice(f * FF_CHUNK, (f + 1) * FF_CHUNK)
        g = _mm(u_scr[rows, :], wg_ref[:, cols])
        up = _mm(u_scr[rows, :], wu_ref[:, cols])
        a_scr[rows, cols] = (g * _sigmoid(g) * up).astype(BF16)

    def body(first_step):
        if first_step:
            for f in range(N_FF_CHUNKS):
                for cp in chunk_copies(f):
                    cp.start()
            down_copy().start()
        for p in range(tm // FF_HEAD_ROWS):
            rows = slice(p * FF_HEAD_ROWS, (p + 1) * FF_HEAD_ROWS)
            u_scr[rows, :] = _adaln(s_ref[rows, :], gain_ref[...], mod_ref, k).astype(BF16)
            if first_step and p == 0:
                for cp in chunk_copies(0):
                    cp.wait()
            hidden(rows, 0)
        for f in range(1, N_FF_CHUNKS):
            if first_step:
                for cp in chunk_copies(f):
                    cp.wait()
            hidden(slice(None), f)
        if first_step:
            down_copy().wait()
        gate = mod_ref[0, 3 * k + 2:3 * k + 3, :]
        for p in range(tm // FF_TAIL_ROWS):
            rows = slice(p * FF_TAIL_ROWS, (p + 1) * FF_TAIL_ROWS)
            y = s_ref[rows, :] + 0.5 * gate * _mm(a_scr[rows, :], wd_ref[...])
            if final:
                y = _rms(y) * fg_ref[...]
            o_ref[rows, :] = y

    pl.when(pl.program_id(0) == 0)(functools.partial(body, True))
    pl.when(pl.program_id(0) != 0)(functools.partial(body, False))


def _ffn_half(s, mod, gain, wg, wu, wd, sel, k, tm, final_gain=None):
    n_tok, d = s.shape
    rows_per_mod = n_tok // mod.shape[0]
    assert rows_per_mod % tm == 0
    final = final_gain is not None
    hbm = pl.BlockSpec(memory_space=pl.ANY)
    in_specs = [
        pl.BlockSpec((tm, d), lambda i: (i, 0)),
        pl.BlockSpec((1, N_MOD, d), lambda i: (i * tm // rows_per_mod, 0, 0)),
        pl.BlockSpec((1, d), lambda i: (0, 0)),
        hbm, hbm, hbm,
    ]
    args = [s, mod, gain.reshape(1, d), wg, wu, wd]
    if final:
        in_specs.append(pl.BlockSpec((1, d), lambda i: (0, 0)))
        args.append(final_gain.reshape(1, d))
    return pl.pallas_call(
        functools.partial(_ffn_kernel, k=k, final=final, sel=sel),
        grid=(n_tok // tm,),
        in_specs=in_specs,
        out_specs=pl.BlockSpec((tm, d), lambda i: (i, 0)),
        out_shape=jax.ShapeDtypeStruct((n_tok, d), F32),
        scratch_shapes=[pltpu.VMEM((tm, d), BF16), pltpu.VMEM((tm, D_FF), BF16),
                        pltpu.VMEM(wg.shape[2:], F32), pltpu.VMEM(wu.shape[2:], F32),
                        pltpu.VMEM(wd.shape[2:], F32),
                        pltpu.SemaphoreType.DMA((2 * N_FF_CHUNKS + 1,))],
        compiler_params=_params(("arbitrary",)),
        name="ffn_half",
    )(*args)


def _rope(z, c_ref, s1_ref, s2_ref):
    fwd = pltpu.roll(z, HEAD_PAD - ROPE_AXIS_DIM // 2, axis=1)
    bwd = pltpu.roll(z, ROPE_AXIS_DIM // 2, axis=1)
    return z * c_ref[...] + fwd * s1_ref[...] + bwd * s2_ref[...]


def _mix_in_kernel(*refs, latent):
    if latent:
        (s_ref, mod_ref, gain_ref, w_in_ref, w_kr_ref, qg_ref, wq_ref, kvg_ref, wk_ref, wv_ref, place_ref,
         c_ref, s1_ref, s2_ref, pool_ref, q_ref, kt_ref, v_ref) = refs
    else:
        (s_ref, mod_ref, gain_ref, w_in_ref, w_kr_ref, kvg_ref, wk_ref, wv_ref, place_ref,
         kt_ref, v_ref) = refs
    u = _adaln(s_ref[...], gain_ref[...], mod_ref, 1).astype(BF16)
    cuts = (POOL_DIM, POOL_DIM + Q_LORA_RANK, POOL_DIM + Q_LORA_RANK + KV_LORA_RANK)
    z = _mm(u, w_in_ref[:, :cuts[2]])
    ckv = (_rms(z[:, cuts[1]:cuts[2]]) * kvg_ref[...]).astype(BF16)
    kr = pltpu.roll(_mm(u, w_kr_ref[...]), QK_NOPE_DIM, axis=1)
    if latent:
        pool_ref[...] = z[:, :cuts[0]]
        cq = (_rms(z[:, cuts[0]:cuts[1]]) * qg_ref[...]).astype(BF16)
        q = jnp.dot(cq, wq_ref[...], preferred_element_type=F32)
        for h in range(MLA_HEADS):
            sl = slice(h * HEAD_PAD, (h + 1) * HEAD_PAD)
            q_ref[:, sl] = _rope(q[:, sl], c_ref, s1_ref, s2_ref).astype(BF16)
        kr = _rope(kr, c_ref, s1_ref, s2_ref)
    kt = lax.dot_general(wk_ref[...], ckv, _NT, preferred_element_type=F32)
    kt = kt + lax.dot_general(place_ref[...], kr.astype(BF16), _NT, preferred_element_type=F32)
    kt_ref[0] = kt.astype(BF16)
    v_ref[...] = jnp.dot(ckv, wv_ref[...], preferred_element_type=F32).astype(BF16)


def _mix_in(s, mod, gain, w_in, w_kr, kvg, wk, wv, place, tm, latent_args=None):
    n_tok, d = s.shape
    tiles_per_batch = n_tok // mod.shape[0] // tm
    latent = latent_args is not None
    const = lambda a: pl.BlockSpec(a.shape, lambda i: (0,) * a.ndim)
    tok = lambda w: pl.BlockSpec((tm, w), lambda i: (i, 0))
    in_specs = [tok(d), pl.BlockSpec((1, N_MOD, d), lambda i: (i // tiles_per_batch, 0, 0)),
                pl.BlockSpec((1, d), lambda i: (0, 0)),
                pl.BlockSpec((None,) + w_in.shape[1:], lambda i: (0, 0, 0), pipeline_mode=pl.Buffered(1)),
                const(w_kr)]
    args = [s, mod, gain.reshape(1, d), w_in, w_kr]
    if latent:
        qg, wq, tabs = latent_args
        in_specs += [const(qg), const(wq)]
        args += [qg, wq]
    in_specs += [const(kvg), const(wk), const(wv), const(place)]
    args += [kvg, wk, wv, place]
    kt_spec = pl.BlockSpec((1, QK_PAD_DIM, tm), lambda i: (i // tiles_per_batch, 0, i % tiles_per_batch))
    out_specs = [kt_spec, tok(ATTN_V_DIM)]
    out_shape = [jax.ShapeDtypeStruct((mod.shape[0], QK_PAD_DIM, n_tok // mod.shape[0]), BF16),
                 jax.ShapeDtypeStruct((n_tok, ATTN_V_DIM), BF16)]
    if latent:
        rope_spec = pl.BlockSpec((tm, HEAD_PAD), lambda i: (i % tiles_per_batch, 0))
        in_specs += [rope_spec] * 3
        args += list(tabs)
        out_specs = [tok(POOL_DIM), tok(QK_PAD_DIM)] + out_specs
        out_shape = [jax.ShapeDtypeStruct((n_tok, POOL_DIM), F32),
                     jax.ShapeDtypeStruct((n_tok, QK_PAD_DIM), BF16)] + out_shape
    return pl.pallas_call(
        functools.partial(_mix_in_kernel, latent=latent),
        grid=(n_tok // tm,),
        in_specs=in_specs,
        out_specs=out_specs,
        out_shape=out_shape,
        compiler_params=_params(("arbitrary",)),
        name="mix_in_latent" if latent else "mix_in_context",
    )(*args)


def _attn_kernel(q_ref, kth_ref, ktg_ref, vh_ref, vg_ref, o_ref):
    c = ATTN_SCALE * math.log2(math.e)
    lane_head = lax.broadcasted_iota(jnp.int32, (1, PV_WIDTH), 1) // V_HEAD_DIM
    for sub in range(q_ref.shape[1] // ATTN_SUB_Q):
        rows = slice(sub * ATTN_SUB_Q, (sub + 1) * ATTN_SUB_Q)
        for grp in range(MLA_HEADS // PV_HEADS):
            vcols = slice(grp * PV_WIDTH, (grp + 1) * PV_WIDTH)
            acc = None
            for hh in range(PV_HEADS):
                h = grp * PV_HEADS + hh
                qk = slice(h * HEAD_PAD, (h + 1) * HEAD_PAD)
                q = q_ref[0, rows, qk]
                s_h = jnp.dot(q, kth_ref[0, qk, :], preferred_element_type=F32)
                s_g = jnp.dot(q, ktg_ref[0, qk, :], preferred_element_type=F32)
                m = jnp.maximum(jnp.max(s_h, axis=-1, keepdims=True),
                                jnp.max(s_g, axis=-1, keepdims=True))
                e_h = jnp.exp2((s_h - m) * c)
                e_g = jnp.exp2((s_g - m) * c)
                denom = jnp.sum(e_h, axis=-1, keepdims=True) + jnp.sum(e_g, axis=-1, keepdims=True)
                res = jnp.dot(e_h.astype(BF16), vh_ref[0, :, vcols], preferred_element_type=F32)
                res = res + jnp.dot(e_g.astype(BF16), vg_ref[0, :, vcols], preferred_element_type=F32)
                term = jnp.where(lane_head == hh, res * (1.0 / denom), 0.0)
                acc = term if acc is None else acc + term
            o_ref[0, rows, vcols] = acc.astype(BF16)


def _attention(q, kt_h, kt_g, v_h, v_g, tq):
    b, l, _ = q.shape
    t_g = kt_g.shape[2]
    return pl.pallas_call(
        _attn_kernel,
        grid=(b, l // tq),
        in_specs=[
            pl.BlockSpec((1, tq, QK_PAD_DIM), lambda i, j: (i, j, 0)),
            pl.BlockSpec((1, QK_PAD_DIM, l), lambda i, j: (i, 0, 0)),
            pl.BlockSpec((1, QK_PAD_DIM, t_g), lambda i, j: (i, 0, 0)),
            pl.BlockSpec((1, l, ATTN_V_DIM), lambda i, j: (i, 0, 0)),
            pl.BlockSpec((1, t_g, ATTN_V_DIM), lambda i, j: (i, 0, 0)),
        ],
        out_specs=pl.BlockSpec((1, tq, ATTN_V_DIM), lambda i, j: (i, j, 0)),
        out_shape=jax.ShapeDtypeStruct((b, l, ATTN_V_DIM), BF16),
        compiler_params=_params(("arbitrary", "arbitrary")),
        name="latent_attention",
    )(q, kt_h, kt_g, v_h, v_g)


def _mix_out_kernel(h_ref, mod_ref, pool_ref, poolp_ref, pooln_ref, attn_ref, pw_ref, ps_ref, wo_ref,
                    o_ref, win_scr, y_scr, *, tm, seq):
    t0 = pl.program_id(1) * tm
    win_scr[0:HALO, :] = jnp.where(t0 > 0, poolp_ref[0], 0.0)
    win_scr[HALO:HALO + tm, :] = pool_ref[0]
    win_scr[HALO + tm:, :] = jnp.where(t0 + tm < seq, pooln_ref[0], 0.0)
    t = (t0 + lax.broadcasted_iota(jnp.int32, (tm, 1), 0)).astype(F32)
    n = tm + 2 * HALO
    for g, w in enumerate(POOL_WINDOWS):
        lanes = slice(g * POOL_GROUP_DIM, (g + 1) * POOL_GROUP_DIM)
        fwd = win_scr[:, lanes]
        span = 1
        while span < w:
            fwd = fwd + pltpu.roll(fwd, n - span, axis=0)
            span *= 2
        total = pltpu.roll(fwd, w // 2, axis=0)[HALO:HALO + tm]
        cnt = jnp.minimum(t, float(w // 2)) + jnp.minimum(float(seq - 1) - t, float(w - w // 2 - 1)) + 1.0
        p = total / cnt - win_scr[HALO:HALO + tm, lanes]
        y = _mm(p.astype(BF16), pw_ref[g])
        y_scr[:, lanes] = (y * ps_ref[:, lanes]).astype(BF16)
    out = _mm(y_scr[...], wo_ref[:POOL_DIM, :]) + _mm(attn_ref[0], wo_ref[POOL_DIM:, :])
    gate = mod_ref[0, 3 * 1 + 2:3 * 1 + 3, :]
    o_ref[0] = h_ref[0] + gate * out


def _mix_out(h, mod, pool, attn, pool_w, pool_scale, w_out, tm):
    b, l, d = h.shape
    hb = tm // HALO
    const = lambda a: pl.BlockSpec((None,) + a.shape[1:], lambda i, j: (0,) * a.ndim,
                                   pipeline_mode=pl.Buffered(1))
    return pl.pallas_call(
        functools.partial(_mix_out_kernel, tm=tm, seq=l),
        grid=(b, l // tm),
        in_specs=[
            pl.BlockSpec((1, tm, d), lambda i, j: (i, j, 0)),
            pl.BlockSpec((1, N_MOD, d), lambda i, j: (i, 0, 0)),
            pl.BlockSpec((1, tm, POOL_DIM), lambda i, j: (i, j, 0)),
            pl.BlockSpec((1, HALO, POOL_DIM), lambda i, j: (i, jnp.maximum(j * hb - 1, 0), 0)),
            pl.BlockSpec((1, HALO, POOL_DIM), lambda i, j: (i, jnp.minimum((j + 1) * hb, l // HALO - 1), 0)),
            pl.BlockSpec((1, tm, ATTN_V_DIM), lambda i, j: (i, j, 0)),
            const(pool_w), const(pool_scale), const(w_out),
        ],
        out_specs=pl.BlockSpec((1, tm, d), lambda i, j: (i, j, 0)),
        out_shape=jax.ShapeDtypeStruct((b, l, d), F32),
        scratch_shapes=[pltpu.VMEM((tm + 2 * HALO, POOL_DIM), F32), pltpu.VMEM((tm, POOL_DIM), BF16)],
        compiler_params=_params(("arbitrary", "arbitrary")),
        name="mix_out",
    )(h, mod, pool, pool, pool, attn, pool_w, pool_scale, w_out)


def _conv_kernel(h_ref, hp_ref, hn_ref, mod_ref, gain_ref, w_in_ref, cw_ref, w_out_ref, o_ref,
                 u_scr, z_scr, *, tm, seq):
    t0 = pl.program_id(1) * tm
    gain = gain_ref[...]
    u_scr[0:HALO, :] = _adaln(hp_ref[0], gain, mod_ref, 1).astype(BF16)
    u_scr[HALO:HALO + tm, :] = _adaln(h_ref[0], gain, mod_ref, 1).astype(BF16)
    u_scr[HALO + tm:, :] = _adaln(hn_ref[0], gain, mod_ref, 1).astype(BF16)
    d = h_ref.shape[-1]
    cv = _mm(u_scr[...], w_in_ref[:, d:])
    t = t0 - HALO + lax.broadcasted_iota(jnp.int32, (tm + 2 * HALO, 1), 0)
    inside = jnp.logical_and(t >= 0, t < seq)
    z_scr[...] = jnp.where(inside, cv[:, :d] * cv[:, d:], 0.0)
    y = (cw_ref[0:1, :] * z_scr[HALO - 1:HALO - 1 + tm, :]
         + cw_ref[1:2, :] * z_scr[HALO:HALO + tm, :]
         + cw_ref[2:3, :] * z_scr[HALO + 1:HALO + 1 + tm, :])
    bg = _mm(u_scr[HALO:HALO + tm, :], w_in_ref[:, :d])
    out = _mm((bg * y).astype(BF16), w_out_ref[...])
    gate = mod_ref[0, 3 * 1 + 2:3 * 1 + 3, :]
    o_ref[0] = h_ref[0] + gate * out


def _conv_mixer(h, mod, gain, w_in, conv_w, w_out, tm):
    b, l, d = h.shape
    hb = tm // HALO
    const = lambda a: pl.BlockSpec((None,) + a.shape[1:], lambda i, j: (0,) * a.ndim,
                                   pipeline_mode=pl.Buffered(1))
    return pl.pallas_call(
        functools.partial(_conv_kernel, tm=tm, seq=l),
        grid=(b, l // tm),
        in_specs=[
            pl.BlockSpec((1, tm, d), lambda i, j: (i, j, 0)),
            pl.BlockSpec((1, HALO, d), lambda i, j: (i, jnp.maximum(j * hb - 1, 0), 0)),
            pl.BlockSpec((1, HALO, d), lambda i, j: (i, jnp.minimum((j + 1) * hb, l // HALO - 1), 0)),
            pl.BlockSpec((1, N_MOD, d), lambda i, j: (i, 0, 0)),
            pl.BlockSpec((1, d), lambda i, j: (0, 0)),
            const(w_in), const(conv_w), const(w_out),
        ],
        out_specs=pl.BlockSpec((1, tm, d), lambda i, j: (i, j, 0)),
        out_shape=jax.ShapeDtypeStruct((b, l, d), F32),
        scratch_shapes=[pltpu.VMEM((tm + 2 * HALO, d), BF16), pltpu.VMEM((tm + 2 * HALO, d), F32)],
        compiler_params=_params(("arbitrary", "arbitrary")),
        name="conv_mixer",
    )(h, h, h, mod, gain.reshape(1, d), w_in, conv_w, w_out)


def _head_pad(w, per_head, start, width):
    r = w.shape[0]
    w = w.reshape(r, MLA_HEADS, per_head)[:, :, start:start + width]
    return jnp.pad(w, ((0, 0), (0, 0), (0, HEAD_PAD - width))).reshape(r, QK_PAD_DIM)


def _rope_tables(length):
    pos = np.arange(length)
    row = (pos // GRID_W).astype(np.float32)
    col = (pos % GRID_W).astype(np.float32)
    half = ROPE_AXIS_DIM // 2
    freqs = np.power(np.float32(ROPE_THETA),
                     -np.arange(0, ROPE_AXIS_DIM, 2, dtype=np.float32) / np.float32(ROPE_AXIS_DIM))
    lane = np.arange(HEAD_PAD)
    o = lane - QK_NOPE_DIM
    rotary = np.logical_and(o >= 0, o < QK_ROPE_DIM)
    o = np.clip(o, 0, QK_ROPE_DIM - 1)
    ang = np.where((o // ROPE_AXIS_DIM == 0)[None, :], row[:, None], col[:, None]) * freqs[o % half][None, :]
    ang = ang.astype(np.float32)
    first = (o % ROPE_AXIS_DIM) < half
    cos = np.where(rotary[None, :], np.cos(ang), 1.0).astype(np.float32)
    sin = np.where(rotary[None, :], np.sin(ang), 0.0).astype(np.float32)
    return (jnp.asarray(cos), jnp.asarray(np.where(first[None, :], -sin, 0.0).astype(np.float32)),
            jnp.asarray(np.where(first[None, :], 0.0, sin).astype(np.float32)))


def kernel(x, c, ctx, c_ctx, norm_g, w_mod, b_mod, ffn_w_gate, ffn_w_up, ffn_w_down, ab_w_in, pool_w,
           pool_scale, q_norm_g, w_uq, kv_norm_g, w_ukv, ab_w_out, conv_w_in, conv_w, conv_w_out,
           final_norm_g):
    b, l, d = x.shape
    t_g = ctx.shape[1]
    tm_h, tm_g, tq, tm_mix = FFN_TILE_ROWS, CTX_MIX_TILE_ROWS, ATTN_TILE_ROWS, MIX_TILE_ROWS
    assert w_mod.shape[0] == 2 and ab_w_in.shape[0] == 1 and conv_w_in.shape[0] == 1
    assert l % tm_h == 0 and l % tq == 0 and l % tm_mix == 0 and t_g == tm_g and l % GRID_W == 0

    cond = jnp.zeros((MOD_ROWS, d), F32).at[:b].set(c).at[b].set(c_ctx)
    m = _modulation(cond, w_mod, b_mod)
    mod_h = [m[i, :b].reshape(b, N_MOD, d) for i in range(2)]
    mod_g = jnp.broadcast_to(m[0, b].reshape(1, N_MOD, d), (b, N_MOD, d))

    ffn_w = (ffn_w_gate, ffn_w_up, ffn_w_down)

    h = x.reshape(b * l, d)
    g = ctx.reshape(b * t_g, d)

    h = _ffn_half(h, mod_h[0], norm_g[0, 0], *ffn_w, (0, 0), k=0, tm=tm_h)
    g = _ffn_half(g, m[0, b].reshape(1, N_MOD, d), norm_g[0, 0], *ffn_w, (0, 0), k=0, tm=tm_h)

    kr_start = POOL_DIM + Q_LORA_RANK + KV_LORA_RANK
    w_kr = jnp.pad(ab_w_in[0, :, kr_start:], ((0, 0), (0, HEAD_PAD - QK_ROPE_DIM)))
    wq = _head_pad(w_uq[0], QK_HEAD_DIM, 0, QK_HEAD_DIM).astype(BF16)
    wk = _head_pad(w_ukv[0], QK_NOPE_DIM + V_HEAD_DIM, 0, QK_NOPE_DIM).T.astype(BF16)
    wv = w_ukv[0].reshape(KV_LORA_RANK, MLA_HEADS, QK_NOPE_DIM + V_HEAD_DIM)[:, :, QK_NOPE_DIM:]
    wv = wv.reshape(KV_LORA_RANK, ATTN_V_DIM).astype(BF16)
    lane = jnp.arange(HEAD_PAD)
    rotary = jnp.logical_and(lane >= QK_NOPE_DIM, lane < QK_HEAD_DIM)
    place = jnp.tile(jnp.where(rotary[:, None], jnp.eye(HEAD_PAD, dtype=F32), 0.0), (MLA_HEADS, 1)).astype(BF16)
    qg = q_norm_g[0].reshape(1, Q_LORA_RANK)
    kvg = kv_norm_g[0].reshape(1, KV_LORA_RANK)
    tabs = _rope_tables(l)

    pool, q, k_h, v_h = _mix_in(h, mod_h[0], norm_g[0, 1], ab_w_in, w_kr, kvg, wk, wv, place, tm_mix,
                                latent_args=(qg, wq, tabs))
    k_g, v_g = _mix_in(g, mod_g, norm_g[0, 1], ab_w_in, w_kr, kvg, wk, wv, place, tm_g)
    attn = _attention(q.reshape(b, l, -1), k_h, k_g, v_h.reshape(b, l, -1), v_g.reshape(b, t_g, -1), tq)
    h = _mix_out(h.reshape(b, l, d), mod_h[0], pool.reshape(b, l, -1), attn, pool_w,
                 pool_scale.reshape(-1, 1, POOL_DIM), ab_w_out, tm_mix)
    h = _ffn_half(h.reshape(b * l, d), mod_h[0], norm_g[0, 2], *ffn_w, (0, 1), k=2, tm=tm_h)

    h = _ffn_half(h, mod_h[1], norm_g[1, 0], *ffn_w, (1, 0), k=0, tm=tm_h)
    h = _conv_mixer(h.reshape(b, l, d), mod_h[1], norm_g[1, 1], conv_w_in, conv_w, conv_w_out, tm_mix)
    h = _ffn_half(h.reshape(b * l, d), mod_h[1], norm_g[1, 2], *ffn_w, (1, 1), k=2, tm=tm_h,
                  final_gain=final_norm_g)
    return h.reshape(b, l, d)
```

```python
import functools
import math

import jax
import jax.numpy as jnp
import numpy as np
from jax import lax
from jax.experimental import pallas as pl
from jax.experimental.pallas import tpu as pltpu

D_MODEL = 1024
GRID_W = 64
CTX_LEN = 256
RMS_EPS = 1e-6
N_MOD = 9
D_FF = 2816
POOL_WINDOWS = (2, 4, 8, 16)
POOL_DIM = D_MODEL // 2
POOL_GROUP_DIM = POOL_DIM // len(POOL_WINDOWS)
MLA_HEADS = D_MODEL // 128
QK_NOPE_DIM = 64
QK_ROPE_DIM = 32
QK_HEAD_DIM = QK_NOPE_DIM + QK_ROPE_DIM
V_HEAD_DIM = 64
Q_LORA_RANK = 768
KV_LORA_RANK = 256
ROPE_AXIS_DIM = QK_ROPE_DIM // 2
ROPE_THETA = 10000.0
ATTN_SCALE = 1.0 / math.sqrt(QK_HEAD_DIM)
ATTN_V_DIM = MLA_HEADS * V_HEAD_DIM

LANES = 128
SUBLANES = 8
HEAD_PAD = LANES
QK_PAD_DIM = MLA_HEADS * HEAD_PAD
MOD_ROWS = 16
MOD_TILE_N = 3072
SIDE_MOD_TILE_N = 384
FF_CHUNK = 256
N_FF_CHUNKS = D_FF // FF_CHUNK
FF_HEAD_ROWS = 256
FF_TAIL_ROWS = 256
HALO = SUBLANES
PV_WIDTH = 256
PV_HEADS = PV_WIDTH // V_HEAD_DIM
ATTN_SUB_Q = 256
V7X_VMEM_BYTES = 64 * 1024 * 1024
VMEM_LIMIT = V7X_VMEM_BYTES * 7 // 8
FFN_TILE_ROWS = 512
ATTN_TILE_ROWS = 512
MIX_TILE_ROWS = 1024
CTX_MIX_TILE_ROWS = CTX_LEN

BF16 = jnp.bfloat16
F32 = jnp.float32


def _sigmoid(x):
    return 1.0 / (1.0 + jnp.exp(-x))


def _rms(x):
    return x * lax.rsqrt(jnp.mean(x * x, axis=-1, keepdims=True) + RMS_EPS)


def _adaln(x, gain, mod_ref, k):
    shift = mod_ref[0, 3 * k:3 * k + 1, :]
    scale = mod_ref[0, 3 * k + 1:3 * k + 2, :]
    return _rms(x) * gain * (1.0 + scale) + shift


_NT = (((1,), (1,)), ((), ()))


def _mm(a, w):
    return lax.dot_general(a, w, (((1,), (0,)), ((), ())), preferred_element_type=F32)


def _params(semantics):
    return pltpu.CompilerParams(dimension_semantics=semantics, vmem_limit_bytes=VMEM_LIMIT)


def _mod_kernel(cond_ref, w_ref, b_ref, o_ref):
    cond = cond_ref[...]
    a = (cond * _sigmoid(cond)).astype(BF16)
    o_ref[0] = _mm(a, w_ref[0]) + b_ref[0]


def _modulation(cond, w_mod, b_mod, n_layers):
    depth, d, n = w_mod.shape
    return pl.pallas_call(
        _mod_kernel,
        grid=(n_layers, n // MOD_TILE_N),
        in_specs=[
            pl.BlockSpec((MOD_ROWS, d), lambda i, j: (0, 0)),
            pl.BlockSpec((1, d, MOD_TILE_N), lambda i, j: (i, 0, j)),
            pl.BlockSpec((1, 1, MOD_TILE_N), lambda i, j: (i, 0, j)),
        ],
        out_specs=pl.BlockSpec((1, MOD_ROWS, MOD_TILE_N), lambda i, j: (i, 0, j)),
        out_shape=jax.ShapeDtypeStruct((n_layers, MOD_ROWS, n), F32),
        compiler_params=_params(("arbitrary", "arbitrary")),
        name="modulation",
    )(cond, w_mod, b_mod.reshape(depth, 1, n))


def _ffn_kernel(*refs, k, final, sel, n_side):
    refs = list(refs)
    s_ref, mod_ref, gain_ref, wg_hbm, wu_hbm, wd_hbm = refs[:6]
    u_scr, a_scr, wg_ref, wu_ref, wd_ref, sem = refs[-6:]
    extra = refs[6:-6]
    fg_ref = extra.pop(0) if final else None
    if n_side:
        cond_ref, wm_ref, bm_ref, o_ref, om_ref = extra
    else:
        (o_ref,) = extra
    tm = s_ref.shape[0]

    if n_side:
        @pl.when(pl.program_id(0) < n_side)
        def _():
            cond = cond_ref[...]
            om_ref[...] = _mm((cond * _sigmoid(cond)).astype(BF16), wm_ref[...]) + bm_ref[...]

    def chunk_copies(f):
        cols = slice(f * FF_CHUNK, (f + 1) * FF_CHUNK)
        return (pltpu.make_async_copy(wg_hbm.at[sel[0], sel[1], :, cols], wg_ref.at[:, cols], sem.at[2 * f]),
                pltpu.make_async_copy(wu_hbm.at[sel[0], sel[1], :, cols], wu_ref.at[:, cols], sem.at[2 * f + 1]))

    def down_copy():
        return pltpu.make_async_copy(wd_hbm.at[sel[0], sel[1]], wd_ref, sem.at[2 * N_FF_CHUNKS])

    def hidden(rows, f):
        cols = slice(f * FF_CHUNK, (f + 1) * FF_CHUNK)
        g = _mm(u_scr[rows, :], wg_ref[:, cols])
        up = _mm(u_scr[rows, :], wu_ref[:, cols])
        a_scr[rows, cols] = (g * _sigmoid(g) * up).astype(BF16)

    def body(first_step):
        if first_step:
            for f in range(N_FF_CHUNKS):
                for cp in chunk_copies(f):
                    cp.start()
            down_copy().start()
        for p in range(tm // FF_HEAD_ROWS):
            rows = slice(p * FF_HEAD_ROWS, (p + 1) * FF_HEAD_ROWS)
            u_scr[rows, :] = _adaln(s_ref[rows, :], gain_ref[...], mod_ref, k).astype(BF16)
            if first_step and p == 0:
                for cp in chunk_copies(0):
                    cp.wait()
            hidden(rows, 0)
        for f in range(1, N_FF_CHUNKS):
            if first_step:
                for cp in chunk_copies(f):
                    cp.wait()
            hidden(slice(None), f)
        if first_step:
            down_copy().wait()
        gate = mod_ref[0, 3 * k + 2:3 * k + 3, :]
        for p in range(tm // FF_TAIL_ROWS):
            rows = slice(p * FF_TAIL_ROWS, (p + 1) * FF_TAIL_ROWS)
            y = s_ref[rows, :] + 0.5 * gate * _mm(a_scr[rows, :], wd_ref[...])
            if final:
                y = _rms(y) * fg_ref[...]
            o_ref[rows, :] = y

    pl.when(pl.program_id(0) == 0)(functools.partial(body, True))
    pl.when(pl.program_id(0) != 0)(functools.partial(body, False))


def _ffn_half(s, mod, gain, wg, wu, wd, sel, k, tm, final_gain=None, side_mod=None):
    n_tok, d = s.shape
    rows_per_mod = n_tok // mod.shape[0]
    assert rows_per_mod % tm == 0
    final = final_gain is not None
    hbm = pl.BlockSpec(memory_space=pl.ANY)
    in_specs = [
        pl.BlockSpec((tm, d), lambda i: (i, 0)),
        pl.BlockSpec((1, N_MOD, d), lambda i: (i * tm // rows_per_mod, 0, 0)),
        pl.BlockSpec((1, d), lambda i: (0, 0)),
        hbm, hbm, hbm,
    ]
    args = [s, mod, gain.reshape(1, d), wg, wu, wd]
    if final:
        in_specs.append(pl.BlockSpec((1, d), lambda i: (0, 0)))
        args.append(final_gain.reshape(1, d))
    out_specs = pl.BlockSpec((tm, d), lambda i: (i, 0))
    out_shape = jax.ShapeDtypeStruct((n_tok, d), F32)
    n_side = 0
    if side_mod is not None:
        cond, w_mod, b_mod, layer = side_mod
        n_mod = w_mod.shape[-1]
        n_side = n_mod // SIDE_MOD_TILE_N
        assert n_side <= n_tok // tm
        blk = lambda i: jnp.minimum(i, n_side - 1)
        in_specs += [pl.BlockSpec(cond.shape, lambda i: (0, 0)),
                     pl.BlockSpec((None, d, SIDE_MOD_TILE_N), lambda i: (layer, 0, blk(i))),
                     pl.BlockSpec((None, 1, SIDE_MOD_TILE_N), lambda i: (layer, 0, blk(i)))]
        args += [cond, w_mod, b_mod.reshape(b_mod.shape[0], 1, n_mod)]
        out_specs = [out_specs, pl.BlockSpec((cond.shape[0], SIDE_MOD_TILE_N), lambda i: (0, blk(i)))]
        out_shape = [out_shape, jax.ShapeDtypeStruct((cond.shape[0], n_mod), F32)]
    return pl.pallas_call(
        functools.partial(_ffn_kernel, k=k, final=final, sel=sel, n_side=n_side),
        grid=(n_tok // tm,),
        in_specs=in_specs,
        out_specs=out_specs,
        out_shape=out_shape,
        scratch_shapes=[pltpu.VMEM((tm, d), BF16), pltpu.VMEM((tm, D_FF), BF16),
                        pltpu.VMEM(wg.shape[2:], F32), pltpu.VMEM(wu.shape[2:], F32),
                        pltpu.VMEM(wd.shape[2:], F32),
                        pltpu.SemaphoreType.DMA((2 * N_FF_CHUNKS + 1,))],
        compiler_params=_params(("arbitrary",)),
        name="ffn_half",
    )(*args)


def _rope(z, c_ref, s1_ref, s2_ref):
    fwd = pltpu.roll(z, HEAD_PAD - ROPE_AXIS_DIM // 2, axis=1)
    bwd = pltpu.roll(z, ROPE_AXIS_DIM // 2, axis=1)
    return z * c_ref[...] + fwd * s1_ref[...] + bwd * s2_ref[...]


def _mix_in_kernel(*refs, latent):
    if latent:
        (s_ref, mod_ref, gain_ref, w_in_ref, w_kr_ref, qg_ref, wq_ref, kvg_ref, wk_ref, wv_ref, place_ref,
         c_ref, s1_ref, s2_ref, pool_ref, q_ref, kt_ref, v_ref) = refs
    else:
        (s_ref, mod_ref, gain_ref, w_in_ref, w_kr_ref, kvg_ref, wk_ref, wv_ref, place_ref,
         kt_ref, v_ref) = refs
    u = _adaln(s_ref[...], gain_ref[...], mod_ref, 1).astype(BF16)
    cuts = (POOL_DIM, POOL_DIM + Q_LORA_RANK, POOL_DIM + Q_LORA_RANK + KV_LORA_RANK)
    z = _mm(u, w_in_ref[:, :cuts[2]])
    ckv = (_rms(z[:, cuts[1]:cuts[2]]) * kvg_ref[...]).astype(BF16)
    kr = pltpu.roll(_mm(u, w_kr_ref[...]), QK_NOPE_DIM, axis=1)
    if latent:
        pool_ref[...] = z[:, :cuts[0]]
        cq = (_rms(z[:, cuts[0]:cuts[1]]) * qg_ref[...]).astype(BF16)
        q = jnp.dot(cq, wq_ref[...], preferred_element_type=F32)
        for h in range(MLA_HEADS):
            sl = slice(h * HEAD_PAD, (h + 1) * HEAD_PAD)
            q_ref[:, sl] = _rope(q[:, sl], c_ref, s1_ref, s2_ref).astype(BF16)
        kr = _rope(kr, c_ref, s1_ref, s2_ref)
    kt = lax.dot_general(wk_ref[...], ckv, _NT, preferred_element_type=F32)
    kt = kt + lax.dot_general(place_ref[...], kr.astype(BF16), _NT, preferred_element_type=F32)
    kt_ref[0] = kt.astype(BF16)
    v_ref[...] = jnp.dot(ckv, wv_ref[...], preferred_element_type=F32).astype(BF16)


def _mix_in(s, mod, gain, w_in, w_kr, kvg, wk, wv, place, tm, latent_args=None):
    n_tok, d = s.shape
    tiles_per_batch = n_tok // mod.shape[0] // tm
    latent = latent_args is not None
    const = lambda a: pl.BlockSpec(a.shape, lambda i: (0,) * a.ndim)
    tok = lambda w: pl.BlockSpec((tm, w), lambda i: (i, 0))
    in_specs = [tok(d), pl.BlockSpec((1, N_MOD, d), lambda i: (i // tiles_per_batch, 0, 0)),
                pl.BlockSpec((1, d), lambda i: (0, 0)),
                pl.BlockSpec((None,) + w_in.shape[1:], lambda i: (0, 0, 0), pipeline_mode=pl.Buffered(1)),
                const(w_kr)]
    args = [s, mod, gain.reshape(1, d), w_in, w_kr]
    if latent:
        qg, wq, tabs = latent_args
        in_specs += [const(qg), const(wq)]
        args += [qg, wq]
    in_specs += [const(kvg), const(wk), const(wv), const(place)]
    args += [kvg, wk, wv, place]
    kt_spec = pl.BlockSpec((1, QK_PAD_DIM, tm), lambda i: (i // tiles_per_batch, 0, i % tiles_per_batch))
    out_specs = [kt_spec, tok(ATTN_V_DIM)]
    out_shape = [jax.ShapeDtypeStruct((mod.shape[0], QK_PAD_DIM, n_tok // mod.shape[0]), BF16),
                 jax.ShapeDtypeStruct((n_tok, ATTN_V_DIM), BF16)]
    if latent:
        rope_spec = pl.BlockSpec((tm, HEAD_PAD), lambda i: (i % tiles_per_batch, 0))
        in_specs += [rope_spec] * 3
        args += list(tabs)
        out_specs = [tok(POOL_DIM), tok(QK_PAD_DIM)] + out_specs
        out_shape = [jax.ShapeDtypeStruct((n_tok, POOL_DIM), F32),
                     jax.ShapeDtypeStruct((n_tok, QK_PAD_DIM), BF16)] + out_shape
    return pl.pallas_call(
        functools.partial(_mix_in_kernel, latent=latent),
        grid=(n_tok // tm,),
        in_specs=in_specs,
        out_specs=out_specs,
        out_shape=out_shape,
        compiler_params=_params(("arbitrary",)),
        name="mix_in_latent" if latent else "mix_in_context",
    )(*args)


def _attn_kernel(q_ref, kth_ref, ktg_ref, vh_ref, vg_ref, o_ref):
    c = ATTN_SCALE * math.log2(math.e)
    lane_head = lax.broadcasted_iota(jnp.int32, (1, PV_WIDTH), 1) // V_HEAD_DIM
    for sub in range(q_ref.shape[1] // ATTN_SUB_Q):
        rows = slice(sub * ATTN_SUB_Q, (sub + 1) * ATTN_SUB_Q)
        for grp in range(MLA_HEADS // PV_HEADS):
            vcols = slice(grp * PV_WIDTH, (grp + 1) * PV_WIDTH)
            acc = None
            for hh in range(PV_HEADS):
                h = grp * PV_HEADS + hh
                qk = slice(h * HEAD_PAD, (h + 1) * HEAD_PAD)
                q = q_ref[0, rows, qk]
                s_h = jnp.dot(q, kth_ref[0, qk, :], preferred_element_type=F32)
                s_g = jnp.dot(q, ktg_ref[0, qk, :], preferred_element_type=F32)
                m = jnp.maximum(jnp.max(s_h, axis=-1, keepdims=True),
                                jnp.max(s_g, axis=-1, keepdims=True))
                e_h = jnp.exp2((s_h - m) * c)
                e_g = jnp.exp2((s_g - m) * c)
                denom = jnp.sum(e_h, axis=-1, keepdims=True) + jnp.sum(e_g, axis=-1, keepdims=True)
                res = jnp.dot(e_h.astype(BF16), vh_ref[0, :, vcols], preferred_element_type=F32)
                res = res + jnp.dot(e_g.astype(BF16), vg_ref[0, :, vcols], preferred_element_type=F32)
                term = jnp.where(lane_head == hh, res * (1.0 / denom), 0.0)
                acc = term if acc is None else acc + term
            o_ref[0, rows, vcols] = acc.astype(BF16)


def _attention(q, kt_h, kt_g, v_h, v_g, tq):
    b, l, _ = q.shape
    t_g = kt_g.shape[2]
    return pl.pallas_call(
        _attn_kernel,
        grid=(b, l // tq),
        in_specs=[
            pl.BlockSpec((1, tq, QK_PAD_DIM), lambda i, j: (i, j, 0)),
            pl.BlockSpec((1, QK_PAD_DIM, l), lambda i, j: (i, 0, 0)),
            pl.BlockSpec((1, QK_PAD_DIM, t_g), lambda i, j: (i, 0, 0)),
            pl.BlockSpec((1, l, ATTN_V_DIM), lambda i, j: (i, 0, 0)),
            pl.BlockSpec((1, t_g, ATTN_V_DIM), lambda i, j: (i, 0, 0)),
        ],
        out_specs=pl.BlockSpec((1, tq, ATTN_V_DIM), lambda i, j: (i, j, 0)),
        out_shape=jax.ShapeDtypeStruct((b, l, ATTN_V_DIM), BF16),
        compiler_params=_params(("arbitrary", "arbitrary")),
        name="latent_attention",
    )(q, kt_h, kt_g, v_h, v_g)


def _mix_out_kernel(h_ref, mod_ref, pool_ref, poolp_ref, pooln_ref, attn_ref, pw_ref, ps_ref, wo_ref,
                    o_ref, win_scr, y_scr, *, tm, seq):
    t0 = pl.program_id(1) * tm
    win_scr[0:HALO, :] = jnp.where(t0 > 0, poolp_ref[0], 0.0)
    win_scr[HALO:HALO + tm, :] = pool_ref[0]
    win_scr[HALO + tm:, :] = jnp.where(t0 + tm < seq, pooln_ref[0], 0.0)
    t = (t0 + lax.broadcasted_iota(jnp.int32, (tm, 1), 0)).astype(F32)
    n = tm + 2 * HALO
    for g, w in enumerate(POOL_WINDOWS):
        lanes = slice(g * POOL_GROUP_DIM, (g + 1) * POOL_GROUP_DIM)
        fwd = win_scr[:, lanes]
        span = 1
        while span < w:
            fwd = fwd + pltpu.roll(fwd, n - span, axis=0)
            span *= 2
        total = pltpu.roll(fwd, w // 2, axis=0)[HALO:HALO + tm]
        cnt = jnp.minimum(t, float(w // 2)) + jnp.minimum(float(seq - 1) - t, float(w - w // 2 - 1)) + 1.0
        p = total / cnt - win_scr[HALO:HALO + tm, lanes]
        y = _mm(p.astype(BF16), pw_ref[g])
        y_scr[:, lanes] = (y * ps_ref[:, lanes]).astype(BF16)
    out = _mm(y_scr[...], wo_ref[:POOL_DIM, :]) + _mm(attn_ref[0], wo_ref[POOL_DIM:, :])
    gate = mod_ref[0, 3 * 1 + 2:3 * 1 + 3, :]
    o_ref[0] = h_ref[0] + gate * out


def _mix_out(h, mod, pool, attn, pool_w, pool_scale, w_out, tm):
    b, l, d = h.shape
    hb = tm // HALO
    const = lambda a: pl.BlockSpec((None,) + a.shape[1:], lambda i, j: (0,) * a.ndim,
                                   pipeline_mode=pl.Buffered(1))
    return pl.pallas_call(
        functools.partial(_mix_out_kernel, tm=tm, seq=l),
        grid=(b, l // tm),
        in_specs=[
            pl.BlockSpec((1, tm, d), lambda i, j: (i, j, 0)),
            pl.BlockSpec((1, N_MOD, d), lambda i, j: (i, 0, 0)),
            pl.BlockSpec((1, tm, POOL_DIM), lambda i, j: (i, j, 0)),
            pl.BlockSpec((1, HALO, POOL_DIM), lambda i, j: (i, jnp.maximum(j * hb - 1, 0), 0)),
            pl.BlockSpec((1, HALO, POOL_DIM), lambda i, j: (i, jnp.minimum((j + 1) * hb, l // HALO - 1), 0)),
            pl.BlockSpec((1, tm, ATTN_V_DIM), lambda i, j: (i, j, 0)),
            const(pool_w), const(pool_scale), const(w_out),
        ],
        out_specs=pl.BlockSpec((1, tm, d), lambda i, j: (i, j, 0)),
        out_shape=jax.ShapeDtypeStruct((b, l, d), F32),
        scratch_shapes=[pltpu.VMEM((tm + 2 * HALO, POOL_DIM), F32), pltpu.VMEM((tm, POOL_DIM), BF16)],
        compiler_params=_params(("arbitrary", "arbitrary")),
        name="mix_out",
    )(h, mod, pool, pool, pool, attn, pool_w, pool_scale, w_out)


def _conv_kernel(h_ref, hp_ref, hn_ref, mod_ref, gain_ref, w_in_ref, cw_ref, w_out_ref, o_ref,
                 u_scr, z_scr, *, tm, seq):
    t0 = pl.program_id(1) * tm
    gain = gain_ref[...]
    u_scr[0:HALO, :] = _adaln(hp_ref[0], gain, mod_ref, 1).astype(BF16)
    u_scr[HALO:HALO + tm, :] = _adaln(h_ref[0], gain, mod_ref, 1).astype(BF16)
    u_scr[HALO + tm:, :] = _adaln(hn_ref[0], gain, mod_ref, 1).astype(BF16)
    d = h_ref.shape[-1]
    cv = _mm(u_scr[...], w_in_ref[:, d:])
    t = t0 - HALO + lax.broadcasted_iota(jnp.int32, (tm + 2 * HALO, 1), 0)
    inside = jnp.logical_and(t >= 0, t < seq)
    z_scr[...] = jnp.where(inside, cv[:, :d] * cv[:, d:], 0.0)
    y = (cw_ref[0:1, :] * z_scr[HALO - 1:HALO - 1 + tm, :]
         + cw_ref[1:2, :] * z_scr[HALO:HALO + tm, :]
         + cw_ref[2:3, :] * z_scr[HALO + 1:HALO + 1 + tm, :])
    bg = _mm(u_scr[HALO:HALO + tm, :], w_in_ref[:, :d])
    out = _mm((bg * y).astype(BF16), w_out_ref[...])
    gate = mod_ref[0, 3 * 1 + 2:3 * 1 + 3, :]
    o_ref[0] = h_ref[0] + gate * out


def _conv_mixer(h, mod, gain, w_in, conv_w, w_out, tm):
    b, l, d = h.shape
    hb = tm // HALO
    const = lambda a: pl.BlockSpec((None,) + a.shape[1:], lambda i, j: (0,) * a.ndim,
                                   pipeline_mode=pl.Buffered(1))
    return pl.pallas_call(
        functools.partial(_conv_kernel, tm=tm, seq=l),
        grid=(b, l // tm),
        in_specs=[
            pl.BlockSpec((1, tm, d), lambda i, j: (i, j, 0)),
            pl.BlockSpec((1, HALO, d), lambda i, j: (i, jnp.maximum(j * hb - 1, 0), 0)),
            pl.BlockSpec((1, HALO, d), lambda i, j: (i, jnp.minimum((j + 1) * hb, l // HALO - 1), 0)),
            pl.BlockSpec((1, N_MOD, d), lambda i, j: (i, 0, 0)),
            pl.BlockSpec((1, d), lambda i, j: (0, 0)),
            const(w_in), const(conv_w), const(w_out),
        ],
        out_specs=pl.BlockSpec((1, tm, d), lambda i, j: (i, j, 0)),
        out_shape=jax.ShapeDtypeStruct((b, l, d), F32),
        scratch_shapes=[pltpu.VMEM((tm + 2 * HALO, d), BF16), pltpu.VMEM((tm + 2 * HALO, d), F32)],
        compiler_params=_params(("arbitrary", "arbitrary")),
        name="conv_mixer",
    )(h, h, h, mod, gain.reshape(1, d), w_in, conv_w, w_out)


def _head_pad(w, per_head, start, width):
    r = w.shape[0]
    w = w.reshape(r, MLA_HEADS, per_head)[:, :, start:start + width]
    return jnp.pad(w, ((0, 0), (0, 0), (0, HEAD_PAD - width))).reshape(r, QK_PAD_DIM)


def _rope_tables(length):
    pos = np.arange(length)
    row = (pos // GRID_W).astype(np.float32)
    col = (pos % GRID_W).astype(np.float32)
    half = ROPE_AXIS_DIM // 2
    freqs = np.power(np.float32(ROPE_THETA),
                     -np.arange(0, ROPE_AXIS_DIM, 2, dtype=np.float32) / np.float32(ROPE_AXIS_DIM))
    lane = np.arange(HEAD_PAD)
    o = lane - QK_NOPE_DIM
    rotary = np.logical_and(o >= 0, o < QK_ROPE_DIM)
    o = np.clip(o, 0, QK_ROPE_DIM - 1)
    ang = np.where((o // ROPE_AXIS_DIM == 0)[None, :], row[:, None], col[:, None]) * freqs[o % half][None, :]
    ang = ang.astype(np.float32)
    first = (o % ROPE_AXIS_DIM) < half
    cos = np.where(rotary[None, :], np.cos(ang), 1.0).astype(np.float32)
    sin = np.where(rotary[None, :], np.sin(ang), 0.0).astype(np.float32)
    return (jnp.asarray(cos), jnp.asarray(np.where(first[None, :], -sin, 0.0).astype(np.float32)),
            jnp.asarray(np.where(first[None, :], 0.0, sin).astype(np.float32)))


def kernel(x, c, ctx, c_ctx, norm_g, w_mod, b_mod, ffn_w_gate, ffn_w_up, ffn_w_down, ab_w_in, pool_w,
           pool_scale, q_norm_g, w_uq, kv_norm_g, w_ukv, ab_w_out, conv_w_in, conv_w, conv_w_out,
           final_norm_g):
    b, l, d = x.shape
    t_g = ctx.shape[1]
    tm_h, tm_g, tq, tm_mix = FFN_TILE_ROWS, CTX_MIX_TILE_ROWS, ATTN_TILE_ROWS, MIX_TILE_ROWS
    assert w_mod.shape[0] == 2 and ab_w_in.shape[0] == 1 and conv_w_in.shape[0] == 1
    assert l % tm_h == 0 and l % tq == 0 and l % tm_mix == 0 and t_g == tm_g and l % GRID_W == 0

    cond = jnp.zeros((MOD_ROWS, d), F32).at[:b].set(c).at[b].set(c_ctx)
    m = _modulation(cond, w_mod, b_mod, 1)
    mod_h = [m[0, :b].reshape(b, N_MOD, d), None]
    mod_g = jnp.broadcast_to(m[0, b].reshape(1, N_MOD, d), (b, N_MOD, d))

    ffn_w = (ffn_w_gate, ffn_w_up, ffn_w_down)

    h = x.reshape(b * l, d)
    g = ctx.reshape(b * t_g, d)

    h, m1 = _ffn_half(h, mod_h[0], norm_g[0, 0], *ffn_w, (0, 0), k=0, tm=tm_h,
                      side_mod=(cond, w_mod, b_mod, 1))
    mod_h[1] = m1[:b].reshape(b, N_MOD, d)
    g = _ffn_half(g, m[0, b].reshape(1, N_MOD, d), norm_g[0, 0], *ffn_w, (0, 0), k=0, tm=tm_h)

    kr_start = POOL_DIM + Q_LORA_RANK + KV_LORA_RANK
    w_kr = jnp.pad(ab_w_in[0, :, kr_start:], ((0, 0), (0, HEAD_PAD - QK_ROPE_DIM)))
    wq = _head_pad(w_uq[0], QK_HEAD_DIM, 0, QK_HEAD_DIM).astype(BF16)
    wk = _head_pad(w_ukv[0], QK_NOPE_DIM + V_HEAD_DIM, 0, QK_NOPE_DIM).T.astype(BF16)
    wv = w_ukv[0].reshape(KV_LORA_RANK, MLA_HEADS, QK_NOPE_DIM + V_HEAD_DIM)[:, :, QK_NOPE_DIM:]
    wv = wv.reshape(KV_LORA_RANK, ATTN_V_DIM).astype(BF16)
    lane = jnp.arange(HEAD_PAD)
    rotary = jnp.logical_and(lane >= QK_NOPE_DIM, lane < QK_HEAD_DIM)
    place = jnp.tile(jnp.where(rotary[:, None], jnp.eye(HEAD_PAD, dtype=F32), 0.0), (MLA_HEADS, 1)).astype(BF16)
    qg = q_norm_g[0].reshape(1, Q_LORA_RANK)
    kvg = kv_norm_g[0].reshape(1, KV_LORA_RANK)
    tabs = _rope_tables(l)

    pool, q, k_h, v_h = _mix_in(h, mod_h[0], norm_g[0, 1], ab_w_in, w_kr, kvg, wk, wv, place, tm_mix,
                                latent_args=(qg, wq, tabs))
    k_g, v_g = _mix_in(g, mod_g, norm_g[0, 1], ab_w_in, w_kr, kvg, wk, wv, place, tm_g)
    attn = _attention(q.reshape(b, l, -1), k_h, k_g, v_h.reshape(b, l, -1), v_g.reshape(b, t_g, -1), tq)
    h = _mix_out(h.reshape(b, l, d), mod_h[0], pool.reshape(b, l, -1), attn, pool_w,
                 pool_scale.reshape(-1, 1, POOL_DIM), ab_w_out, tm_mix)
    h = _ffn_half(h.reshape(b * l, d), mod_h[0], norm_g[0, 2], *ffn_w, (0, 1), k=2, tm=tm_h)

    h = _ffn_half(h, mod_h[1], norm_g[1, 0], *ffn_w, (1, 0), k=0, tm=tm_h)
    h = _conv_mixer(h.reshape(b, l, d), mod_h[1], norm_g[1, 1], conv_w_in, conv_w, conv_w_out, tm_mix)
    h = _ffn_half(h.reshape(b * l, d), mod_h[1], norm_g[1, 2], *ffn_w, (1, 1), k=2, tm=tm_h,
                  final_gain=final_norm_g)
    return h.reshape(b, l, d)
```

```python
import functools
import math

import jax
import jax.numpy as jnp
import numpy as np
from jax import lax
from jax.experimental import pallas as pl
from jax.experimental.pallas import tpu as pltpu

D_MODEL = 1024
GRID_W = 64
CTX_LEN = 256
RMS_EPS = 1e-6
N_MOD = 9
D_FF = 2816
POOL_WINDOWS = (2, 4, 8, 16)
POOL_DIM = D_MODEL // 2
POOL_GROUP_DIM = POOL_DIM // len(POOL_WINDOWS)
MLA_HEADS = D_MODEL // 128
QK_NOPE_DIM = 64
QK_ROPE_DIM = 32
QK_HEAD_DIM = QK_NOPE_DIM + QK_ROPE_DIM
V_HEAD_DIM = 64
Q_LORA_RANK = 768
KV_LORA_RANK = 256
ROPE_AXIS_DIM = QK_ROPE_DIM // 2
ROPE_THETA = 10000.0
ATTN_SCALE = 1.0 / math.sqrt(QK_HEAD_DIM)
ATTN_V_DIM = MLA_HEADS * V_HEAD_DIM

LANES = 128
SUBLANES = 8
HEAD_PAD = LANES
QK_PAD_DIM = MLA_HEADS * HEAD_PAD
MOD_ROWS = 16
MOD_TILE_N = 3072
SIDE_MOD_TILE_N = 384
FF_CHUNK = 256
N_FF_CHUNKS = D_FF // FF_CHUNK
FF_HEAD_ROWS = 256
FF_TAIL_ROWS = 256
HALO = SUBLANES
PV_WIDTH = 256
PV_HEADS = PV_WIDTH // V_HEAD_DIM
ATTN_SUB_Q = 256
V7X_VMEM_BYTES = 64 * 1024 * 1024
VMEM_LIMIT = V7X_VMEM_BYTES * 7 // 8
FFN_TILE_ROWS = 512
ATTN_TILE_ROWS = 512
MIX_TILE_ROWS = 1024
CTX_MIX_TILE_ROWS = CTX_LEN

BF16 = jnp.bfloat16
F32 = jnp.float32


def _sigmoid(x):
    return 1.0 / (1.0 + jnp.exp(-x))


def _rms(x):
    return x * lax.rsqrt(jnp.mean(x * x, axis=-1, keepdims=True) + RMS_EPS)


def _adaln(x, gain, mod_ref, k):
    shift = mod_ref[0, 3 * k:3 * k + 1, :]
    scale = mod_ref[0, 3 * k + 1:3 * k + 2, :]
    return _rms(x) * gain * (1.0 + scale) + shift


_NT = (((1,), (1,)), ((), ()))


def _mm(a, w):
    return lax.dot_general(a, w, (((1,), (0,)), ((), ())), preferred_element_type=F32)


def _params(semantics):
    return pltpu.CompilerParams(dimension_semantics=semantics, vmem_limit_bytes=VMEM_LIMIT)


def _mod_kernel(cond_ref, w_ref, b_ref, o_ref):
    cond = cond_ref[...]
    a = (cond * _sigmoid(cond)).astype(BF16)
    o_ref[0] = _mm(a, w_ref[0]) + b_ref[0]


def _modulation(cond, w_mod, b_mod, n_layers):
    depth, d, n = w_mod.shape
    return pl.pallas_call(
        _mod_kernel,
        grid=(n_layers, n // MOD_TILE_N),
        in_specs=[
            pl.BlockSpec((MOD_ROWS, d), lambda i, j: (0, 0)),
            pl.BlockSpec((1, d, MOD_TILE_N), lambda i, j: (i, 0, j)),
            pl.BlockSpec((1, 1, MOD_TILE_N), lambda i, j: (i, 0, j)),
        ],
        out_specs=pl.BlockSpec((1, MOD_ROWS, MOD_TILE_N), lambda i, j: (i, 0, j)),
        out_shape=jax.ShapeDtypeStruct((n_layers, MOD_ROWS, n), F32),
        compiler_params=_params(("arbitrary", "arbitrary")),
        name="modulation",
    )(cond, w_mod, b_mod.reshape(depth, 1, n))


def _ffn_kernel(*refs, k, final, sel, n_side):
    refs = list(refs)
    s_ref, mod_ref, gain_ref, wg_hbm, wu_hbm, wd_hbm = refs[:6]
    u_scr, a_scr, wg_ref, wu_ref, wd_ref, sem = refs[-6:]
    extra = refs[6:-6]
    fg_ref = extra.pop(0) if final else None
    if n_side:
        cond_ref, wm_ref, bm_ref, o_ref, om_ref = extra
    else:
        (o_ref,) = extra
    tm = s_ref.shape[0]

    def chunk_copies(f):
        cols = slice(f * FF_CHUNK, (f + 1) * FF_CHUNK)
        return (pltpu.make_async_copy(wg_hbm.at[sel[0], sel[1], :, cols], wg_ref.at[:, cols], sem.at[2 * f]),
                pltpu.make_async_copy(wu_hbm.at[sel[0], sel[1], :, cols], wu_ref.at[:, cols], sem.at[2 * f + 1]))

    def down_copy():
        return pltpu.make_async_copy(wd_hbm.at[sel[0], sel[1]], wd_ref, sem.at[2 * N_FF_CHUNKS])

    def hidden(rows, f):
        cols = slice(f * FF_CHUNK, (f + 1) * FF_CHUNK)
        g = _mm(u_scr[rows, :], wg_ref[:, cols])
        up = _mm(u_scr[rows, :], wu_ref[:, cols])
        a_scr[rows, cols] = (g * _sigmoid(g) * up).astype(BF16)

    def body(first_step):
        if first_step:
            for f in range(N_FF_CHUNKS):
                for cp in chunk_copies(f):
                    cp.start()
            down_copy().start()
        if n_side:
            cond = cond_ref[...]
            om_ref[...] = _mm((cond * _sigmoid(cond)).astype(BF16), wm_ref[...]) + bm_ref[...]
        for p in range(tm // FF_HEAD_ROWS):
            rows = slice(p * FF_HEAD_ROWS, (p + 1) * FF_HEAD_ROWS)
            u_scr[rows, :] = _adaln(s_ref[rows, :], gain_ref[...], mod_ref, k).astype(BF16)
            if first_step and p == 0:
                for cp in chunk_copies(0):
                    cp.wait()
            hidden(rows, 0)
        for f in range(1, N_FF_CHUNKS):
            if first_step:
                for cp in chunk_copies(f):
                    cp.wait()
            hidden(slice(None), f)
        if first_step:
            down_copy().wait()
        gate = mod_ref[0, 3 * k + 2:3 * k + 3, :]
        for p in range(tm // FF_TAIL_ROWS):
            rows = slice(p * FF_TAIL_ROWS, (p + 1) * FF_TAIL_ROWS)
            y = s_ref[rows, :] + 0.5 * gate * _mm(a_scr[rows, :], wd_ref[...])
            if final:
                y = _rms(y) * fg_ref[...]
            o_ref[rows, :] = y

    pl.when(pl.program_id(0) == 0)(functools.partial(body, True))
    pl.when(pl.program_id(0) != 0)(functools.partial(body, False))


def _ffn_half(s, mod, gain, wg, wu, wd, sel, k, tm, final_gain=None, side_mod=None):
    n_tok, d = s.shape
    rows_per_mod = n_tok // mod.shape[0]
    assert rows_per_mod % tm == 0
    final = final_gain is not None
    hbm = pl.BlockSpec(memory_space=pl.ANY)
    in_specs = [
        pl.BlockSpec((tm, d), lambda i: (i, 0)),
        pl.BlockSpec((1, N_MOD, d), lambda i: (i * tm // rows_per_mod, 0, 0)),
        pl.BlockSpec((1, d), lambda i: (0, 0)),
        hbm, hbm, hbm,
    ]
    args = [s, mod, gain.reshape(1, d), wg, wu, wd]
    if final:
        in_specs.append(pl.BlockSpec((1, d), lambda i: (0, 0)))
        args.append(final_gain.reshape(1, d))
    out_specs = pl.BlockSpec((tm, d), lambda i: (i, 0))
    out_shape = jax.ShapeDtypeStruct((n_tok, d), F32)
    n_side = 0
    if side_mod is not None:
        cond, w_mod, b_mod, layer = side_mod
        n_mod = w_mod.shape[-1]
        n_side = n_mod // SIDE_MOD_TILE_N
        assert n_side <= n_tok // tm
        blk = lambda i: jnp.minimum(i, n_side - 1)
        in_specs += [pl.BlockSpec(cond.shape, lambda i: (0, 0)),
                     pl.BlockSpec((None, d, SIDE_MOD_TILE_N), lambda i: (layer, 0, blk(i))),
                     pl.BlockSpec((None, 1, SIDE_MOD_TILE_N), lambda i: (layer, 0, blk(i)))]
        args += [cond, w_mod, b_mod.reshape(b_mod.shape[0], 1, n_mod)]
        out_specs = [out_specs, pl.BlockSpec((cond.shape[0], SIDE_MOD_TILE_N), lambda i: (0, blk(i)))]
        out_shape = [out_shape, jax.ShapeDtypeStruct((cond.shape[0], n_mod), F32)]
    return pl.pallas_call(
        functools.partial(_ffn_kernel, k=k, final=final, sel=sel, n_side=n_side),
        grid=(n_tok // tm,),
        in_specs=in_specs,
        out_specs=out_specs,
        out_shape=out_shape,
        scratch_shapes=[pltpu.VMEM((tm, d), BF16), pltpu.VMEM((tm, D_FF), BF16),
                        pltpu.VMEM(wg.shape[2:], F32), pltpu.VMEM(wu.shape[2:], F32),
                        pltpu.VMEM(wd.shape[2:], F32),
                        pltpu.SemaphoreType.DMA((2 * N_FF_CHUNKS + 1,))],
        compiler_params=_params(("arbitrary",)),
        name="ffn_half",
    )(*args)


def _rope(z, c_ref, s1_ref, s2_ref):
    fwd = pltpu.roll(z, HEAD_PAD - ROPE_AXIS_DIM // 2, axis=1)
    bwd = pltpu.roll(z, ROPE_AXIS_DIM // 2, axis=1)
    return z * c_ref[...] + fwd * s1_ref[...] + bwd * s2_ref[...]


def _mix_in_kernel(*refs, latent):
    if latent:
        (s_ref, mod_ref, gain_ref, w_in_ref, w_kr_ref, qg_ref, wq_ref, kvg_ref, wk_ref, wv_ref, place_ref,
         c_ref, s1_ref, s2_ref, pool_ref, q_ref, kt_ref, v_ref) = refs
    else:
        (s_ref, mod_ref, gain_ref, w_in_ref, w_kr_ref, kvg_ref, wk_ref, wv_ref, place_ref,
         kt_ref, v_ref) = refs
    u = _adaln(s_ref[...], gain_ref[...], mod_ref, 1).astype(BF16)
    cuts = (POOL_DIM, POOL_DIM + Q_LORA_RANK, POOL_DIM + Q_LORA_RANK + KV_LORA_RANK)
    z = _mm(u, w_in_ref[:, :cuts[2]])
    ckv = (_rms(z[:, cuts[1]:cuts[2]]) * kvg_ref[...]).astype(BF16)
    kr = pltpu.roll(_mm(u, w_kr_ref[...]), QK_NOPE_DIM, axis=1)
    if latent:
        pool_ref[...] = z[:, :cuts[0]]
        cq = (_rms(z[:, cuts[0]:cuts[1]]) * qg_ref[...]).astype(BF16)
        q = jnp.dot(cq, wq_ref[...], preferred_element_type=F32)
        for h in range(MLA_HEADS):
            sl = slice(h * HEAD_PAD, (h + 1) * HEAD_PAD)
            q_ref[:, sl] = _rope(q[:, sl], c_ref, s1_ref, s2_ref).astype(BF16)
        kr = _rope(kr, c_ref, s1_ref, s2_ref)
    kt = lax.dot_general(wk_ref[...], ckv, _NT, preferred_element_type=F32)
    kt = kt + lax.dot_general(place_ref[...], kr.astype(BF16), _NT, preferred_element_type=F32)
    kt_ref[0] = kt.astype(BF16)
    v_ref[...] = jnp.dot(ckv, wv_ref[...], preferred_element_type=F32).astype(BF16)


def _mix_in(s, mod, gain, w_in, w_kr, kvg, wk, wv, place, tm, latent_args=None):
    n_tok, d = s.shape
    tiles_per_batch = n_tok // mod.shape[0] // tm
    latent = latent_args is not None
    const = lambda a: pl.BlockSpec(a.shape, lambda i: (0,) * a.ndim)
    tok = lambda w: pl.BlockSpec((tm, w), lambda i: (i, 0))
    in_specs = [tok(d), pl.BlockSpec((1, N_MOD, d), lambda i: (i // tiles_per_batch, 0, 0)),
                pl.BlockSpec((1, d), lambda i: (0, 0)),
                pl.BlockSpec((None,) + w_in.shape[1:], lambda i: (0, 0, 0), pipeline_mode=pl.Buffered(1)),
                const(w_kr)]
    args = [s, mod, gain.reshape(1, d), w_in, w_kr]
    if latent:
        qg, wq, tabs = latent_args
        in_specs += [const(qg), const(wq)]
        args += [qg, wq]
    in_specs += [const(kvg), const(wk), const(wv), const(place)]
    args += [kvg, wk, wv, place]
    kt_spec = pl.BlockSpec((1, QK_PAD_DIM, tm), lambda i: (i // tiles_per_batch, 0, i % tiles_per_batch))
    out_specs = [kt_spec, tok(ATTN_V_DIM)]
    out_shape = [jax.ShapeDtypeStruct((mod.shape[0], QK_PAD_DIM, n_tok // mod.shape[0]), BF16),
                 jax.ShapeDtypeStruct((n_tok, ATTN_V_DIM), BF16)]
    if latent:
        rope_spec = pl.BlockSpec((tm, HEAD_PAD), lambda i: (i % tiles_per_batch, 0))
        in_specs += [rope_spec] * 3
        args += list(tabs)
        out_specs = [tok(POOL_DIM), tok(QK_PAD_DIM)] + out_specs
        out_shape = [jax.ShapeDtypeStruct((n_tok, POOL_DIM), F32),
                     jax.ShapeDtypeStruct((n_tok, QK_PAD_DIM), BF16)] + out_shape
    return pl.pallas_call(
        functools.partial(_mix_in_kernel, latent=latent),
        grid=(n_tok // tm,),
        in_specs=in_specs,
        out_specs=out_specs,
        out_shape=out_shape,
        compiler_params=_params(("arbitrary",)),
        name="mix_in_latent" if latent else "mix_in_context",
    )(*args)


def _attn_kernel(q_ref, kth_ref, ktg_ref, vh_ref, vg_ref, o_ref):
    c = ATTN_SCALE * math.log2(math.e)
    lane_head = lax.broadcasted_iota(jnp.int32, (1, PV_WIDTH), 1) // V_HEAD_DIM
    for sub in range(q_ref.shape[1] // ATTN_SUB_Q):
        rows = slice(sub * ATTN_SUB_Q, (sub + 1) * ATTN_SUB_Q)
        for grp in range(MLA_HEADS // PV_HEADS):
            vcols = slice(grp * PV_WIDTH, (grp + 1) * PV_WIDTH)
            acc = None
            for hh in range(PV_HEADS):
                h = grp * PV_HEADS + hh
                qk = slice(h * HEAD_PAD, (h + 1) * HEAD_PAD)
                q = q_ref[0, rows, qk]
                s_h = jnp.dot(q, kth_ref[0, qk, :], preferred_element_type=F32)
                s_g = jnp.dot(q, ktg_ref[0, qk, :], preferred_element_type=F32)
                m = jnp.maximum(jnp.max(s_h, axis=-1, keepdims=True),
                                jnp.max(s_g, axis=-1, keepdims=True))
                e_h = jnp.exp2((s_h - m) * c)
                e_g = jnp.exp2((s_g - m) * c)
                denom = jnp.sum(e_h, axis=-1, keepdims=True) + jnp.sum(e_g, axis=-1, keepdims=True)
                res = jnp.dot(e_h.astype(BF16), vh_ref[0, :, vcols], preferred_element_type=F32)
                res = res + jnp.dot(e_g.astype(BF16), vg_ref[0, :, vcols], preferred_element_type=F32)
                term = jnp.where(lane_head == hh, res * (1.0 / denom), 0.0)
                acc = term if acc is None else acc + term
            o_ref[0, rows, vcols] = acc.astype(BF16)


def _attention(q, kt_h, kt_g, v_h, v_g, tq):
    b, l, _ = q.shape
    t_g = kt_g.shape[2]
    return pl.pallas_call(
        _attn_kernel,
        grid=(b, l // tq),
        in_specs=[
            pl.BlockSpec((1, tq, QK_PAD_DIM), lambda i, j: (i, j, 0)),
            pl.BlockSpec((1, QK_PAD_DIM, l), lambda i, j: (i, 0, 0)),
            pl.BlockSpec((1, QK_PAD_DIM, t_g), lambda i, j: (i, 0, 0)),
            pl.BlockSpec((1, l, ATTN_V_DIM), lambda i, j: (i, 0, 0)),
            pl.BlockSpec((1, t_g, ATTN_V_DIM), lambda i, j: (i, 0, 0)),
        ],
        out_specs=pl.BlockSpec((1, tq, ATTN_V_DIM), lambda i, j: (i, j, 0)),
        out_shape=jax.ShapeDtypeStruct((b, l, ATTN_V_DIM), BF16),
        compiler_params=_params(("arbitrary", "arbitrary")),
        name="latent_attention",
    )(q, kt_h, kt_g, v_h, v_g)


def _mix_out_kernel(h_ref, mod_ref, pool_ref, poolp_ref, pooln_ref, attn_ref, pw_ref, ps_ref, wo_ref,
                    o_ref, win_scr, y_scr, *, tm, seq):
    t0 = pl.program_id(1) * tm
    win_scr[0:HALO, :] = jnp.where(t0 > 0, poolp_ref[0], 0.0)
    win_scr[HALO:HALO + tm, :] = pool_ref[0]
    win_scr[HALO + tm:, :] = jnp.where(t0 + tm < seq, pooln_ref[0], 0.0)
    t = (t0 + lax.broadcasted_iota(jnp.int32, (tm, 1), 0)).astype(F32)
    n = tm + 2 * HALO
    for g, w in enumerate(POOL_WINDOWS):
        lanes = slice(g * POOL_GROUP_DIM, (g + 1) * POOL_GROUP_DIM)
        fwd = win_scr[:, lanes]
        span = 1
        while span < w:
            fwd = fwd + pltpu.roll(fwd, n - span, axis=0)
            span *= 2
        total = pltpu.roll(fwd, w // 2, axis=0)[HALO:HALO + tm]
        cnt = jnp.minimum(t, float(w // 2)) + jnp.minimum(float(seq - 1) - t, float(w - w // 2 - 1)) + 1.0
        p = total / cnt - win_scr[HALO:HALO + tm, lanes]
        y = _mm(p.astype(BF16), pw_ref[g])
        y_scr[:, lanes] = (y * ps_ref[:, lanes]).astype(BF16)
    out = _mm(y_scr[...], wo_ref[:POOL_DIM, :]) + _mm(attn_ref[0], wo_ref[POOL_DIM:, :])
    gate = mod_ref[0, 3 * 1 + 2:3 * 1 + 3, :]
    o_ref[0] = h_ref[0] + gate * out


def _mix_out(h, mod, pool, attn, pool_w, pool_scale, w_out, tm):
    b, l, d = h.shape
    hb = tm // HALO
    const = lambda a: pl.BlockSpec((None,) + a.shape[1:], lambda i, j: (0,) * a.ndim,
                                   pipeline_mode=pl.Buffered(1))
    return pl.pallas_call(
        functools.partial(_mix_out_kernel, tm=tm, seq=l),
        grid=(b, l // tm),
        in_specs=[
            pl.BlockSpec((1, tm, d), lambda i, j: (i, j, 0)),
            pl.BlockSpec((1, N_MOD, d), lambda i, j: (i, 0, 0)),
            pl.BlockSpec((1, tm, POOL_DIM), lambda i, j: (i, j, 0)),
            pl.BlockSpec((1, HALO, POOL_DIM), lambda i, j: (i, jnp.maximum(j * hb - 1, 0), 0)),
            pl.BlockSpec((1, HALO, POOL_DIM), lambda i, j: (i, jnp.minimum((j + 1) * hb, l // HALO - 1), 0)),
            pl.BlockSpec((1, tm, ATTN_V_DIM), lambda i, j: (i, j, 0)),
            const(pool_w), const(pool_scale), const(w_out),
        ],
        out_specs=pl.BlockSpec((1, tm, d), lambda i, j: (i, j, 0)),
        out_shape=jax.ShapeDtypeStruct((b, l, d), F32),
        scratch_shapes=[pltpu.VMEM((tm + 2 * HALO, POOL_DIM), F32), pltpu.VMEM((tm, POOL_DIM), BF16)],
        compiler_params=_params(("arbitrary", "arbitrary")),
        name="mix_out",
    )(h, mod, pool, pool, pool, attn, pool_w, pool_scale, w_out)


def _conv_kernel(h_ref, hp_ref, hn_ref, mod_ref, gain_ref, w_in_ref, cw_ref, w_out_ref, o_ref,
                 u_scr, z_scr, *, tm, seq):
    t0 = pl.program_id(1) * tm
    gain = gain_ref[...]
    u_scr[0:HALO, :] = _adaln(hp_ref[0], gain, mod_ref, 1).astype(BF16)
    u_scr[HALO:HALO + tm, :] = _adaln(h_ref[0], gain, mod_ref, 1).astype(BF16)
    u_scr[HALO + tm:, :] = _adaln(hn_ref[0], gain, mod_ref, 1).astype(BF16)
    d = h_ref.shape[-1]
    cv = _mm(u_scr[...], w_in_ref[:, d:])
    t = t0 - HALO + lax.broadcasted_iota(jnp.int32, (tm + 2 * HALO, 1), 0)
    inside = jnp.logical_and(t >= 0, t < seq)
    z_scr[...] = jnp.where(inside, cv[:, :d] * cv[:, d:], 0.0)
    y = (cw_ref[0:1, :] * z_scr[HALO - 1:HALO - 1 + tm, :]
         + cw_ref[1:2, :] * z_scr[HALO:HALO + tm, :]
         + cw_ref[2:3, :] * z_scr[HALO + 1:HALO + 1 + tm, :])
    bg = _mm(u_scr[HALO:HALO + tm, :], w_in_ref[:, :d])
    out = _mm((bg * y).astype(BF16), w_out_ref[...])
    gate = mod_ref[0, 3 * 1 + 2:3 * 1 + 3, :]
    o_ref[0] = h_ref[0] + gate * out


def _conv_mixer(h, mod, gain, w_in, conv_w, w_out, tm):
    b, l, d = h.shape
    hb = tm // HALO
    const = lambda a: pl.BlockSpec((None,) + a.shape[1:], lambda i, j: (0,) * a.ndim,
                                   pipeline_mode=pl.Buffered(1))
    return pl.pallas_call(
        functools.partial(_conv_kernel, tm=tm, seq=l),
        grid=(b, l // tm),
        in_specs=[
            pl.BlockSpec((1, tm, d), lambda i, j: (i, j, 0)),
            pl.BlockSpec((1, HALO, d), lambda i, j: (i, jnp.maximum(j * hb - 1, 0), 0)),
            pl.BlockSpec((1, HALO, d), lambda i, j: (i, jnp.minimum((j + 1) * hb, l // HALO - 1), 0)),
            pl.BlockSpec((1, N_MOD, d), lambda i, j: (i, 0, 0)),
            pl.BlockSpec((1, d), lambda i, j: (0, 0)),
            const(w_in), const(conv_w), const(w_out),
        ],
        out_specs=pl.BlockSpec((1, tm, d), lambda i, j: (i, j, 0)),
        out_shape=jax.ShapeDtypeStruct((b, l, d), F32),
        scratch_shapes=[pltpu.VMEM((tm + 2 * HALO, d), BF16), pltpu.VMEM((tm + 2 * HALO, d), F32)],
        compiler_params=_params(("arbitrary", "arbitrary")),
        name="conv_mixer",
    )(h, h, h, mod, gain.reshape(1, d), w_in, conv_w, w_out)


def _head_pad(w, per_head, start, width):
    r = w.shape[0]
    w = w.reshape(r, MLA_HEADS, per_head)[:, :, start:start + width]
    return jnp.pad(w, ((0, 0), (0, 0), (0, HEAD_PAD - width))).reshape(r, QK_PAD_DIM)


def _rope_tables(length):
    pos = np.arange(length)
    row = (pos // GRID_W).astype(np.float32)
    col = (pos % GRID_W).astype(np.float32)
    half = ROPE_AXIS_DIM // 2
    freqs = np.power(np.float32(ROPE_THETA),
                     -np.arange(0, ROPE_AXIS_DIM, 2, dtype=np.float32) / np.float32(ROPE_AXIS_DIM))
    lane = np.arange(HEAD_PAD)
    o = lane - QK_NOPE_DIM
    rotary = np.logical_and(o >= 0, o < QK_ROPE_DIM)
    o = np.clip(o, 0, QK_ROPE_DIM - 1)
    ang = np.where((o // ROPE_AXIS_DIM == 0)[None, :], row[:, None], col[:, None]) * freqs[o % half][None, :]
    ang = ang.astype(np.float32)
    first = (o % ROPE_AXIS_DIM) < half
    cos = np.where(rotary[None, :], np.cos(ang), 1.0).astype(np.float32)
    sin = np.where(rotary[None, :], np.sin(ang), 0.0).astype(np.float32)
    return (jnp.asarray(cos), jnp.asarray(np.where(first[None, :], -sin, 0.0).astype(np.float32)),
            jnp.asarray(np.where(first[None, :], 0.0, sin).astype(np.float32)))


def kernel(x, c, ctx, c_ctx, norm_g, w_mod, b_mod, ffn_w_gate, ffn_w_up, ffn_w_down, ab_w_in, pool_w,
           pool_scale, q_norm_g, w_uq, kv_norm_g, w_ukv, ab_w_out, conv_w_in, conv_w, conv_w_out,
           final_norm_g):
    b, l, d = x.shape
    t_g = ctx.shape[1]
    tm_h, tm_g, tq, tm_mix = FFN_TILE_ROWS, CTX_MIX_TILE_ROWS, ATTN_TILE_ROWS, MIX_TILE_ROWS
    assert w_mod.shape[0] == 2 and ab_w_in.shape[0] == 1 and conv_w_in.shape[0] == 1
    assert l % tm_h == 0 and l % tq == 0 and l % tm_mix == 0 and t_g == tm_g and l % GRID_W == 0

    cond = jnp.zeros((MOD_ROWS, d), F32).at[:b].set(c).at[b].set(c_ctx)
    m = _modulation(cond, w_mod, b_mod, 1)
    mod_h = [m[0, :b].reshape(b, N_MOD, d), None]
    mod_g = jnp.broadcast_to(m[0, b].reshape(1, N_MOD, d), (b, N_MOD, d))

    ffn_w = (ffn_w_gate, ffn_w_up, ffn_w_down)

    h = x.reshape(b * l, d)
    g = ctx.reshape(b * t_g, d)

    h, m1 = _ffn_half(h, mod_h[0], norm_g[0, 0], *ffn_w, (0, 0), k=0, tm=tm_h,
                      side_mod=(cond, w_mod, b_mod, 1))
    mod_h[1] = m1[:b].reshape(b, N_MOD, d)
    g = _ffn_half(g, m[0, b].reshape(1, N_MOD, d), norm_g[0, 0], *ffn_w, (0, 0), k=0, tm=tm_h)

    kr_start = POOL_DIM + Q_LORA_RANK + KV_LORA_RANK
    w_kr = jnp.pad(ab_w_in[0, :, kr_start:], ((0, 0), (0, HEAD_PAD - QK_ROPE_DIM)))
    wq = _head_pad(w_uq[0], QK_HEAD_DIM, 0, QK_HEAD_DIM).astype(BF16)
    wk = _head_pad(w_ukv[0], QK_NOPE_DIM + V_HEAD_DIM, 0, QK_NOPE_DIM).T.astype(BF16)
    wv = w_ukv[0].reshape(KV_LORA_RANK, MLA_HEADS, QK_NOPE_DIM + V_HEAD_DIM)[:, :, QK_NOPE_DIM:]
    wv = wv.reshape(KV_LORA_RANK, ATTN_V_DIM).astype(BF16)
    lane = jnp.arange(HEAD_PAD)
    rotary = jnp.logical_and(lane >= QK_NOPE_DIM, lane < QK_HEAD_DIM)
    place = jnp.tile(jnp.where(rotary[:, None], jnp.eye(HEAD_PAD, dtype=F32), 0.0), (MLA_HEADS, 1)).astype(BF16)
    qg = q_norm_g[0].reshape(1, Q_LORA_RANK)
    kvg = kv_norm_g[0].reshape(1, KV_LORA_RANK)
    tabs = _rope_tables(l)

    pool, q, k_h, v_h = _mix_in(h, mod_h[0], norm_g[0, 1], ab_w_in, w_kr, kvg, wk, wv, place, tm_mix,
                                latent_args=(qg, wq, tabs))
    k_g, v_g = _mix_in(g, mod_g, norm_g[0, 1], ab_w_in, w_kr, kvg, wk, wv, place, tm_g)
    attn = _attention(q.reshape(b, l, -1), k_h, k_g, v_h.reshape(b, l, -1), v_g.reshape(b, t_g, -1), tq)
    h = _mix_out(h.reshape(b, l, d), mod_h[0], pool.reshape(b, l, -1), attn, pool_w,
                 pool_scale.reshape(-1, 1, POOL_DIM), ab_w_out, tm_mix)
    h = _ffn_half(h.reshape(b * l, d), mod_h[0], norm_g[0, 2], *ffn_w, (0, 1), k=2, tm=tm_h)

    h = _ffn_half(h, mod_h[1], norm_g[1, 0], *ffn_w, (1, 0), k=0, tm=tm_h)
    h = _conv_mixer(h.reshape(b, l, d), mod_h[1], norm_g[1, 1], conv_w_in, conv_w, conv_w_out, tm_mix)
    h = _ffn_half(h.reshape(b * l, d), mod_h[1], norm_g[1, 2], *ffn_w, (1, 1), k=2, tm=tm_h,
                  final_gain=final_norm_g)
    return h.reshape(b, l, d)
```

```python
import functools
import math

import jax
import jax.numpy as jnp
import numpy as np
from jax import lax
from jax.experimental import pallas as pl
from jax.experimental.pallas import tpu as pltpu

D_MODEL = 1024
GRID_W = 64
CTX_LEN = 256
RMS_EPS = 1e-6
N_MOD = 9
D_FF = 2816
POOL_WINDOWS = (2, 4, 8, 16)
POOL_DIM = D_MODEL // 2
POOL_GROUP_DIM = POOL_DIM // len(POOL_WINDOWS)
MLA_HEADS = D_MODEL // 128
QK_NOPE_DIM = 64
QK_ROPE_DIM = 32
QK_HEAD_DIM = QK_NOPE_DIM + QK_ROPE_DIM
V_HEAD_DIM = 64
Q_LORA_RANK = 768
KV_LORA_RANK = 256
ROPE_AXIS_DIM = QK_ROPE_DIM // 2
ROPE_THETA = 10000.0
ATTN_SCALE = 1.0 / math.sqrt(QK_HEAD_DIM)
ATTN_V_DIM = MLA_HEADS * V_HEAD_DIM

LANES = 128
SUBLANES = 8
HEAD_PAD = LANES
QK_PAD_DIM = MLA_HEADS * HEAD_PAD
MOD_ROWS = 16
MOD_TILE_N = 1536
SIDE_MOD_TILE_N = 512
FF_CHUNK = 256
N_FF_CHUNKS = D_FF // FF_CHUNK
FF_HEAD_ROWS = 256
FF_TAIL_ROWS = 256
HALO = SUBLANES
PV_WIDTH = 256
PV_HEADS = PV_WIDTH // V_HEAD_DIM
ATTN_SUB_Q = 256
V7X_VMEM_BYTES = 64 * 1024 * 1024
VMEM_LIMIT = V7X_VMEM_BYTES * 7 // 8
FFN_TILE_ROWS = 512
ATTN_TILE_ROWS = 512
MIX_TILE_ROWS = 1024
CTX_MIX_TILE_ROWS = CTX_LEN

BF16 = jnp.bfloat16
F32 = jnp.float32


def _sigmoid(x):
    return 1.0 / (1.0 + jnp.exp(-x))


def _rms(x):
    return x * lax.rsqrt(jnp.mean(x * x, axis=-1, keepdims=True) + RMS_EPS)


def _adaln(x, gain, mod_ref, k):
    shift = mod_ref[0, 3 * k:3 * k + 1, :]
    scale = mod_ref[0, 3 * k + 1:3 * k + 2, :]
    return _rms(x) * gain * (1.0 + scale) + shift


_NT = (((1,), (1,)), ((), ()))


def _mm(a, w):
    return lax.dot_general(a, w, (((1,), (0,)), ((), ())), preferred_element_type=F32)


def _params(semantics):
    return pltpu.CompilerParams(dimension_semantics=semantics, vmem_limit_bytes=VMEM_LIMIT)


def _mod_kernel(cond_ref, w_ref, b_ref, o_ref):
    cond = cond_ref[...]
    o_ref[...] = _mm((cond * _sigmoid(cond)).astype(BF16), w_ref[...]) + b_ref[...]


def _modulation(cond, w_mod, b_mod, n_cols):
    depth, d, n = w_mod.shape
    return pl.pallas_call(
        _mod_kernel,
        grid=(n_cols // MOD_TILE_N,),
        in_specs=[
            pl.BlockSpec((MOD_ROWS, d), lambda j: (0, 0)),
            pl.BlockSpec((None, d, MOD_TILE_N), lambda j: (0, 0, j)),
            pl.BlockSpec((None, 1, MOD_TILE_N), lambda j: (0, 0, j)),
        ],
        out_specs=pl.BlockSpec((MOD_ROWS, MOD_TILE_N), lambda j: (0, j)),
        out_shape=jax.ShapeDtypeStruct((MOD_ROWS, n_cols), F32),
        compiler_params=_params(("arbitrary",)),
        name="modulation",
    )(cond, w_mod, b_mod.reshape(depth, 1, n))


def _ffn_kernel(*refs, k, final, sel, n_side):
    refs = list(refs)
    s_ref, mod_ref, gain_ref, wg_hbm, wu_hbm, wd_hbm = refs[:6]
    u_scr, a_scr, wg_ref, wu_ref, wd_ref, sem = refs[-6:]
    extra = refs[6:-6]
    fg_ref = extra.pop(0) if final else None
    if n_side:
        cond_ref, wm_ref, bm_ref, o_ref, om_ref = extra
    else:
        (o_ref,) = extra
    tm = s_ref.shape[0]

    def chunk_copies(f):
        cols = slice(f * FF_CHUNK, (f + 1) * FF_CHUNK)
        return (pltpu.make_async_copy(wg_hbm.at[sel[0], sel[1], :, cols], wg_ref.at[:, cols], sem.at[2 * f]),
                pltpu.make_async_copy(wu_hbm.at[sel[0], sel[1], :, cols], wu_ref.at[:, cols], sem.at[2 * f + 1]))

    def down_copy():
        return pltpu.make_async_copy(wd_hbm.at[sel[0], sel[1]], wd_ref, sem.at[2 * N_FF_CHUNKS])

    def hidden(rows, f):
        cols = slice(f * FF_CHUNK, (f + 1) * FF_CHUNK)
        g = _mm(u_scr[rows, :], wg_ref[:, cols])
        up = _mm(u_scr[rows, :], wu_ref[:, cols])
        a_scr[rows, cols] = (g * _sigmoid(g) * up).astype(BF16)

    def body(first_step):
        if first_step:
            for f in range(N_FF_CHUNKS):
                for cp in chunk_copies(f):
                    cp.start()
            down_copy().start()
        if n_side:
            cond = cond_ref[...]
            om_ref[...] = _mm((cond * _sigmoid(cond)).astype(BF16), wm_ref[...]) + bm_ref[...]
        for p in range(tm // FF_HEAD_ROWS):
            rows = slice(p * FF_HEAD_ROWS, (p + 1) * FF_HEAD_ROWS)
            u_scr[rows, :] = _adaln(s_ref[rows, :], gain_ref[...], mod_ref, k).astype(BF16)
            if first_step and p == 0:
                for cp in chunk_copies(0):
                    cp.wait()
            hidden(rows, 0)
        for f in range(1, N_FF_CHUNKS):
            if first_step:
                for cp in chunk_copies(f):
                    cp.wait()
            hidden(slice(None), f)
        if first_step:
            down_copy().wait()
        gate = mod_ref[0, 3 * k + 2:3 * k + 3, :]
        for p in range(tm // FF_TAIL_ROWS):
            rows = slice(p * FF_TAIL_ROWS, (p + 1) * FF_TAIL_ROWS)
            y = s_ref[rows, :] + 0.5 * gate * _mm(a_scr[rows, :], wd_ref[...])
            if final:
                y = _rms(y) * fg_ref[...]
            o_ref[rows, :] = y

    pl.when(pl.program_id(0) == 0)(functools.partial(body, True))
    pl.when(pl.program_id(0) != 0)(functools.partial(body, False))


def _ffn_half(s, mod, gain, wg, wu, wd, sel, k, tm, final_gain=None, side_mod=None):
    n_tok, d = s.shape
    rows_per_mod = n_tok // mod.shape[0]
    assert rows_per_mod % tm == 0
    final = final_gain is not None
    hbm = pl.BlockSpec(memory_space=pl.ANY)
    in_specs = [
        pl.BlockSpec((tm, d), lambda i: (i, 0)),
        pl.BlockSpec((1, mod.shape[1], d), lambda i: (i * tm // rows_per_mod, 0, 0)),
        pl.BlockSpec((1, d), lambda i: (0, 0)),
        hbm, hbm, hbm,
    ]
    args = [s, mod, gain.reshape(1, d), wg, wu, wd]
    if final:
        in_specs.append(pl.BlockSpec((1, d), lambda i: (0, 0)))
        args.append(final_gain.reshape(1, d))
    out_specs = pl.BlockSpec((tm, d), lambda i: (i, 0))
    out_shape = jax.ShapeDtypeStruct((n_tok, d), F32)
    n_side = 0
    if side_mod is not None:
        cond, w_mod, b_mod, first_col = side_mod
        n_mod = w_mod.shape[-1]
        c0 = first_col // SIDE_MOD_TILE_N
        n_first = n_mod // SIDE_MOD_TILE_N - c0
        n_side = n_first + n_mod // SIDE_MOD_TILE_N
        assert n_side <= n_tok // tm and first_col % SIDE_MOD_TILE_N == 0

        def src(i):
            j = jnp.minimum(i, n_side - 1)
            return jnp.where(j < n_first, 0, 1), 0, jnp.where(j < n_first, c0 + j, j - n_first)

        in_specs += [pl.BlockSpec(cond.shape, lambda i: (0, 0)),
                     pl.BlockSpec((None, d, SIDE_MOD_TILE_N), src),
                     pl.BlockSpec((None, 1, SIDE_MOD_TILE_N), src)]
        args += [cond, w_mod, b_mod.reshape(b_mod.shape[0], 1, n_mod)]
        out_specs = [out_specs, pl.BlockSpec((cond.shape[0], SIDE_MOD_TILE_N),
                                             lambda i: (0, jnp.minimum(i, n_side - 1)))]
        out_shape = [out_shape, jax.ShapeDtypeStruct((cond.shape[0], n_side * SIDE_MOD_TILE_N), F32)]
    return pl.pallas_call(
        functools.partial(_ffn_kernel, k=k, final=final, sel=sel, n_side=n_side),
        grid=(n_tok // tm,),
        in_specs=in_specs,
        out_specs=out_specs,
        out_shape=out_shape,
        scratch_shapes=[pltpu.VMEM((tm, d), BF16), pltpu.VMEM((tm, D_FF), BF16),
                        pltpu.VMEM(wg.shape[2:], F32), pltpu.VMEM(wu.shape[2:], F32),
                        pltpu.VMEM(wd.shape[2:], F32),
                        pltpu.SemaphoreType.DMA((2 * N_FF_CHUNKS + 1,))],
        compiler_params=_params(("arbitrary",)),
        name="ffn_half",
    )(*args)


def _rope(z, c_ref, s1_ref, s2_ref):
    fwd = pltpu.roll(z, HEAD_PAD - ROPE_AXIS_DIM // 2, axis=1)
    bwd = pltpu.roll(z, ROPE_AXIS_DIM // 2, axis=1)
    return z * c_ref[...] + fwd * s1_ref[...] + bwd * s2_ref[...]


def _mix_in_kernel(*refs, latent):
    if latent:
        (s_ref, mod_ref, gain_ref, w_in_ref, w_kr_ref, qg_ref, wq_ref, kvg_ref, wk_ref, wv_ref, place_ref,
         c_ref, s1_ref, s2_ref, pool_ref, q_ref, kt_ref, v_ref) = refs
    else:
        (s_ref, mod_ref, gain_ref, w_in_ref, w_kr_ref, kvg_ref, wk_ref, wv_ref, place_ref,
         kt_ref, v_ref) = refs
    u = _adaln(s_ref[...], gain_ref[...], mod_ref, 1).astype(BF16)
    cuts = (POOL_DIM, POOL_DIM + Q_LORA_RANK, POOL_DIM + Q_LORA_RANK + KV_LORA_RANK)
    z = _mm(u, w_in_ref[...])
    ckv = (_rms(z[:, cuts[1]:cuts[2]]) * kvg_ref[...]).astype(BF16)
    kr = pltpu.roll(_mm(u, w_kr_ref[...]), QK_NOPE_DIM, axis=1)
    if latent:
        pool_ref[...] = z[:, :cuts[0]]
        cq = (_rms(z[:, cuts[0]:cuts[1]]) * qg_ref[...]).astype(BF16)
        q = jnp.dot(cq, wq_ref[...], preferred_element_type=F32)
        for h in range(MLA_HEADS):
            sl = slice(h * HEAD_PAD, (h + 1) * HEAD_PAD)
            q_ref[:, sl] = _rope(q[:, sl], c_ref, s1_ref, s2_ref).astype(BF16)
        kr = _rope(kr, c_ref, s1_ref, s2_ref)
    kt = lax.dot_general(wk_ref[...], ckv, _NT, preferred_element_type=F32)
    kt = kt + lax.dot_general(place_ref[...], kr.astype(BF16), _NT, preferred_element_type=F32)
    kt_ref[0] = kt.astype(BF16)
    v_ref[...] = jnp.dot(ckv, wv_ref[...], preferred_element_type=F32).astype(BF16)


def _mix_in(s, mod, gain, w_in, w_kr, kvg, wk, wv, place, tm, latent_args=None):
    n_tok, d = s.shape
    tiles_per_batch = n_tok // mod.shape[0] // tm
    latent = latent_args is not None
    const = lambda a: pl.BlockSpec(a.shape, lambda i: (0,) * a.ndim)
    tok = lambda w: pl.BlockSpec((tm, w), lambda i: (i, 0))
    in_specs = [tok(d), pl.BlockSpec((1, N_MOD, d), lambda i: (i // tiles_per_batch, 0, 0)),
                pl.BlockSpec((1, d), lambda i: (0, 0)),
                pl.BlockSpec(w_in.shape, lambda i: (0, 0), pipeline_mode=pl.Buffered(1)),
                const(w_kr)]
    args = [s, mod, gain.reshape(1, d), w_in, w_kr]
    if latent:
        qg, wq, tabs = latent_args
        in_specs += [const(qg), const(wq)]
        args += [qg, wq]
    in_specs += [const(kvg), const(wk), const(wv), const(place)]
    args += [kvg, wk, wv, place]
    kt_spec = pl.BlockSpec((1, QK_PAD_DIM, tm), lambda i: (i // tiles_per_batch, 0, i % tiles_per_batch))
    out_specs = [kt_spec, tok(ATTN_V_DIM)]
    out_shape = [jax.ShapeDtypeStruct((mod.shape[0], QK_PAD_DIM, n_tok // mod.shape[0]), BF16),
                 jax.ShapeDtypeStruct((n_tok, ATTN_V_DIM), BF16)]
    if latent:
        rope_spec = pl.BlockSpec((tm, HEAD_PAD), lambda i: (i % tiles_per_batch, 0))
        in_specs += [rope_spec] * 3
        args += list(tabs)
        out_specs = [tok(POOL_DIM), tok(QK_PAD_DIM)] + out_specs
        out_shape = [jax.ShapeDtypeStruct((n_tok, POOL_DIM), F32),
                     jax.ShapeDtypeStruct((n_tok, QK_PAD_DIM), BF16)] + out_shape
    return pl.pallas_call(
        functools.partial(_mix_in_kernel, latent=latent),
        grid=(n_tok // tm,),
        in_specs=in_specs,
        out_specs=out_specs,
        out_shape=out_shape,
        compiler_params=_params(("arbitrary",)),
        name="mix_in_latent" if latent else "mix_in_context",
    )(*args)


def _attn_kernel(q_ref, kth_ref, ktg_ref, vh_ref, vg_ref, o_ref):
    c = ATTN_SCALE * math.log2(math.e)
    lane_head = lax.broadcasted_iota(jnp.int32, (1, PV_WIDTH), 1) // V_HEAD_DIM
    for sub in range(q_ref.shape[1] // ATTN_SUB_Q):
        rows = slice(sub * ATTN_SUB_Q, (sub + 1) * ATTN_SUB_Q)
        for grp in range(MLA_HEADS // PV_HEADS):
            vcols = slice(grp * PV_WIDTH, (grp + 1) * PV_WIDTH)
            acc = None
            for hh in range(PV_HEADS):
                h = grp * PV_HEADS + hh
                qk = slice(h * HEAD_PAD, (h + 1) * HEAD_PAD)
                q = q_ref[0, rows, qk]
                s_h = jnp.dot(q, kth_ref[0, qk, :], preferred_element_type=F32)
                s_g = jnp.dot(q, ktg_ref[0, qk, :], preferred_element_type=F32)
                m = jnp.maximum(jnp.max(s_h, axis=-1, keepdims=True),
                                jnp.max(s_g, axis=-1, keepdims=True))
                e_h = jnp.exp2((s_h - m) * c)
                e_g = jnp.exp2((s_g - m) * c)
                denom = jnp.sum(e_h, axis=-1, keepdims=True) + jnp.sum(e_g, axis=-1, keepdims=True)
                res = jnp.dot(e_h.astype(BF16), vh_ref[0, :, vcols], preferred_element_type=F32)
                res = res + jnp.dot(e_g.astype(BF16), vg_ref[0, :, vcols], preferred_element_type=F32)
                term = jnp.where(lane_head == hh, res * (1.0 / denom), 0.0)
                acc = term if acc is None else acc + term
            o_ref[0, rows, vcols] = acc.astype(BF16)


def _attention(q, kt_h, kt_g, v_h, v_g, tq):
    b, l, _ = q.shape
    t_g = kt_g.shape[2]
    return pl.pallas_call(
        _attn_kernel,
        grid=(b, l // tq),
        in_specs=[
            pl.BlockSpec((1, tq, QK_PAD_DIM), lambda i, j: (i, j, 0)),
            pl.BlockSpec((1, QK_PAD_DIM, l), lambda i, j: (i, 0, 0)),
            pl.BlockSpec((1, QK_PAD_DIM, t_g), lambda i, j: (i, 0, 0)),
            pl.BlockSpec((1, l, ATTN_V_DIM), lambda i, j: (i, 0, 0)),
            pl.BlockSpec((1, t_g, ATTN_V_DIM), lambda i, j: (i, 0, 0)),
        ],
        out_specs=pl.BlockSpec((1, tq, ATTN_V_DIM), lambda i, j: (i, j, 0)),
        out_shape=jax.ShapeDtypeStruct((b, l, ATTN_V_DIM), BF16),
        compiler_params=_params(("arbitrary", "arbitrary")),
        name="latent_attention",
    )(q, kt_h, kt_g, v_h, v_g)


def _mix_out_kernel(h_ref, mod_ref, pool_ref, poolp_ref, pooln_ref, attn_ref, pw_ref, ps_ref, wo_ref,
                    o_ref, win_scr, y_scr, *, tm, seq):
    t0 = pl.program_id(1) * tm
    win_scr[0:HALO, :] = jnp.where(t0 > 0, poolp_ref[0], 0.0)
    win_scr[HALO:HALO + tm, :] = pool_ref[0]
    win_scr[HALO + tm:, :] = jnp.where(t0 + tm < seq, pooln_ref[0], 0.0)
    t = (t0 + lax.broadcasted_iota(jnp.int32, (tm, 1), 0)).astype(F32)
    n = tm + 2 * HALO
    for g, w in enumerate(POOL_WINDOWS):
        lanes = slice(g * POOL_GROUP_DIM, (g + 1) * POOL_GROUP_DIM)
        fwd = win_scr[:, lanes]
        span = 1
        while span < w:
            fwd = fwd + pltpu.roll(fwd, n - span, axis=0)
            span *= 2
        total = pltpu.roll(fwd, w // 2, axis=0)[HALO:HALO + tm]
        cnt = jnp.minimum(t, float(w // 2)) + jnp.minimum(float(seq - 1) - t, float(w - w // 2 - 1)) + 1.0
        p = total / cnt - win_scr[HALO:HALO + tm, lanes]
        y = _mm(p.astype(BF16), pw_ref[g])
        y_scr[:, lanes] = (y * ps_ref[:, lanes]).astype(BF16)
    out = _mm(y_scr[...], wo_ref[:POOL_DIM, :]) + _mm(attn_ref[0], wo_ref[POOL_DIM:, :])
    gate = mod_ref[0, 3 * 1 + 2:3 * 1 + 3, :]
    o_ref[0] = h_ref[0] + gate * out


def _mix_out(h, mod, pool, attn, pool_w, pool_scale, w_out, tm):
    b, l, d = h.shape
    hb = tm // HALO
    const = lambda a: pl.BlockSpec((None,) + a.shape[1:], lambda i, j: (0,) * a.ndim,
                                   pipeline_mode=pl.Buffered(1))
    return pl.pallas_call(
        functools.partial(_mix_out_kernel, tm=tm, seq=l),
        grid=(b, l // tm),
        in_specs=[
            pl.BlockSpec((1, tm, d), lambda i, j: (i, j, 0)),
            pl.BlockSpec((1, N_MOD, d), lambda i, j: (i, 0, 0)),
            pl.BlockSpec((1, tm, POOL_DIM), lambda i, j: (i, j, 0)),
            pl.BlockSpec((1, HALO, POOL_DIM), lambda i, j: (i, jnp.maximum(j * hb - 1, 0), 0)),
            pl.BlockSpec((1, HALO, POOL_DIM), lambda i, j: (i, jnp.minimum((j + 1) * hb, l // HALO - 1), 0)),
            pl.BlockSpec((1, tm, ATTN_V_DIM), lambda i, j: (i, j, 0)),
            const(pool_w), const(pool_scale), const(w_out),
        ],
        out_specs=pl.BlockSpec((1, tm, d), lambda i, j: (i, j, 0)),
        out_shape=jax.ShapeDtypeStruct((b, l, d), F32),
        scratch_shapes=[pltpu.VMEM((tm + 2 * HALO, POOL_DIM), F32), pltpu.VMEM((tm, POOL_DIM), BF16)],
        compiler_params=_params(("arbitrary", "arbitrary")),
        name="mix_out",
    )(h, mod, pool, pool, pool, attn, pool_w, pool_scale, w_out)


def _conv_kernel(h_ref, hp_ref, hn_ref, mod_ref, gain_ref, w_in_ref, cw_ref, w_out_ref, o_ref,
                 u_scr, z_scr, *, tm, seq):
    t0 = pl.program_id(1) * tm
    gain = gain_ref[...]
    u_scr[0:HALO, :] = _adaln(hp_ref[0], gain, mod_ref, 1).astype(BF16)
    u_scr[HALO:HALO + tm, :] = _adaln(h_ref[0], gain, mod_ref, 1).astype(BF16)
    u_scr[HALO + tm:, :] = _adaln(hn_ref[0], gain, mod_ref, 1).astype(BF16)
    d = h_ref.shape[-1]
    cv = _mm(u_scr[...], w_in_ref[:, d:])
    t = t0 - HALO + lax.broadcasted_iota(jnp.int32, (tm + 2 * HALO, 1), 0)
    inside = jnp.logical_and(t >= 0, t < seq)
    z_scr[...] = jnp.where(inside, cv[:, :d] * cv[:, d:], 0.0)
    y = (cw_ref[0:1, :] * z_scr[HALO - 1:HALO - 1 + tm, :]
         + cw_ref[1:2, :] * z_scr[HALO:HALO + tm, :]
         + cw_ref[2:3, :] * z_scr[HALO + 1:HALO + 1 + tm, :])
    bg = _mm(u_scr[HALO:HALO + tm, :], w_in_ref[:, :d])
    out = _mm((bg * y).astype(BF16), w_out_ref[...])
    gate = mod_ref[0, 3 * 1 + 2:3 * 1 + 3, :]
    o_ref[0] = h_ref[0] + gate * out


def _conv_mixer(h, mod, gain, w_in, conv_w, w_out, tm):
    b, l, d = h.shape
    hb = tm // HALO
    const = lambda a: pl.BlockSpec((None,) + a.shape[1:], lambda i, j: (0,) * a.ndim,
                                   pipeline_mode=pl.Buffered(1))
    return pl.pallas_call(
        functools.partial(_conv_kernel, tm=tm, seq=l),
        grid=(b, l // tm),
        in_specs=[
            pl.BlockSpec((1, tm, d), lambda i, j: (i, j, 0)),
            pl.BlockSpec((1, HALO, d), lambda i, j: (i, jnp.maximum(j * hb - 1, 0), 0)),
            pl.BlockSpec((1, HALO, d), lambda i, j: (i, jnp.minimum((j + 1) * hb, l // HALO - 1), 0)),
            pl.BlockSpec((1, N_MOD, d), lambda i, j: (i, 0, 0)),
            pl.BlockSpec((1, d), lambda i, j: (0, 0)),
            const(w_in), const(conv_w), const(w_out),
        ],
        out_specs=pl.BlockSpec((1, tm, d), lambda i, j: (i, j, 0)),
        out_shape=jax.ShapeDtypeStruct((b, l, d), F32),
        scratch_shapes=[pltpu.VMEM((tm + 2 * HALO, d), BF16), pltpu.VMEM((tm + 2 * HALO, d), F32)],
        compiler_params=_params(("arbitrary", "arbitrary")),
        name="conv_mixer",
    )(h, h, h, mod, gain.reshape(1, d), w_in, conv_w, w_out)


def _head_pad(w, per_head, start, width):
    r = w.shape[0]
    w = w.reshape(r, MLA_HEADS, per_head)[:, :, start:start + width]
    return jnp.pad(w, ((0, 0), (0, 0), (0, HEAD_PAD - width))).reshape(r, QK_PAD_DIM)


def _rope_tables(length):
    pos = np.arange(length)
    row = (pos // GRID_W).astype(np.float32)
    col = (pos % GRID_W).astype(np.float32)
    half = ROPE_AXIS_DIM // 2
    freqs = np.power(np.float32(ROPE_THETA),
                     -np.arange(0, ROPE_AXIS_DIM, 2, dtype=np.float32) / np.float32(ROPE_AXIS_DIM))
    lane = np.arange(HEAD_PAD)
    o = lane - QK_NOPE_DIM
    rotary = np.logical_and(o >= 0, o < QK_ROPE_DIM)
    o = np.clip(o, 0, QK_ROPE_DIM - 1)
    ang = np.where((o // ROPE_AXIS_DIM == 0)[None, :], row[:, None], col[:, None]) * freqs[o % half][None, :]
    ang = ang.astype(np.float32)
    first = (o % ROPE_AXIS_DIM) < half
    cos = np.where(rotary[None, :], np.cos(ang), 1.0).astype(np.float32)
    sin = np.where(rotary[None, :], np.sin(ang), 0.0).astype(np.float32)
    return (jnp.asarray(cos), jnp.asarray(np.where(first[None, :], -sin, 0.0).astype(np.float32)),
            jnp.asarray(np.where(first[None, :], 0.0, sin).astype(np.float32)))


def kernel(x, c, ctx, c_ctx, norm_g, w_mod, b_mod, ffn_w_gate, ffn_w_up, ffn_w_down, ab_w_in, pool_w,
           pool_scale, q_norm_g, w_uq, kv_norm_g, w_ukv, ab_w_out, conv_w_in, conv_w, conv_w_out,
           final_norm_g):
    b, l, d = x.shape
    t_g = ctx.shape[1]
    tm_h, tm_g, tq, tm_mix = FFN_TILE_ROWS, CTX_MIX_TILE_ROWS, ATTN_TILE_ROWS, MIX_TILE_ROWS
    assert w_mod.shape[0] == 2 and ab_w_in.shape[0] == 1 and conv_w_in.shape[0] == 1
    assert l % tm_h == 0 and l % tq == 0 and l % tm_mix == 0 and t_g == tm_g and l % GRID_W == 0

    cond = jnp.zeros((MOD_ROWS, d), F32).at[:b].set(c).at[b].set(c_ctx)
    n_first = 3 * d
    m_first = _modulation(cond, w_mod, b_mod, n_first).reshape(MOD_ROWS, 3, d)

    ffn_w = (ffn_w_gate, ffn_w_up, ffn_w_down)

    h = x.reshape(b * l, d)
    g = ctx.reshape(b * t_g, d)

    h, m_rest = _ffn_half(h, m_first[:b], norm_g[0, 0], *ffn_w, (0, 0), k=0, tm=tm_h,
                          side_mod=(cond, w_mod, b_mod, n_first))
    g = _ffn_half(g, m_first[b:b + 1], norm_g[0, 0], *ffn_w, (0, 0), k=0, tm=tm_h)
    m_rest = m_rest.reshape(MOD_ROWS, 2 * N_MOD - 3, d)
    m0 = jnp.concatenate([m_first, m_rest[:, :N_MOD - 3]], axis=1)
    mod_h = [m0[:b], m_rest[:b, N_MOD - 3:]]
    mod_g = jnp.broadcast_to(m0[b:b + 1], (b, N_MOD, d))

    kr_start = POOL_DIM + Q_LORA_RANK + KV_LORA_RANK
    w_in = ab_w_in[0, :, :kr_start]
    w_kr = jnp.pad(ab_w_in[0, :, kr_start:], ((0, 0), (0, HEAD_PAD - QK_ROPE_DIM)))
    wq = _head_pad(w_uq[0], QK_HEAD_DIM, 0, QK_HEAD_DIM).astype(BF16)
    wk = _head_pad(w_ukv[0], QK_NOPE_DIM + V_HEAD_DIM, 0, QK_NOPE_DIM).T.astype(BF16)
    wv = w_ukv[0].reshape(KV_LORA_RANK, MLA_HEADS, QK_NOPE_DIM + V_HEAD_DIM)[:, :, QK_NOPE_DIM:]
    wv = wv.reshape(KV_LORA_RANK, ATTN_V_DIM).astype(BF16)
    lane = jnp.arange(HEAD_PAD)
    rotary = jnp.logical_and(lane >= QK_NOPE_DIM, lane < QK_HEAD_DIM)
    place = jnp.tile(jnp.where(rotary[:, None], jnp.eye(HEAD_PAD, dtype=F32), 0.0), (MLA_HEADS, 1)).astype(BF16)
    qg = q_norm_g[0].reshape(1, Q_LORA_RANK)
    kvg = kv_norm_g[0].reshape(1, KV_LORA_RANK)
    tabs = _rope_tables(l)

    pool, q, k_h, v_h = _mix_in(h, mod_h[0], norm_g[0, 1], w_in, w_kr, kvg, wk, wv, place, tm_mix,
                                latent_args=(qg, wq, tabs))
    k_g, v_g = _mix_in(g, mod_g, norm_g[0, 1], w_in, w_kr, kvg, wk, wv, place, tm_g)
    attn = _attention(q.reshape(b, l, -1), k_h, k_g, v_h.reshape(b, l, -1), v_g.reshape(b, t_g, -1), tq)
    h = _mix_out(h.reshape(b, l, d), mod_h[0], pool.reshape(b, l, -1), attn, pool_w,
                 pool_scale.reshape(-1, 1, POOL_DIM), ab_w_out, tm_mix)
    h = _ffn_half(h.reshape(b * l, d), mod_h[0], norm_g[0, 2], *ffn_w, (0, 1), k=2, tm=tm_h)

    h = _ffn_half(h, mod_h[1], norm_g[1, 0], *ffn_w, (1, 0), k=0, tm=tm_h)
    h = _conv_mixer(h.reshape(b, l, d), mod_h[1], norm_g[1, 1], conv_w_in, conv_w, conv_w_out, tm_mix)
    h = _ffn_half(h.reshape(b * l, d), mod_h[1], norm_g[1, 2], *ffn_w, (1, 1), k=2, tm=tm_h,
                  final_gain=final_norm_g)
    return h.reshape(b, l, d)
```

```python
import functools
import math

import jax
import jax.numpy as jnp
import numpy as np
from jax import lax
from jax.experimental import pallas as pl
from jax.experimental.pallas import tpu as pltpu

D_MODEL = 1024
GRID_W = 64
RMS_EPS = 1e-6
N_MOD = 9
D_FF = 2816
POOL_WINDOWS = (2, 4, 8, 16)
POOL_DIM = D_MODEL // 2
POOL_GROUP_DIM = POOL_DIM // len(POOL_WINDOWS)
MLA_HEADS = D_MODEL // 128
QK_NOPE_DIM = 64
QK_ROPE_DIM = 32
QK_HEAD_DIM = QK_NOPE_DIM + QK_ROPE_DIM
V_HEAD_DIM = 64
Q_LORA_RANK = 768
KV_LORA_RANK = 256
ROPE_AXIS_DIM = QK_ROPE_DIM // 2
ROPE_THETA = 10000.0
ATTN_SCALE = 1.0 / math.sqrt(QK_HEAD_DIM)
ATTN_V_DIM = MLA_HEADS * V_HEAD_DIM

LANES = 128
SUBLANES = 8
HEAD_PAD = LANES
QK_PAD_DIM = MLA_HEADS * HEAD_PAD
MOD_ROWS = 16
MOD_TILE_N = 1536
SIDE_MOD_TILE_N = 512
FF_CHUNK = 256
N_FF_CHUNKS = D_FF // FF_CHUNK
FF_HEAD_ROWS = 256
FF_TAIL_ROWS = 256
HALO = SUBLANES
PV_WIDTH = 256
PV_HEADS = PV_WIDTH // V_HEAD_DIM
ATTN_SUB_Q = 256
V7X_VMEM_BYTES = 64 * 1024 * 1024
VMEM_LIMIT = V7X_VMEM_BYTES * 7 // 8
FFN_TILE_ROWS = 512
ATTN_TILE_ROWS = 512
MIX_TILE_ROWS = 1024

BF16 = jnp.bfloat16
F32 = jnp.float32


def _sigmoid(x):
    return 1.0 / (1.0 + jnp.exp(-x))


def _rms(x):
    return x * lax.rsqrt(jnp.mean(x * x, axis=-1, keepdims=True) + RMS_EPS)


def _adaln(x, gain, mod_ref, k):
    shift = mod_ref[0, 3 * k:3 * k + 1, :]
    scale = mod_ref[0, 3 * k + 1:3 * k + 2, :]
    return _rms(x) * gain * (1.0 + scale) + shift


_NT = (((1,), (1,)), ((), ()))


def _mm(a, w):
    return lax.dot_general(a, w, (((1,), (0,)), ((), ())), preferred_element_type=F32)


def _params(semantics):
    return pltpu.CompilerParams(dimension_semantics=semantics, vmem_limit_bytes=VMEM_LIMIT)


def _mod_kernel(cond_ref, w_ref, b_ref, o_ref):
    cond = cond_ref[...]
    o_ref[...] = _mm((cond * _sigmoid(cond)).astype(BF16), w_ref[...]) + b_ref[...]


def _modulation(cond, w_mod, b_mod, n_cols):
    depth, d, n = w_mod.shape
    return pl.pallas_call(
        _mod_kernel,
        grid=(n_cols // MOD_TILE_N,),
        in_specs=[
            pl.BlockSpec((MOD_ROWS, d), lambda j: (0, 0)),
            pl.BlockSpec((None, d, MOD_TILE_N), lambda j: (0, 0, j)),
            pl.BlockSpec((None, 1, MOD_TILE_N), lambda j: (0, 0, j)),
        ],
        out_specs=pl.BlockSpec((MOD_ROWS, MOD_TILE_N), lambda j: (0, j)),
        out_shape=jax.ShapeDtypeStruct((MOD_ROWS, n_cols), F32),
        compiler_params=_params(("arbitrary",)),
        name="modulation",
    )(cond, w_mod, b_mod.reshape(depth, 1, n))


def _ffn_kernel(*refs, k, final, sel, n_side, n_first):
    refs = list(refs)
    s_ref, mod_ref, gain_ref, wg_hbm, wu_hbm, wd_hbm = refs[:6]
    u_scr, a_scr, wg_ref, wu_ref, wd_ref, sem = refs[-6:]
    extra = refs[6:-6]
    fg_ref = extra.pop(0) if final else None
    s2_ref, mod2_ref = (extra.pop(0), extra.pop(0)) if n_first else (None, None)
    if n_side:
        cond_ref, wm_ref, bm_ref, o_ref, om_ref = extra
    else:
        (o_ref,) = extra
    tm = s_ref.shape[0]

    second = pl.program_id(0) >= n_first if n_first else None

    def tile_rows(rows):
        return s_ref[rows, :] if second is None else jnp.where(second, s2_ref[rows, :], s_ref[rows, :])

    def mod_row(r):
        row = mod_ref[0, r:r + 1, :]
        return row if second is None else jnp.where(second, mod2_ref[0, r:r + 1, :], row)

    def chunk_copies(f):
        cols = slice(f * FF_CHUNK, (f + 1) * FF_CHUNK)
        return (pltpu.make_async_copy(wg_hbm.at[sel[0], sel[1], :, cols], wg_ref.at[:, cols], sem.at[2 * f]),
                pltpu.make_async_copy(wu_hbm.at[sel[0], sel[1], :, cols], wu_ref.at[:, cols], sem.at[2 * f + 1]))

    def down_copy():
        return pltpu.make_async_copy(wd_hbm.at[sel[0], sel[1]], wd_ref, sem.at[2 * N_FF_CHUNKS])

    def hidden(rows, f):
        cols = slice(f * FF_CHUNK, (f + 1) * FF_CHUNK)
        g = _mm(u_scr[rows, :], wg_ref[:, cols])
        up = _mm(u_scr[rows, :], wu_ref[:, cols])
        a_scr[rows, cols] = (g * _sigmoid(g) * up).astype(BF16)

    def body(first_step):
        if first_step:
            for f in range(N_FF_CHUNKS):
                for cp in chunk_copies(f):
                    cp.start()
            down_copy().start()
        if n_side:
            cond = cond_ref[...]
            om_ref[...] = _mm((cond * _sigmoid(cond)).astype(BF16), wm_ref[...]) + bm_ref[...]
        for p in range(tm // FF_HEAD_ROWS):
            rows = slice(p * FF_HEAD_ROWS, (p + 1) * FF_HEAD_ROWS)
            u = _rms(tile_rows(rows)) * gain_ref[...] * (1.0 + mod_row(3 * k + 1)) + mod_row(3 * k)
            u_scr[rows, :] = u.astype(BF16)
            if first_step and p == 0:
                for cp in chunk_copies(0):
                    cp.wait()
            hidden(rows, 0)
        for f in range(1, N_FF_CHUNKS):
            if first_step:
                for cp in chunk_copies(f):
                    cp.wait()
            hidden(slice(None), f)
        if first_step:
            down_copy().wait()
        gate = mod_row(3 * k + 2)
        for p in range(tm // FF_TAIL_ROWS):
            rows = slice(p * FF_TAIL_ROWS, (p + 1) * FF_TAIL_ROWS)
            y = tile_rows(rows) + 0.5 * gate * _mm(a_scr[rows, :], wd_ref[...])
            if final:
                y = _rms(y) * fg_ref[...]
            o_ref[rows, :] = y

    pl.when(pl.program_id(0) == 0)(functools.partial(body, True))
    pl.when(pl.program_id(0) != 0)(functools.partial(body, False))


def _ffn_half(s, mod, gain, wg, wu, wd, sel, k, tm, final_gain=None, side_mod=None, second=None):
    n_tok, d = s.shape
    rows_per_mod = n_tok // mod.shape[0]
    assert rows_per_mod % tm == 0
    final = final_gain is not None
    n_first = n_tok // tm if second is not None else 0
    n_tiles = n_tok // tm
    first = (lambda i: jnp.minimum(i, n_first - 1)) if second is not None else (lambda i: i)
    hbm = pl.BlockSpec(memory_space=pl.ANY)
    in_specs = [
        pl.BlockSpec((tm, d), lambda i: (first(i), 0)),
        pl.BlockSpec((1, mod.shape[1], d), lambda i: (first(i) * tm // rows_per_mod, 0, 0)),
        pl.BlockSpec((1, d), lambda i: (0, 0)),
        hbm, hbm, hbm,
    ]
    args = [s, mod, gain.reshape(1, d), wg, wu, wd]
    if final:
        in_specs.append(pl.BlockSpec((1, d), lambda i: (0, 0)))
        args.append(final_gain.reshape(1, d))
    if second is not None:
        s2, mod2 = second
        assert s2.shape[0] % tm == 0 and mod2.shape[0] == 1 and mod2.shape[1:] == mod.shape[1:]
        in_specs += [pl.BlockSpec((tm, d), lambda i: (jnp.maximum(i - n_first, 0), 0)),
                     pl.BlockSpec(mod2.shape, lambda i: (0, 0, 0))]
        args += [s2, mod2]
        n_tiles += s2.shape[0] // tm
    out_specs = pl.BlockSpec((tm, d), lambda i: (i, 0))
    out_shape = jax.ShapeDtypeStruct((n_tiles * tm, d), F32)
    n_side = 0
    if side_mod is not None:
        cond, w_mod, b_mod, first_col = side_mod
        n_mod = w_mod.shape[-1]
        c0 = first_col // SIDE_MOD_TILE_N
        n_l0 = n_mod // SIDE_MOD_TILE_N - c0
        n_side = n_l0 + n_mod // SIDE_MOD_TILE_N
        assert n_side <= n_tiles and first_col % SIDE_MOD_TILE_N == 0

        def src(i):
            j = jnp.minimum(i, n_side - 1)
            return jnp.where(j < n_l0, 0, 1), 0, jnp.where(j < n_l0, c0 + j, j - n_l0)

        in_specs += [pl.BlockSpec(cond.shape, lambda i: (0, 0)),
                     pl.BlockSpec((None, d, SIDE_MOD_TILE_N), src),
                     pl.BlockSpec((None, 1, SIDE_MOD_TILE_N), src)]
        args += [cond, w_mod, b_mod.reshape(b_mod.shape[0], 1, n_mod)]
        out_specs = [out_specs, pl.BlockSpec((cond.shape[0], SIDE_MOD_TILE_N),
                                             lambda i: (0, jnp.minimum(i, n_side - 1)))]
        out_shape = [out_shape, jax.ShapeDtypeStruct((cond.shape[0], n_side * SIDE_MOD_TILE_N), F32)]
    return pl.pallas_call(
        functools.partial(_ffn_kernel, k=k, final=final, sel=sel, n_side=n_side, n_first=n_first),
        grid=(n_tiles,),
        in_specs=in_specs,
        out_specs=out_specs,
        out_shape=out_shape,
        scratch_shapes=[pltpu.VMEM((tm, d), BF16), pltpu.VMEM((tm, D_FF), BF16),
                        pltpu.VMEM(wg.shape[2:], F32), pltpu.VMEM(wu.shape[2:], F32),
                        pltpu.VMEM(wd.shape[2:], F32),
                        pltpu.SemaphoreType.DMA((2 * N_FF_CHUNKS + 1,))],
        compiler_params=_params(("arbitrary",)),
        name="ffn_half",
    )(*args)


def _rope(z, c_ref, s1_ref, s2_ref):
    fwd = pltpu.roll(z, HEAD_PAD - ROPE_AXIS_DIM // 2, axis=1)
    bwd = pltpu.roll(z, ROPE_AXIS_DIM // 2, axis=1)
    return z * c_ref[...] + fwd * s1_ref[...] + bwd * s2_ref[...]


def _mix_in_kernel(*refs, latent):
    if latent:
        (s_ref, mod_ref, gain_ref, w_in_ref, w_kr_ref, qg_ref, wq_ref, kvg_ref, wk_ref, wv_ref, place_ref,
         c_ref, s1_ref, s2_ref, pool_ref, q_ref, kt_ref, v_ref) = refs
    else:
        (s_ref, mod_ref, gain_ref, w_in_ref, w_kr_ref, kvg_ref, wk_ref, wv_ref, place_ref,
         kt_ref, v_ref) = refs
    u = _adaln(s_ref[...], gain_ref[...], mod_ref, 1).astype(BF16)
    cuts = (POOL_DIM, POOL_DIM + Q_LORA_RANK, POOL_DIM + Q_LORA_RANK + KV_LORA_RANK)
    z = _mm(u, w_in_ref[:, :cuts[2]])
    ckv = (_rms(z[:, cuts[1]:cuts[2]]) * kvg_ref[...]).astype(BF16)
    kr = pltpu.roll(_mm(u, w_kr_ref[...]), QK_NOPE_DIM, axis=1)
    if latent:
        pool_ref[...] = z[:, :cuts[0]]
        cq = (_rms(z[:, cuts[0]:cuts[1]]) * qg_ref[...]).astype(BF16)
        q = jnp.dot(cq, wq_ref[...], preferred_element_type=F32)
        for h in range(MLA_HEADS):
            sl = slice(h * HEAD_PAD, (h + 1) * HEAD_PAD)
            q_ref[:, sl] = _rope(q[:, sl], c_ref, s1_ref, s2_ref).astype(BF16)
        kr = _rope(kr, c_ref, s1_ref, s2_ref)
    kt = lax.dot_general(wk_ref[...], ckv, _NT, preferred_element_type=F32)
    kt = kt + lax.dot_general(place_ref[...], kr.astype(BF16), _NT, preferred_element_type=F32)
    kt_ref[...] = kt.astype(BF16).reshape(kt_ref.shape)
    v_ref[...] = jnp.dot(ckv, wv_ref[...], preferred_element_type=F32).astype(BF16)


def _mix_in(s, mod, gain, w_in, w_kr, kvg, wk, wv, place, tm, latent_args=None, rows=None):
    row0, n_tok = rows if rows is not None else (0, s.shape[0])
    d = s.shape[1]
    assert row0 % tm == 0
    rows_per_mod = n_tok // mod.shape[0]
    assert rows_per_mod % tm == 0
    tiles_per_batch = rows_per_mod // tm
    latent = latent_args is not None
    const = lambda a: pl.BlockSpec(a.shape, lambda i: (0,) * a.ndim)
    tok = lambda w: pl.BlockSpec((tm, w), lambda i: (i, 0))
    in_specs = [pl.BlockSpec((tm, d), lambda i: (row0 // tm + i, 0)),
                pl.BlockSpec((1, N_MOD, d), lambda i: (i // tiles_per_batch, 0, 0)),
                pl.BlockSpec((1, d), lambda i: (0, 0)),
                pl.BlockSpec((None,) + w_in.shape[1:], lambda i: (0, 0, 0), pipeline_mode=pl.Buffered(1)),
                const(w_kr)]
    args = [s, mod, gain.reshape(1, d), w_in, w_kr]
    if latent:
        qg, wq, tabs = latent_args
        in_specs += [const(qg), const(wq)]
        args += [qg, wq]
    in_specs += [const(kvg), const(wk), const(wv), const(place)]
    args += [kvg, wk, wv, place]
    if latent:
        kt_spec = pl.BlockSpec((1, QK_PAD_DIM, tm), lambda i: (i // tiles_per_batch, 0, i % tiles_per_batch))
        kt_shape = (mod.shape[0], QK_PAD_DIM, rows_per_mod)
    else:
        kt_spec = pl.BlockSpec((QK_PAD_DIM, tm), lambda i: (0, i))
        kt_shape = (QK_PAD_DIM, n_tok)
    out_specs = [kt_spec, tok(ATTN_V_DIM)]
    out_shape = [jax.ShapeDtypeStruct(kt_shape, BF16), jax.ShapeDtypeStruct((n_tok, ATTN_V_DIM), BF16)]
    if latent:
        rope_spec = pl.BlockSpec((tm, HEAD_PAD), lambda i: (i % tiles_per_batch, 0))
        in_specs += [rope_spec] * 3
        args += list(tabs)
        out_specs = [tok(POOL_DIM), tok(QK_PAD_DIM)] + out_specs
        out_shape = [jax.ShapeDtypeStruct((n_tok, POOL_DIM), F32),
                     jax.ShapeDtypeStruct((n_tok, QK_PAD_DIM), BF16)] + out_shape
    return pl.pallas_call(
        functools.partial(_mix_in_kernel, latent=latent),
        grid=(n_tok // tm,),
        in_specs=in_specs,
        out_specs=out_specs,
        out_shape=out_shape,
        compiler_params=_params(("arbitrary",)),
        name="mix_in_latent" if latent else "mix_in_context",
    )(*args)


def _attn_kernel(q_ref, kth_ref, ktg_ref, vh_ref, vg_ref, o_ref):
    c = ATTN_SCALE * math.log2(math.e)
    lane_head = lax.broadcasted_iota(jnp.int32, (1, PV_WIDTH), 1) // V_HEAD_DIM
    for sub in range(q_ref.shape[1] // ATTN_SUB_Q):
        rows = slice(sub * ATTN_SUB_Q, (sub + 1) * ATTN_SUB_Q)
        for grp in range(MLA_HEADS // PV_HEADS):
            vcols = slice(grp * PV_WIDTH, (grp + 1) * PV_WIDTH)
            acc = None
            for hh in range(PV_HEADS):
                h = grp * PV_HEADS + hh
                qk = slice(h * HEAD_PAD, (h + 1) * HEAD_PAD)
                q = q_ref[0, rows, qk]
                s_h = jnp.dot(q, kth_ref[0, qk, :], preferred_element_type=F32)
                s_g = jnp.dot(q, ktg_ref[qk, :], preferred_element_type=F32)
                m = jnp.maximum(jnp.max(s_h, axis=-1, keepdims=True),
                                jnp.max(s_g, axis=-1, keepdims=True))
                e_h = jnp.exp2((s_h - m) * c)
                e_g = jnp.exp2((s_g - m) * c)
                denom = jnp.sum(e_h, axis=-1, keepdims=True) + jnp.sum(e_g, axis=-1, keepdims=True)
                res = jnp.dot(e_h.astype(BF16), vh_ref[0, :, vcols], preferred_element_type=F32)
                res = res + jnp.dot(e_g.astype(BF16), vg_ref[0, :, vcols], preferred_element_type=F32)
                term = jnp.where(lane_head == hh, res * (1.0 / denom), 0.0)
                acc = term if acc is None else acc + term
            o_ref[0, rows, vcols] = acc.astype(BF16)


def _attention(q, kt_h, kt_g, v_h, v_g, tq):
    b, l, _ = q.shape
    t_g = v_g.shape[1]
    return pl.pallas_call(
        _attn_kernel,
        grid=(b, l // tq),
        in_specs=[
            pl.BlockSpec((1, tq, QK_PAD_DIM), lambda i, j: (i, j, 0)),
            pl.BlockSpec((1, QK_PAD_DIM, l), lambda i, j: (i, 0, 0)),
            pl.BlockSpec((QK_PAD_DIM, t_g), lambda i, j: (0, i)),
            pl.BlockSpec((1, l, ATTN_V_DIM), lambda i, j: (i, 0, 0)),
            pl.BlockSpec((1, t_g, ATTN_V_DIM), lambda i, j: (i, 0, 0)),
        ],
        out_specs=pl.BlockSpec((1, tq, ATTN_V_DIM), lambda i, j: (i, j, 0)),
        out_shape=jax.ShapeDtypeStruct((b, l, ATTN_V_DIM), BF16),
        compiler_params=_params(("arbitrary", "arbitrary")),
        name="latent_attention",
    )(q, kt_h, kt_g, v_h, v_g)


def _mix_out_kernel(h_ref, mod_ref, pool_ref, poolp_ref, pooln_ref, attn_ref, pw_ref, ps_ref, wo_ref,
                    o_ref, win_scr, y_scr, *, tm, seq):
    t0 = pl.program_id(1) * tm
    win_scr[0:HALO, :] = jnp.where(t0 > 0, poolp_ref[0], 0.0)
    win_scr[HALO:HALO + tm, :] = pool_ref[0]
    win_scr[HALO + tm:, :] = jnp.where(t0 + tm < seq, pooln_ref[0], 0.0)
    t = (t0 + lax.broadcasted_iota(jnp.int32, (tm, 1), 0)).astype(F32)
    n = tm + 2 * HALO
    for g, w in enumerate(POOL_WINDOWS):
        lanes = slice(g * POOL_GROUP_DIM, (g + 1) * POOL_GROUP_DIM)
        fwd = win_scr[:, lanes]
        span = 1
        while span < w:
            fwd = fwd + pltpu.roll(fwd, n - span, axis=0)
            span *= 2
        total = pltpu.roll(fwd, w // 2, axis=0)[HALO:HALO + tm]
        cnt = jnp.minimum(t, float(w // 2)) + jnp.minimum(float(seq - 1) - t, float(w - w // 2 - 1)) + 1.0
        p = total / cnt - win_scr[HALO:HALO + tm, lanes]
        y = _mm(p.astype(BF16), pw_ref[g])
        y_scr[:, lanes] = (y * ps_ref[:, lanes]).astype(BF16)
    out = _mm(y_scr[...], wo_ref[:POOL_DIM, :]) + _mm(attn_ref[0], wo_ref[POOL_DIM:, :])
    gate = mod_ref[0, 3 * 1 + 2:3 * 1 + 3, :]
    o_ref[0] = h_ref[...] + gate * out


def _mix_out(h, mod, pool, attn, pool_w, pool_scale, w_out, tm):
    b, l, _ = pool.shape
    d = h.shape[1]
    hb = tm // HALO
    const = lambda a: pl.BlockSpec((None,) + a.shape[1:], lambda i, j: (0,) * a.ndim,
                                   pipeline_mode=pl.Buffered(1))
    return pl.pallas_call(
        functools.partial(_mix_out_kernel, tm=tm, seq=l),
        grid=(b, l // tm),
        in_specs=[
            pl.BlockSpec((tm, d), lambda i, j: (i * (l // tm) + j, 0)),
            pl.BlockSpec((1, N_MOD, d), lambda i, j: (i, 0, 0)),
            pl.BlockSpec((1, tm, POOL_DIM), lambda i, j: (i, j, 0)),
            pl.BlockSpec((1, HALO, POOL_DIM), lambda i, j: (i, jnp.maximum(j * hb - 1, 0), 0)),
            pl.BlockSpec((1, HALO, POOL_DIM), lambda i, j: (i, jnp.minimum((j + 1) * hb, l // HALO - 1), 0)),
            pl.BlockSpec((1, tm, ATTN_V_DIM), lambda i, j: (i, j, 0)),
            const(pool_w), const(pool_scale), const(w_out),
        ],
        out_specs=pl.BlockSpec((1, tm, d), lambda i, j: (i, j, 0)),
        out_shape=jax.ShapeDtypeStruct((b, l, d), F32),
        scratch_shapes=[pltpu.VMEM((tm + 2 * HALO, POOL_DIM), F32), pltpu.VMEM((tm, POOL_DIM), BF16)],
        compiler_params=_params(("arbitrary", "arbitrary")),
        name="mix_out",
    )(h, mod, pool, pool, pool, attn, pool_w, pool_scale, w_out)


def _conv_kernel(h_ref, hp_ref, hn_ref, mod_ref, gain_ref, w_in_ref, cw_ref, w_out_ref, o_ref,
                 u_scr, z_scr, *, tm, seq):
    t0 = pl.program_id(1) * tm
    gain = gain_ref[...]
    u_scr[0:HALO, :] = _adaln(hp_ref[0], gain, mod_ref, 1).astype(BF16)
    u_scr[HALO:HALO + tm, :] = _adaln(h_ref[0], gain, mod_ref, 1).astype(BF16)
    u_scr[HALO + tm:, :] = _adaln(hn_ref[0], gain, mod_ref, 1).astype(BF16)
    d = h_ref.shape[-1]
    cv = _mm(u_scr[...], w_in_ref[:, d:])
    t = t0 - HALO + lax.broadcasted_iota(jnp.int32, (tm + 2 * HALO, 1), 0)
    inside = jnp.logical_and(t >= 0, t < seq)
    z_scr[...] = jnp.where(inside, cv[:, :d] * cv[:, d:], 0.0)
    y = (cw_ref[0:1, :] * z_scr[HALO - 1:HALO - 1 + tm, :]
         + cw_ref[1:2, :] * z_scr[HALO:HALO + tm, :]
         + cw_ref[2:3, :] * z_scr[HALO + 1:HALO + 1 + tm, :])
    bg = _mm(u_scr[HALO:HALO + tm, :], w_in_ref[:, :d])
    out = _mm((bg * y).astype(BF16), w_out_ref[...])
    gate = mod_ref[0, 3 * 1 + 2:3 * 1 + 3, :]
    o_ref[0] = h_ref[0] + gate * out


def _conv_mixer(h, mod, gain, w_in, conv_w, w_out, tm):
    b, l, d = h.shape
    hb = tm // HALO
    const = lambda a: pl.BlockSpec((None,) + a.shape[1:], lambda i, j: (0,) * a.ndim,
                                   pipeline_mode=pl.Buffered(1))
    return pl.pallas_call(
        functools.partial(_conv_kernel, tm=tm, seq=l),
        grid=(b, l // tm),
        in_specs=[
            pl.BlockSpec((1, tm, d), lambda i, j: (i, j, 0)),
            pl.BlockSpec((1, HALO, d), lambda i, j: (i, jnp.maximum(j * hb - 1, 0), 0)),
            pl.BlockSpec((1, HALO, d), lambda i, j: (i, jnp.minimum((j + 1) * hb, l // HALO - 1), 0)),
            pl.BlockSpec((1, N_MOD, d), lambda i, j: (i, 0, 0)),
            pl.BlockSpec((1, d), lambda i, j: (0, 0)),
            const(w_in), const(conv_w), const(w_out),
        ],
        out_specs=pl.BlockSpec((1, tm, d), lambda i, j: (i, j, 0)),
        out_shape=jax.ShapeDtypeStruct((b, l, d), F32),
        scratch_shapes=[pltpu.VMEM((tm + 2 * HALO, d), BF16), pltpu.VMEM((tm + 2 * HALO, d), F32)],
        compiler_params=_params(("arbitrary", "arbitrary")),
        name="conv_mixer",
    )(h, h, h, mod, gain.reshape(1, d), w_in, conv_w, w_out)


def _head_pad(w, per_head, start, width):
    r = w.shape[0]
    w = w.reshape(r, MLA_HEADS, per_head)[:, :, start:start + width]
    return jnp.pad(w, ((0, 0), (0, 0), (0, HEAD_PAD - width))).reshape(r, QK_PAD_DIM)


def _rope_tables(length):
    pos = np.arange(length)
    row = (pos // GRID_W).astype(np.float32)
    col = (pos % GRID_W).astype(np.float32)
    half = ROPE_AXIS_DIM // 2
    freqs = np.power(np.float32(ROPE_THETA),
                     -np.arange(0, ROPE_AXIS_DIM, 2, dtype=np.float32) / np.float32(ROPE_AXIS_DIM))
    lane = np.arange(HEAD_PAD)
    o = lane - QK_NOPE_DIM
    rotary = np.logical_and(o >= 0, o < QK_ROPE_DIM)
    o = np.clip(o, 0, QK_ROPE_DIM - 1)
    ang = np.where((o // ROPE_AXIS_DIM == 0)[None, :], row[:, None], col[:, None]) * freqs[o % half][None, :]
    ang = ang.astype(np.float32)
    first = (o % ROPE_AXIS_DIM) < half
    cos = np.where(rotary[None, :], np.cos(ang), 1.0).astype(np.float32)
    sin = np.where(rotary[None, :], np.sin(ang), 0.0).astype(np.float32)
    return (jnp.asarray(cos), jnp.asarray(np.where(first[None, :], -sin, 0.0).astype(np.float32)),
            jnp.asarray(np.where(first[None, :], 0.0, sin).astype(np.float32)))


def kernel(x, c, ctx, c_ctx, norm_g, w_mod, b_mod, ffn_w_gate, ffn_w_up, ffn_w_down, ab_w_in, pool_w,
           pool_scale, q_norm_g, w_uq, kv_norm_g, w_ukv, ab_w_out, conv_w_in, conv_w, conv_w_out,
           final_norm_g):
    b, l, d = x.shape
    t_g = ctx.shape[1]
    tm_h, tq, tm_mix = FFN_TILE_ROWS, ATTN_TILE_ROWS, MIX_TILE_ROWS
    assert w_mod.shape[0] == 2 and ab_w_in.shape[0] == 1 and conv_w_in.shape[0] == 1
    assert l % tm_h == 0 and l % tq == 0 and l % tm_mix == 0 and t_g % LANES == 0 and l % GRID_W == 0

    cond = jnp.zeros((MOD_ROWS, d), F32).at[:b].set(c).at[b].set(c_ctx)
    n_first = 3 * d
    m_first = _modulation(cond, w_mod, b_mod, n_first).reshape(MOD_ROWS, 3, d)

    ffn_w = (ffn_w_gate, ffn_w_up, ffn_w_down)

    h = x.reshape(b * l, d)
    g = ctx.reshape(b * t_g, d)

    hg, m_rest = _ffn_half(h, m_first[:b], norm_g[0, 0], *ffn_w, (0, 0), k=0, tm=tm_h,
                           side_mod=(cond, w_mod, b_mod, n_first), second=(g, m_first[b:b + 1]))
    m_rest = m_rest.reshape(MOD_ROWS, 2 * N_MOD - 3, d)
    m0 = jnp.concatenate([m_first, m_rest[:, :N_MOD - 3]], axis=1)
    mod_h = [m0[:b], m_rest[:b, N_MOD - 3:]]

    kr_start = POOL_DIM + Q_LORA_RANK + KV_LORA_RANK
    w_kr = jnp.pad(ab_w_in[0, :, kr_start:], ((0, 0), (0, HEAD_PAD - QK_ROPE_DIM)))
    wq = _head_pad(w_uq[0], QK_HEAD_DIM, 0, QK_HEAD_DIM).astype(BF16)
    wk = _head_pad(w_ukv[0], QK_NOPE_DIM + V_HEAD_DIM, 0, QK_NOPE_DIM).T.astype(BF16)
    wv = w_ukv[0].reshape(KV_LORA_RANK, MLA_HEADS, QK_NOPE_DIM + V_HEAD_DIM)[:, :, QK_NOPE_DIM:]
    wv = wv.reshape(KV_LORA_RANK, ATTN_V_DIM).astype(BF16)
    lane = jnp.arange(HEAD_PAD)
    rotary = jnp.logical_and(lane >= QK_NOPE_DIM, lane < QK_HEAD_DIM)
    place = jnp.tile(jnp.where(rotary[:, None], jnp.eye(HEAD_PAD, dtype=F32), 0.0), (MLA_HEADS, 1)).astype(BF16)
    qg = q_norm_g[0].reshape(1, Q_LORA_RANK)
    kvg = kv_norm_g[0].reshape(1, KV_LORA_RANK)
    tabs = _rope_tables(l)

    pool, q, k_h, v_h = _mix_in(hg, mod_h[0], norm_g[0, 1], ab_w_in, w_kr, kvg, wk, wv, place, tm_mix,
                                rows=(0, b * l),
                                latent_args=(qg, wq, tabs))
    k_g, v_g = _mix_in(hg, m0[b:b + 1], norm_g[0, 1], ab_w_in, w_kr, kvg, wk, wv, place, tm_mix,
                       rows=(b * l, b * t_g))
    attn = _attention(q.reshape(b, l, -1), k_h, k_g, v_h.reshape(b, l, -1), v_g.reshape(b, t_g, -1), tq)
    h = _mix_out(hg, mod_h[0], pool.reshape(b, l, -1), attn, pool_w,
                 pool_scale.reshape(-1, 1, POOL_DIM), ab_w_out, tm_mix)
    h = _ffn_half(h.reshape(b * l, d), mod_h[0], norm_g[0, 2], *ffn_w, (0, 1), k=2, tm=tm_h)

    h = _ffn_half(h, mod_h[1], norm_g[1, 0], *ffn_w, (1, 0), k=0, tm=tm_h)
    h = _conv_mixer(h.reshape(b, l, d), mod_h[1], norm_g[1, 1], conv_w_in, conv_w, conv_w_out, tm_mix)
    h = _ffn_half(h.reshape(b * l, d), mod_h[1], norm_g[1, 2], *ffn_w, (1, 1), k=2, tm=tm_h,
                  final_gain=final_norm_g)
    return h.reshape(b, l, d)
```

```python
import functools
import math

import jax
import jax.numpy as jnp
import numpy as np
from jax import lax
from jax.experimental import pallas as pl
from jax.experimental.pallas import tpu as pltpu

D_MODEL = 1024
GRID_W = 64
RMS_EPS = 1e-6
N_MOD = 9
D_FF = 2816
POOL_WINDOWS = (2, 4, 8, 16)
POOL_DIM = D_MODEL // 2
POOL_GROUP_DIM = POOL_DIM // len(POOL_WINDOWS)
MLA_HEADS = D_MODEL // 128
QK_NOPE_DIM = 64
QK_ROPE_DIM = 32
QK_HEAD_DIM = QK_NOPE_DIM + QK_ROPE_DIM
V_HEAD_DIM = 64
Q_LORA_RANK = 768
KV_LORA_RANK = 256
ROPE_AXIS_DIM = QK_ROPE_DIM // 2
ROPE_THETA = 10000.0
ATTN_SCALE = 1.0 / math.sqrt(QK_HEAD_DIM)
ATTN_V_DIM = MLA_HEADS * V_HEAD_DIM

LANES = 128
SUBLANES = 8
HEAD_PAD = LANES
QK_PAD_DIM = MLA_HEADS * HEAD_PAD
MOD_ROWS = 16
MOD_TILE_N = 1536
SIDE_MOD_TILE_N = 512
FF_CHUNK = 256
N_FF_CHUNKS = D_FF // FF_CHUNK
FF_HEAD_ROWS = 256
FF_TAIL_ROWS = 256
HALO = SUBLANES
PV_WIDTH = 256
PV_HEADS = PV_WIDTH // V_HEAD_DIM
ATTN_SUB_Q = 256
V7X_VMEM_BYTES = 64 * 1024 * 1024
VMEM_LIMIT = V7X_VMEM_BYTES * 7 // 8
FFN_TILE_ROWS = 512
ATTN_TILE_ROWS = 512
MIX_TILE_ROWS = 1024
H_RING_SLOTS = 3

BF16 = jnp.bfloat16
F32 = jnp.float32


def _sigmoid(x):
    return 1.0 / (1.0 + jnp.exp(-x))


def _rms(x):
    return x * lax.rsqrt(jnp.mean(x * x, axis=-1, keepdims=True) + RMS_EPS)


def _adaln(x, gain, mod_ref, k):
    shift = mod_ref[0, 3 * k:3 * k + 1, :]
    scale = mod_ref[0, 3 * k + 1:3 * k + 2, :]
    return _rms(x) * gain * (1.0 + scale) + shift


_NT = (((1,), (1,)), ((), ()))


def _mm(a, w):
    return lax.dot_general(a, w, (((1,), (0,)), ((), ())), preferred_element_type=F32)


def _params(semantics):
    return pltpu.CompilerParams(dimension_semantics=semantics, vmem_limit_bytes=VMEM_LIMIT)


def _mod_kernel(cond_ref, w_ref, b_ref, o_ref):
    cond = cond_ref[...]
    o_ref[...] = _mm((cond * _sigmoid(cond)).astype(BF16), w_ref[...]) + b_ref[...]


def _modulation(cond, w_mod, b_mod, n_cols):
    depth, d, n = w_mod.shape
    return pl.pallas_call(
        _mod_kernel,
        grid=(n_cols // MOD_TILE_N,),
        in_specs=[
            pl.BlockSpec((MOD_ROWS, d), lambda j: (0, 0)),
            pl.BlockSpec((None, d, MOD_TILE_N), lambda j: (0, 0, j)),
            pl.BlockSpec((None, 1, MOD_TILE_N), lambda j: (0, 0, j)),
        ],
        out_specs=pl.BlockSpec((MOD_ROWS, MOD_TILE_N), lambda j: (0, j)),
        out_shape=jax.ShapeDtypeStruct((MOD_ROWS, n_cols), F32),
        compiler_params=_params(("arbitrary",)),
        name="modulation",
    )(cond, w_mod, b_mod.reshape(depth, 1, n))


def _ffn_kernel(*refs, k, final, sel, n_side):
    refs = list(refs)
    s_ref, mod_ref, gain_ref, wg_hbm, wu_hbm, wd_hbm = refs[:6]
    u_scr, a_scr, wg_ref, wu_ref, wd_ref, sem = refs[-6:]
    extra = refs[6:-6]
    fg_ref = extra.pop(0) if final else None
    if n_side:
        cond_ref, wm_ref, bm_ref, o_ref, om_ref = extra
    else:
        (o_ref,) = extra
    tm = s_ref.shape[0]

    def chunk_copies(f):
        cols = slice(f * FF_CHUNK, (f + 1) * FF_CHUNK)
        return (pltpu.make_async_copy(wg_hbm.at[sel[0], sel[1], :, cols], wg_ref.at[:, cols], sem.at[2 * f]),
                pltpu.make_async_copy(wu_hbm.at[sel[0], sel[1], :, cols], wu_ref.at[:, cols], sem.at[2 * f + 1]))

    def down_copy():
        return pltpu.make_async_copy(wd_hbm.at[sel[0], sel[1]], wd_ref, sem.at[2 * N_FF_CHUNKS])

    def hidden(rows, f):
        cols = slice(f * FF_CHUNK, (f + 1) * FF_CHUNK)
        g = _mm(u_scr[rows, :], wg_ref[:, cols])
        up = _mm(u_scr[rows, :], wu_ref[:, cols])
        a_scr[rows, cols] = (g * _sigmoid(g) * up).astype(BF16)

    def body(first_step):
        if first_step:
            for f in range(N_FF_CHUNKS):
                for cp in chunk_copies(f):
                    cp.start()
            down_copy().start()
        if n_side:
            cond = cond_ref[...]
            om_ref[...] = _mm((cond * _sigmoid(cond)).astype(BF16), wm_ref[...]) + bm_ref[...]
        for p in range(tm // FF_HEAD_ROWS):
            rows = slice(p * FF_HEAD_ROWS, (p + 1) * FF_HEAD_ROWS)
            u_scr[rows, :] = _adaln(s_ref[rows, :], gain_ref[...], mod_ref, k).astype(BF16)
            if first_step and p == 0:
                for cp in chunk_copies(0):
                    cp.wait()
            hidden(rows, 0)
        for f in range(1, N_FF_CHUNKS):
            if first_step:
                for cp in chunk_copies(f):
                    cp.wait()
            hidden(slice(None), f)
        if first_step:
            down_copy().wait()
        gate = mod_ref[0, 3 * k + 2:3 * k + 3, :]
        for p in range(tm // FF_TAIL_ROWS):
            rows = slice(p * FF_TAIL_ROWS, (p + 1) * FF_TAIL_ROWS)
            y = s_ref[rows, :] + 0.5 * gate * _mm(a_scr[rows, :], wd_ref[...])
            if final:
                y = _rms(y) * fg_ref[...]
            o_ref[rows, :] = y

    pl.when(pl.program_id(0) == 0)(functools.partial(body, True))
    pl.when(pl.program_id(0) != 0)(functools.partial(body, False))


def _ffn_half(s, mod, gain, wg, wu, wd, sel, k, tm, final_gain=None, side_mod=None):
    n_tok, d = s.shape
    rows_per_mod = n_tok // mod.shape[0]
    assert rows_per_mod % tm == 0
    final = final_gain is not None
    hbm = pl.BlockSpec(memory_space=pl.ANY)
    in_specs = [
        pl.BlockSpec((tm, d), lambda i: (i, 0)),
        pl.BlockSpec((1, mod.shape[1], d), lambda i: (i * tm // rows_per_mod, 0, 0)),
        pl.BlockSpec((1, d), lambda i: (0, 0)),
        hbm, hbm, hbm,
    ]
    args = [s, mod, gain.reshape(1, d), wg, wu, wd]
    if final:
        in_specs.append(pl.BlockSpec((1, d), lambda i: (0, 0)))
        args.append(final_gain.reshape(1, d))
    out_specs = pl.BlockSpec((tm, d), lambda i: (i, 0))
    out_shape = jax.ShapeDtypeStruct((n_tok, d), F32)
    n_side = 0
    if side_mod is not None:
        cond, w_mod, b_mod, first_col = side_mod
        n_mod = w_mod.shape[-1]
        c0 = first_col // SIDE_MOD_TILE_N
        n_first = n_mod // SIDE_MOD_TILE_N - c0
        n_side = n_first + n_mod // SIDE_MOD_TILE_N
        assert n_side <= n_tok // tm and first_col % SIDE_MOD_TILE_N == 0

        def src(i):
            j = jnp.minimum(i, n_side - 1)
            return jnp.where(j < n_first, 0, 1), 0, jnp.where(j < n_first, c0 + j, j - n_first)

        in_specs += [pl.BlockSpec(cond.shape, lambda i: (0, 0)),
                     pl.BlockSpec((None, d, SIDE_MOD_TILE_N), src),
                     pl.BlockSpec((None, 1, SIDE_MOD_TILE_N), src)]
        args += [cond, w_mod, b_mod.reshape(b_mod.shape[0], 1, n_mod)]
        out_specs = [out_specs, pl.BlockSpec((cond.shape[0], SIDE_MOD_TILE_N),
                                             lambda i: (0, jnp.minimum(i, n_side - 1)))]
        out_shape = [out_shape, jax.ShapeDtypeStruct((cond.shape[0], n_side * SIDE_MOD_TILE_N), F32)]
    return pl.pallas_call(
        functools.partial(_ffn_kernel, k=k, final=final, sel=sel, n_side=n_side),
        grid=(n_tok // tm,),
        in_specs=in_specs,
        out_specs=out_specs,
        out_shape=out_shape,
        scratch_shapes=[pltpu.VMEM((tm, d), BF16), pltpu.VMEM((tm, D_FF), BF16),
                        pltpu.VMEM(wg.shape[2:], F32), pltpu.VMEM(wu.shape[2:], F32),
                        pltpu.VMEM(wd.shape[2:], F32),
                        pltpu.SemaphoreType.DMA((2 * N_FF_CHUNKS + 1,))],
        compiler_params=_params(("arbitrary",)),
        name="ffn_half",
    )(*args)


def _rope(z, c_ref, s1_ref, s2_ref):
    fwd = pltpu.roll(z, HEAD_PAD - ROPE_AXIS_DIM // 2, axis=1)
    bwd = pltpu.roll(z, ROPE_AXIS_DIM // 2, axis=1)
    return z * c_ref[...] + fwd * s1_ref[...] + bwd * s2_ref[...]


def _mix_in_kernel(*refs, latent):
    if latent:
        (s_ref, mod_ref, gain_ref, w_in_ref, w_kr_ref, qg_ref, wq_ref, kvg_ref, wk_ref, wv_ref, place_ref,
         c_ref, s1_ref, s2_ref, pool_ref, q_ref, kt_ref, v_ref) = refs
    else:
        (s_ref, mod_ref, gain_ref, w_in_ref, w_kr_ref, kvg_ref, wk_ref, wv_ref, place_ref,
         kt_ref, v_ref) = refs
    u = _adaln(s_ref[...], gain_ref[...], mod_ref, 1).astype(BF16)
    cuts = (POOL_DIM, POOL_DIM + Q_LORA_RANK, POOL_DIM + Q_LORA_RANK + KV_LORA_RANK)
    z = _mm(u, w_in_ref[:, :cuts[2]])
    ckv = (_rms(z[:, cuts[1]:cuts[2]]) * kvg_ref[...]).astype(BF16)
    kr = pltpu.roll(_mm(u, w_kr_ref[...]), QK_NOPE_DIM, axis=1)
    if latent:
        pool_ref[...] = z[:, :cuts[0]]
        cq = (_rms(z[:, cuts[0]:cuts[1]]) * qg_ref[...]).astype(BF16)
        q = jnp.dot(cq, wq_ref[...], preferred_element_type=F32)
        for h in range(MLA_HEADS):
            sl = slice(h * HEAD_PAD, (h + 1) * HEAD_PAD)
            q_ref[:, sl] = _rope(q[:, sl], c_ref, s1_ref, s2_ref).astype(BF16)
        kr = _rope(kr, c_ref, s1_ref, s2_ref)
    kt = lax.dot_general(wk_ref[...], ckv, _NT, preferred_element_type=F32)
    kt = kt + lax.dot_general(place_ref[...], kr.astype(BF16), _NT, preferred_element_type=F32)
    kt_ref[...] = kt.astype(BF16).reshape(kt_ref.shape)
    v_ref[...] = jnp.dot(ckv, wv_ref[...], preferred_element_type=F32).astype(BF16)


def _mix_in(s, mod, gain, w_in, w_kr, kvg, wk, wv, place, tm, latent_args=None):
    n_tok, d = s.shape
    rows_per_mod = n_tok // mod.shape[0]
    assert rows_per_mod % tm == 0
    tiles_per_batch = rows_per_mod // tm
    latent = latent_args is not None
    const = lambda a: pl.BlockSpec(a.shape, lambda i: (0,) * a.ndim)
    tok = lambda w: pl.BlockSpec((tm, w), lambda i: (i, 0))
    in_specs = [tok(d), pl.BlockSpec((1, N_MOD, d), lambda i: (i // tiles_per_batch, 0, 0)),
                pl.BlockSpec((1, d), lambda i: (0, 0)),
                pl.BlockSpec((None,) + w_in.shape[1:], lambda i: (0, 0, 0), pipeline_mode=pl.Buffered(1)),
                const(w_kr)]
    args = [s, mod, gain.reshape(1, d), w_in, w_kr]
    if latent:
        qg, wq, tabs = latent_args
        in_specs += [const(qg), const(wq)]
        args += [qg, wq]
    in_specs += [const(kvg), const(wk), const(wv), const(place)]
    args += [kvg, wk, wv, place]
    if latent:
        kt_spec = pl.BlockSpec((1, QK_PAD_DIM, tm), lambda i: (i // tiles_per_batch, 0, i % tiles_per_batch))
        kt_shape = (mod.shape[0], QK_PAD_DIM, rows_per_mod)
    else:
        kt_spec = pl.BlockSpec((QK_PAD_DIM, tm), lambda i: (0, i))
        kt_shape = (QK_PAD_DIM, n_tok)
    out_specs = [kt_spec, tok(ATTN_V_DIM)]
    out_shape = [jax.ShapeDtypeStruct(kt_shape, BF16), jax.ShapeDtypeStruct((n_tok, ATTN_V_DIM), BF16)]
    if latent:
        rope_spec = pl.BlockSpec((tm, HEAD_PAD), lambda i: (i % tiles_per_batch, 0))
        in_specs += [rope_spec] * 3
        args += list(tabs)
        out_specs = [tok(POOL_DIM), tok(QK_PAD_DIM)] + out_specs
        out_shape = [jax.ShapeDtypeStruct((n_tok, POOL_DIM), F32),
                     jax.ShapeDtypeStruct((n_tok, QK_PAD_DIM), BF16)] + out_shape
    return pl.pallas_call(
        functools.partial(_mix_in_kernel, latent=latent),
        grid=(n_tok // tm,),
        in_specs=in_specs,
        out_specs=out_specs,
        out_shape=out_shape,
        compiler_params=_params(("arbitrary",)),
        name="mix_in_latent" if latent else "mix_in_context",
    )(*args)


def _attn_kernel(q_ref, kth_ref, ktg_ref, vh_ref, vg_ref, o_ref):
    c = ATTN_SCALE * math.log2(math.e)
    lane_head = lax.broadcasted_iota(jnp.int32, (1, PV_WIDTH), 1) // V_HEAD_DIM
    for sub in range(q_ref.shape[1] // ATTN_SUB_Q):
        rows = slice(sub * ATTN_SUB_Q, (sub + 1) * ATTN_SUB_Q)
        for grp in range(MLA_HEADS // PV_HEADS):
            vcols = slice(grp * PV_WIDTH, (grp + 1) * PV_WIDTH)
            acc = None
            for hh in range(PV_HEADS):
                h = grp * PV_HEADS + hh
                qk = slice(h * HEAD_PAD, (h + 1) * HEAD_PAD)
                q = q_ref[0, rows, qk]
                s_h = jnp.dot(q, kth_ref[0, qk, :], preferred_element_type=F32)
                s_g = jnp.dot(q, ktg_ref[qk, :], preferred_element_type=F32)
                m = jnp.maximum(jnp.max(s_h, axis=-1, keepdims=True),
                                jnp.max(s_g, axis=-1, keepdims=True))
                e_h = jnp.exp2((s_h - m) * c)
                e_g = jnp.exp2((s_g - m) * c)
                denom = jnp.sum(e_h, axis=-1, keepdims=True) + jnp.sum(e_g, axis=-1, keepdims=True)
                res = jnp.dot(e_h.astype(BF16), vh_ref[0, :, vcols], preferred_element_type=F32)
                res = res + jnp.dot(e_g.astype(BF16), vg_ref[0, :, vcols], preferred_element_type=F32)
                term = jnp.where(lane_head == hh, res * (1.0 / denom), 0.0)
                acc = term if acc is None else acc + term
            o_ref[0, rows, vcols] = acc.astype(BF16)


def _attention(q, kt_h, kt_g, v_h, v_g, tq):
    b, l, _ = q.shape
    t_g = v_g.shape[1]
    return pl.pallas_call(
        _attn_kernel,
        grid=(b, l // tq),
        in_specs=[
            pl.BlockSpec((1, tq, QK_PAD_DIM), lambda i, j: (i, j, 0)),
            pl.BlockSpec((1, QK_PAD_DIM, l), lambda i, j: (i, 0, 0)),
            pl.BlockSpec((QK_PAD_DIM, t_g), lambda i, j: (0, i)),
            pl.BlockSpec((1, l, ATTN_V_DIM), lambda i, j: (i, 0, 0)),
            pl.BlockSpec((1, t_g, ATTN_V_DIM), lambda i, j: (i, 0, 0)),
        ],
        out_specs=pl.BlockSpec((1, tq, ATTN_V_DIM), lambda i, j: (i, j, 0)),
        out_shape=jax.ShapeDtypeStruct((b, l, ATTN_V_DIM), BF16),
        compiler_params=_params(("arbitrary", "arbitrary")),
        name="latent_attention",
    )(q, kt_h, kt_g, v_h, v_g)


def _mix_out_kernel(h_hbm, mod_ref, pool_ref, poolp_ref, pooln_ref, attn_ref, pw_ref, ps_ref, wo_ref,
                    o_ref, win_scr, y_scr, h_ring, h_sem, *, tm, seq):
    tiles = seq // tm
    step = pl.program_id(0) * tiles + pl.program_id(1)
    n_steps = pl.num_programs(0) * tiles

    def h_copy(s):
        slot = s % H_RING_SLOTS
        return pltpu.make_async_copy(h_hbm.at[s // tiles, pl.ds((s % tiles) * tm, tm), :],
                                     h_ring.at[slot], h_sem.at[slot])

    @pl.when(step == 0)
    def _():
        for s in range(H_RING_SLOTS - 1):
            h_copy(s).start()

    @pl.when(step + (H_RING_SLOTS - 1) < n_steps)
    def _():
        h_copy(step + (H_RING_SLOTS - 1)).start()

    t0 = pl.program_id(1) * tm
    win_scr[0:HALO, :] = jnp.where(t0 > 0, poolp_ref[0], 0.0)
    win_scr[HALO:HALO + tm, :] = pool_ref[0]
    win_scr[HALO + tm:, :] = jnp.where(t0 + tm < seq, pooln_ref[0], 0.0)
    t = (t0 + lax.broadcasted_iota(jnp.int32, (tm, 1), 0)).astype(F32)
    n = tm + 2 * HALO
    for g, w in enumerate(POOL_WINDOWS):
        lanes = slice(g * POOL_GROUP_DIM, (g + 1) * POOL_GROUP_DIM)
        fwd = win_scr[:, lanes]
        span = 1
        while span < w:
            fwd = fwd + pltpu.roll(fwd, n - span, axis=0)
            span *= 2
        total = pltpu.roll(fwd, w // 2, axis=0)[HALO:HALO + tm]
        cnt = jnp.minimum(t, float(w // 2)) + jnp.minimum(float(seq - 1) - t, float(w - w // 2 - 1)) + 1.0
        p = total / cnt - win_scr[HALO:HALO + tm, lanes]
        y = _mm(p.astype(BF16), pw_ref[g])
        y_scr[:, lanes] = (y * ps_ref[:, lanes]).astype(BF16)
    out = _mm(y_scr[...], wo_ref[:POOL_DIM, :]) + _mm(attn_ref[0], wo_ref[POOL_DIM:, :])
    gate = mod_ref[0, 3 * 1 + 2:3 * 1 + 3, :]
    h_copy(step).wait()
    o_ref[0] = h_ring[step % H_RING_SLOTS] + gate * out


def _mix_out(h, mod, pool, attn, pool_w, pool_scale, w_out, tm):
    b, l, d = h.shape
    hb = tm // HALO
    const = lambda a: pl.BlockSpec((None,) + a.shape[1:], lambda i, j: (0,) * a.ndim,
                                   pipeline_mode=pl.Buffered(1))
    return pl.pallas_call(
        functools.partial(_mix_out_kernel, tm=tm, seq=l),
        grid=(b, l // tm),
        in_specs=[
            pl.BlockSpec(memory_space=pl.ANY),
            pl.BlockSpec((1, N_MOD, d), lambda i, j: (i, 0, 0)),
            pl.BlockSpec((1, tm, POOL_DIM), lambda i, j: (i, j, 0)),
            pl.BlockSpec((1, HALO, POOL_DIM), lambda i, j: (i, jnp.maximum(j * hb - 1, 0), 0)),
            pl.BlockSpec((1, HALO, POOL_DIM), lambda i, j: (i, jnp.minimum((j + 1) * hb, l // HALO - 1), 0)),
            pl.BlockSpec((1, tm, ATTN_V_DIM), lambda i, j: (i, j, 0)),
            const(pool_w), const(pool_scale), const(w_out),
        ],
        out_specs=pl.BlockSpec((1, tm, d), lambda i, j: (i, j, 0)),
        out_shape=jax.ShapeDtypeStruct((b, l, d), F32),
        scratch_shapes=[pltpu.VMEM((tm + 2 * HALO, POOL_DIM), F32), pltpu.VMEM((tm, POOL_DIM), BF16),
                        pltpu.VMEM((H_RING_SLOTS, tm, d), F32), pltpu.SemaphoreType.DMA((H_RING_SLOTS,))],
        compiler_params=_params(("arbitrary", "arbitrary")),
        name="mix_out",
    )(h, mod, pool, pool, pool, attn, pool_w, pool_scale, w_out)


def _conv_kernel(h_ref, hp_ref, hn_ref, mod_ref, gain_ref, w_in_ref, cw_ref, w_out_ref, o_ref,
                 u_scr, z_scr, *, tm, seq):
    t0 = pl.program_id(1) * tm
    gain = gain_ref[...]
    u_scr[0:HALO, :] = _adaln(hp_ref[0], gain, mod_ref, 1).astype(BF16)
    u_scr[HALO:HALO + tm, :] = _adaln(h_ref[0], gain, mod_ref, 1).astype(BF16)
    u_scr[HALO + tm:, :] = _adaln(hn_ref[0], gain, mod_ref, 1).astype(BF16)
    d = h_ref.shape[-1]
    cv = _mm(u_scr[...], w_in_ref[:, d:])
    t = t0 - HALO + lax.broadcasted_iota(jnp.int32, (tm + 2 * HALO, 1), 0)
    inside = jnp.logical_and(t >= 0, t < seq)
    z_scr[...] = jnp.where(inside, cv[:, :d] * cv[:, d:], 0.0)
    y = (cw_ref[0:1, :] * z_scr[HALO - 1:HALO - 1 + tm, :]
         + cw_ref[1:2, :] * z_scr[HALO:HALO + tm, :]
         + cw_ref[2:3, :] * z_scr[HALO + 1:HALO + 1 + tm, :])
    bg = _mm(u_scr[HALO:HALO + tm, :], w_in_ref[:, :d])
    out = _mm((bg * y).astype(BF16), w_out_ref[...])
    gate = mod_ref[0, 3 * 1 + 2:3 * 1 + 3, :]
    o_ref[0] = h_ref[0] + gate * out


def _conv_mixer(h, mod, gain, w_in, conv_w, w_out, tm):
    b, l, d = h.shape
    hb = tm // HALO
    const = lambda a: pl.BlockSpec((None,) + a.shape[1:], lambda i, j: (0,) * a.ndim,
                                   pipeline_mode=pl.Buffered(1))
    return pl.pallas_call(
        functools.partial(_conv_kernel, tm=tm, seq=l),
        grid=(b, l // tm),
        in_specs=[
            pl.BlockSpec((1, tm, d), lambda i, j: (i, j, 0)),
            pl.BlockSpec((1, HALO, d), lambda i, j: (i, jnp.maximum(j * hb - 1, 0), 0)),
            pl.BlockSpec((1, HALO, d), lambda i, j: (i, jnp.minimum((j + 1) * hb, l // HALO - 1), 0)),
            pl.BlockSpec((1, N_MOD, d), lambda i, j: (i, 0, 0)),
            pl.BlockSpec((1, d), lambda i, j: (0, 0)),
            const(w_in), const(conv_w), const(w_out),
        ],
        out_specs=pl.BlockSpec((1, tm, d), lambda i, j: (i, j, 0)),
        out_shape=jax.ShapeDtypeStruct((b, l, d), F32),
        scratch_shapes=[pltpu.VMEM((tm + 2 * HALO, d), BF16), pltpu.VMEM((tm + 2 * HALO, d), F32)],
        compiler_params=_params(("arbitrary", "arbitrary")),
        name="conv_mixer",
    )(h, h, h, mod, gain.reshape(1, d), w_in, conv_w, w_out)


def _head_pad(w, per_head, start, width):
    r = w.shape[0]
    w = w.reshape(r, MLA_HEADS, per_head)[:, :, start:start + width]
    return jnp.pad(w, ((0, 0), (0, 0), (0, HEAD_PAD - width))).reshape(r, QK_PAD_DIM)


def _rope_tables(length):
    pos = np.arange(length)
    row = (pos // GRID_W).astype(np.float32)
    col = (pos % GRID_W).astype(np.float32)
    half = ROPE_AXIS_DIM // 2
    freqs = np.power(np.float32(ROPE_THETA),
                     -np.arange(0, ROPE_AXIS_DIM, 2, dtype=np.float32) / np.float32(ROPE_AXIS_DIM))
    lane = np.arange(HEAD_PAD)
    o = lane - QK_NOPE_DIM
    rotary = np.logical_and(o >= 0, o < QK_ROPE_DIM)
    o = np.clip(o, 0, QK_ROPE_DIM - 1)
    ang = np.where((o // ROPE_AXIS_DIM == 0)[None, :], row[:, None], col[:, None]) * freqs[o % half][None, :]
    ang = ang.astype(np.float32)
    first = (o % ROPE_AXIS_DIM) < half
    cos = np.where(rotary[None, :], np.cos(ang), 1.0).astype(np.float32)
    sin = np.where(rotary[None, :], np.sin(ang), 0.0).astype(np.float32)
    return (jnp.asarray(cos), jnp.asarray(np.where(first[None, :], -sin, 0.0).astype(np.float32)),
            jnp.asarray(np.where(first[None, :], 0.0, sin).astype(np.float32)))


def kernel(x, c, ctx, c_ctx, norm_g, w_mod, b_mod, ffn_w_gate, ffn_w_up, ffn_w_down, ab_w_in, pool_w,
           pool_scale, q_norm_g, w_uq, kv_norm_g, w_ukv, ab_w_out, conv_w_in, conv_w, conv_w_out,
           final_norm_g):
    b, l, d = x.shape
    t_g = ctx.shape[1]
    tm_h, tq, tm_mix = FFN_TILE_ROWS, ATTN_TILE_ROWS, MIX_TILE_ROWS
    assert w_mod.shape[0] == 2 and ab_w_in.shape[0] == 1 and conv_w_in.shape[0] == 1
    assert l % tm_h == 0 and l % tq == 0 and l % tm_mix == 0 and t_g % LANES == 0 and l % GRID_W == 0

    cond = jnp.zeros((MOD_ROWS, d), F32).at[:b].set(c).at[b].set(c_ctx)
    n_first = 3 * d
    m_first = _modulation(cond, w_mod, b_mod, n_first).reshape(MOD_ROWS, 3, d)

    ffn_w = (ffn_w_gate, ffn_w_up, ffn_w_down)

    h = x.reshape(b * l, d)
    g = ctx.reshape(b * t_g, d)

    h, m_rest = _ffn_half(h, m_first[:b], norm_g[0, 0], *ffn_w, (0, 0), k=0, tm=tm_h,
                          side_mod=(cond, w_mod, b_mod, n_first))
    g = _ffn_half(g, m_first[b:b + 1], norm_g[0, 0], *ffn_w, (0, 0), k=0, tm=tm_h)
    m_rest = m_rest.reshape(MOD_ROWS, 2 * N_MOD - 3, d)
    m0 = jnp.concatenate([m_first, m_rest[:, :N_MOD - 3]], axis=1)
    mod_h = [m0[:b], m_rest[:b, N_MOD - 3:]]

    kr_start = POOL_DIM + Q_LORA_RANK + KV_LORA_RANK
    w_kr = jnp.pad(ab_w_in[0, :, kr_start:], ((0, 0), (0, HEAD_PAD - QK_ROPE_DIM)))
    wq = _head_pad(w_uq[0], QK_HEAD_DIM, 0, QK_HEAD_DIM).astype(BF16)
    wk = _head_pad(w_ukv[0], QK_NOPE_DIM + V_HEAD_DIM, 0, QK_NOPE_DIM).T.astype(BF16)
    wv = w_ukv[0].reshape(KV_LORA_RANK, MLA_HEADS, QK_NOPE_DIM + V_HEAD_DIM)[:, :, QK_NOPE_DIM:]
    wv = wv.reshape(KV_LORA_RANK, ATTN_V_DIM).astype(BF16)
    lane = jnp.arange(HEAD_PAD)
    rotary = jnp.logical_and(lane >= QK_NOPE_DIM, lane < QK_HEAD_DIM)
    place = jnp.tile(jnp.where(rotary[:, None], jnp.eye(HEAD_PAD, dtype=F32), 0.0), (MLA_HEADS, 1)).astype(BF16)
    qg = q_norm_g[0].reshape(1, Q_LORA_RANK)
    kvg = kv_norm_g[0].reshape(1, KV_LORA_RANK)
    tabs = _rope_tables(l)

    pool, q, k_h, v_h = _mix_in(h, mod_h[0], norm_g[0, 1], ab_w_in, w_kr, kvg, wk, wv, place, tm_mix,
                                latent_args=(qg, wq, tabs))
    k_g, v_g = _mix_in(g, m0[b:b + 1], norm_g[0, 1], ab_w_in, w_kr, kvg, wk, wv, place, tm_mix)
    attn = _attention(q.reshape(b, l, -1), k_h, k_g, v_h.reshape(b, l, -1), v_g.reshape(b, t_g, -1), tq)
    h = _mix_out(h.reshape(b, l, d), mod_h[0], pool.reshape(b, l, -1), attn, pool_w,
                 pool_scale.reshape(-1, 1, POOL_DIM), ab_w_out, tm_mix)
    h = _ffn_half(h.reshape(b * l, d), mod_h[0], norm_g[0, 2], *ffn_w, (0, 1), k=2, tm=tm_h)

    h = _ffn_half(h, mod_h[1], norm_g[1, 0], *ffn_w, (1, 0), k=0, tm=tm_h)
    h = _conv_mixer(h.reshape(b, l, d), mod_h[1], norm_g[1, 1], conv_w_in, conv_w, conv_w_out, tm_mix)
    h = _ffn_half(h.reshape(b * l, d), mod_h[1], norm_g[1, 2], *ffn_w, (1, 1), k=2, tm=tm_h,
                  final_gain=final_norm_g)
    return h.reshape(b, l, d)
```

```python
import functools
import math

import jax
import jax.numpy as jnp
import numpy as np
from jax import lax
from jax.experimental import pallas as pl
from jax.experimental.pallas import tpu as pltpu

D_MODEL = 1024
GRID_W = 64
RMS_EPS = 1e-6
N_MOD = 9
D_FF = 2816
POOL_WINDOWS = (2, 4, 8, 16)
POOL_DIM = D_MODEL // 2
POOL_GROUP_DIM = POOL_DIM // len(POOL_WINDOWS)
MLA_HEADS = D_MODEL // 128
QK_NOPE_DIM = 64
QK_ROPE_DIM = 32
QK_HEAD_DIM = QK_NOPE_DIM + QK_ROPE_DIM
V_HEAD_DIM = 64
Q_LORA_RANK = 768
KV_LORA_RANK = 256
ROPE_AXIS_DIM = QK_ROPE_DIM // 2
ROPE_THETA = 10000.0
ATTN_SCALE = 1.0 / math.sqrt(QK_HEAD_DIM)
ATTN_V_DIM = MLA_HEADS * V_HEAD_DIM

LANES = 128
SUBLANES = 8
HEAD_PAD = LANES
QK_PAD_DIM = MLA_HEADS * HEAD_PAD
MOD_ROWS = 16
MOD_TILE_N = 1536
SIDE_MOD_TILE_N = 512
FF_CHUNK = 256
N_FF_CHUNKS = D_FF // FF_CHUNK
FF_HEAD_ROWS = 256
FF_TAIL_ROWS = 256
HALO = SUBLANES
PV_WIDTH = 256
PV_HEADS = PV_WIDTH // V_HEAD_DIM
ATTN_SUB_Q = 256
V7X_VMEM_BYTES = 64 * 1024 * 1024
VMEM_LIMIT = V7X_VMEM_BYTES * 7 // 8
FFN_TILE_ROWS = 512
ATTN_TILE_ROWS = 512
MIX_TILE_ROWS = 1024
H_RING_SLOTS = 3

BF16 = jnp.bfloat16
F32 = jnp.float32


def _sigmoid(x):
    return 1.0 / (1.0 + jnp.exp(-x))


def _rms(x):
    return x * lax.rsqrt(jnp.mean(x * x, axis=-1, keepdims=True) + RMS_EPS)


def _adaln(x, gain, mod_ref, k):
    shift = mod_ref[0, 3 * k:3 * k + 1, :]
    scale = mod_ref[0, 3 * k + 1:3 * k + 2, :]
    return _rms(x) * gain * (1.0 + scale) + shift


_NT = (((1,), (1,)), ((), ()))


def _mm(a, w):
    return lax.dot_general(a, w, (((1,), (0,)), ((), ())), preferred_element_type=F32)


def _params(semantics):
    return pltpu.CompilerParams(dimension_semantics=semantics, vmem_limit_bytes=VMEM_LIMIT)


def _mod_kernel(cond_ref, w_ref, b_ref, o_ref):
    cond = cond_ref[...]
    o_ref[...] = _mm((cond * _sigmoid(cond)).astype(BF16), w_ref[...]) + b_ref[...]


def _modulation(cond, w_mod, b_mod, n_cols):
    depth, d, n = w_mod.shape
    return pl.pallas_call(
        _mod_kernel,
        grid=(n_cols // MOD_TILE_N,),
        in_specs=[
            pl.BlockSpec((MOD_ROWS, d), lambda j: (0, 0)),
            pl.BlockSpec((None, d, MOD_TILE_N), lambda j: (0, 0, j)),
            pl.BlockSpec((None, 1, MOD_TILE_N), lambda j: (0, 0, j)),
        ],
        out_specs=pl.BlockSpec((MOD_ROWS, MOD_TILE_N), lambda j: (0, j)),
        out_shape=jax.ShapeDtypeStruct((MOD_ROWS, n_cols), F32),
        compiler_params=_params(("arbitrary",)),
        name="modulation",
    )(cond, w_mod, b_mod.reshape(depth, 1, n))


def _ffn_kernel(*refs, k, final, sel, n_side):
    refs = list(refs)
    s_ref, mod_ref, gain_ref, wg_hbm, wu_hbm, wd_hbm = refs[:6]
    u_scr, a_scr, wg_ref, wu_ref, wd_ref, sem = refs[-6:]
    extra = refs[6:-6]
    fg_ref = extra.pop(0) if final else None
    if n_side:
        cond_ref, wm_ref, bm_ref, o_ref, om_ref = extra
    else:
        (o_ref,) = extra
    tm = s_ref.shape[0]

    def chunk_copies(f):
        cols = slice(f * FF_CHUNK, (f + 1) * FF_CHUNK)
        return (pltpu.make_async_copy(wg_hbm.at[sel[0], sel[1], :, cols], wg_ref.at[:, cols], sem.at[2 * f]),
                pltpu.make_async_copy(wu_hbm.at[sel[0], sel[1], :, cols], wu_ref.at[:, cols], sem.at[2 * f + 1]))

    def down_copy():
        return pltpu.make_async_copy(wd_hbm.at[sel[0], sel[1]], wd_ref, sem.at[2 * N_FF_CHUNKS])

    def hidden(rows, f):
        cols = slice(f * FF_CHUNK, (f + 1) * FF_CHUNK)
        g = _mm(u_scr[rows, :], wg_ref[:, cols])
        up = _mm(u_scr[rows, :], wu_ref[:, cols])
        a_scr[rows, cols] = (g * _sigmoid(g) * up).astype(BF16)

    def body(first_step):
        if n_side:
            cond = cond_ref[...]
            om_ref[...] = _mm((cond * _sigmoid(cond)).astype(BF16), wm_ref[...]) + bm_ref[...]
        for p in range(tm // FF_HEAD_ROWS):
            rows = slice(p * FF_HEAD_ROWS, (p + 1) * FF_HEAD_ROWS)
            u_scr[rows, :] = _adaln(s_ref[rows, :], gain_ref[...], mod_ref, k).astype(BF16)
            if first_step and p == 0:
                for f in range(N_FF_CHUNKS):
                    for cp in chunk_copies(f):
                        cp.start()
                down_copy().start()
                for cp in chunk_copies(0):
                    cp.wait()
            hidden(rows, 0)
        for f in range(1, N_FF_CHUNKS):
            if first_step:
                for cp in chunk_copies(f):
                    cp.wait()
            hidden(slice(None), f)
        if first_step:
            down_copy().wait()
        gate = mod_ref[0, 3 * k + 2:3 * k + 3, :]
        for p in range(tm // FF_TAIL_ROWS):
            rows = slice(p * FF_TAIL_ROWS, (p + 1) * FF_TAIL_ROWS)
            y = s_ref[rows, :] + 0.5 * gate * _mm(a_scr[rows, :], wd_ref[...])
            if final:
                y = _rms(y) * fg_ref[...]
            o_ref[rows, :] = y

    pl.when(pl.program_id(0) == 0)(functools.partial(body, True))
    pl.when(pl.program_id(0) != 0)(functools.partial(body, False))


def _ffn_half(s, mod, gain, wg, wu, wd, sel, k, tm, final_gain=None, side_mod=None):
    n_tok, d = s.shape
    rows_per_mod = n_tok // mod.shape[0]
    assert rows_per_mod % tm == 0
    final = final_gain is not None
    hbm = pl.BlockSpec(memory_space=pl.ANY)
    in_specs = [
        pl.BlockSpec((tm, d), lambda i: (i, 0)),
        pl.BlockSpec((1, mod.shape[1], d), lambda i: (i * tm // rows_per_mod, 0, 0)),
        pl.BlockSpec((1, d), lambda i: (0, 0)),
        hbm, hbm, hbm,
    ]
    args = [s, mod, gain.reshape(1, d), wg, wu, wd]
    if final:
        in_specs.append(pl.BlockSpec((1, d), lambda i: (0, 0)))
        args.append(final_gain.reshape(1, d))
    out_specs = pl.BlockSpec((tm, d), lambda i: (i, 0))
    out_shape = jax.ShapeDtypeStruct((n_tok, d), F32)
    n_side = 0
    if side_mod is not None:
        cond, w_mod, b_mod, first_col = side_mod
        n_mod = w_mod.shape[-1]
        c0 = first_col // SIDE_MOD_TILE_N
        n_first = n_mod // SIDE_MOD_TILE_N - c0
        n_side = n_first + n_mod // SIDE_MOD_TILE_N
        assert n_side <= n_tok // tm and first_col % SIDE_MOD_TILE_N == 0

        def src(i):
            j = jnp.minimum(i, n_side - 1)
            return jnp.where(j < n_first, 0, 1), 0, jnp.where(j < n_first, c0 + j, j - n_first)

        in_specs += [pl.BlockSpec(cond.shape, lambda i: (0, 0)),
                     pl.BlockSpec((None, d, SIDE_MOD_TILE_N), src),
                     pl.BlockSpec((None, 1, SIDE_MOD_TILE_N), src)]
        args += [cond, w_mod, b_mod.reshape(b_mod.shape[0], 1, n_mod)]
        out_specs = [out_specs, pl.BlockSpec((cond.shape[0], SIDE_MOD_TILE_N),
                                             lambda i: (0, jnp.minimum(i, n_side - 1)))]
        out_shape = [out_shape, jax.ShapeDtypeStruct((cond.shape[0], n_side * SIDE_MOD_TILE_N), F32)]
    return pl.pallas_call(
        functools.partial(_ffn_kernel, k=k, final=final, sel=sel, n_side=n_side),
        grid=(n_tok // tm,),
        in_specs=in_specs,
        out_specs=out_specs,
        out_shape=out_shape,
        scratch_shapes=[pltpu.VMEM((tm, d), BF16), pltpu.VMEM((tm, D_FF), BF16),
                        pltpu.VMEM(wg.shape[2:], F32), pltpu.VMEM(wu.shape[2:], F32),
                        pltpu.VMEM(wd.shape[2:], F32),
                        pltpu.SemaphoreType.DMA((2 * N_FF_CHUNKS + 1,))],
        compiler_params=_params(("arbitrary",)),
        name="ffn_half",
    )(*args)


def _rope(z, c_ref, s1_ref, s2_ref):
    fwd = pltpu.roll(z, HEAD_PAD - ROPE_AXIS_DIM // 2, axis=1)
    bwd = pltpu.roll(z, ROPE_AXIS_DIM // 2, axis=1)
    return z * c_ref[...] + fwd * s1_ref[...] + bwd * s2_ref[...]


def _mix_in_kernel(*refs, latent):
    if latent:
        (s_ref, mod_ref, gain_ref, w_in_ref, w_kr_ref, qg_ref, wq_ref, kvg_ref, wk_ref, wv_ref, place_ref,
         c_ref, s1_ref, s2_ref, pool_ref, q_ref, kt_ref, v_ref) = refs
    else:
        (s_ref, mod_ref, gain_ref, w_in_ref, w_kr_ref, kvg_ref, wk_ref, wv_ref, place_ref,
         kt_ref, v_ref) = refs
    u = _adaln(s_ref[...], gain_ref[...], mod_ref, 1).astype(BF16)
    cuts = (POOL_DIM, POOL_DIM + Q_LORA_RANK, POOL_DIM + Q_LORA_RANK + KV_LORA_RANK)
    z = _mm(u, w_in_ref[:, :cuts[2]])
    ckv = (_rms(z[:, cuts[1]:cuts[2]]) * kvg_ref[...]).astype(BF16)
    kr = pltpu.roll(_mm(u, w_kr_ref[...]), QK_NOPE_DIM, axis=1)
    if latent:
        pool_ref[...] = z[:, :cuts[0]]
        cq = (_rms(z[:, cuts[0]:cuts[1]]) * qg_ref[...]).astype(BF16)
        q = jnp.dot(cq, wq_ref[...], preferred_element_type=F32)
        for h in range(MLA_HEADS):
            sl = slice(h * HEAD_PAD, (h + 1) * HEAD_PAD)
            q_ref[:, sl] = _rope(q[:, sl], c_ref, s1_ref, s2_ref).astype(BF16)
        kr = _rope(kr, c_ref, s1_ref, s2_ref)
    kt = lax.dot_general(wk_ref[...], ckv, _NT, preferred_element_type=F32)
    kt = kt + lax.dot_general(place_ref[...], kr.astype(BF16), _NT, preferred_element_type=F32)
    kt_ref[...] = kt.astype(BF16).reshape(kt_ref.shape)
    v_ref[...] = jnp.dot(ckv, wv_ref[...], preferred_element_type=F32).astype(BF16)


def _mix_in(s, mod, gain, w_in, w_kr, kvg, wk, wv, place, tm, latent_args=None):
    n_tok, d = s.shape
    rows_per_mod = n_tok // mod.shape[0]
    assert rows_per_mod % tm == 0
    tiles_per_batch = rows_per_mod // tm
    latent = latent_args is not None
    const = lambda a: pl.BlockSpec(a.shape, lambda i: (0,) * a.ndim)
    tok = lambda w: pl.BlockSpec((tm, w), lambda i: (i, 0))
    in_specs = [tok(d), pl.BlockSpec((1, N_MOD, d), lambda i: (i // tiles_per_batch, 0, 0)),
                pl.BlockSpec((1, d), lambda i: (0, 0)),
                pl.BlockSpec((None,) + w_in.shape[1:], lambda i: (0, 0, 0), pipeline_mode=pl.Buffered(1)),
                const(w_kr)]
    args = [s, mod, gain.reshape(1, d), w_in, w_kr]
    if latent:
        qg, wq, tabs = latent_args
        in_specs += [const(qg), const(wq)]
        args += [qg, wq]
    in_specs += [const(kvg), const(wk), const(wv), const(place)]
    args += [kvg, wk, wv, place]
    if latent:
        kt_spec = pl.BlockSpec((1, QK_PAD_DIM, tm), lambda i: (i // tiles_per_batch, 0, i % tiles_per_batch))
        kt_shape = (mod.shape[0], QK_PAD_DIM, rows_per_mod)
    else:
        kt_spec = pl.BlockSpec((QK_PAD_DIM, tm), lambda i: (0, i))
        kt_shape = (QK_PAD_DIM, n_tok)
    out_specs = [kt_spec, tok(ATTN_V_DIM)]
    out_shape = [jax.ShapeDtypeStruct(kt_shape, BF16), jax.ShapeDtypeStruct((n_tok, ATTN_V_DIM), BF16)]
    if latent:
        rope_spec = pl.BlockSpec((tm, HEAD_PAD), lambda i: (i % tiles_per_batch, 0))
        in_specs += [rope_spec] * 3
        args += list(tabs)
        out_specs = [tok(POOL_DIM), tok(QK_PAD_DIM)] + out_specs
        out_shape = [jax.ShapeDtypeStruct((n_tok, POOL_DIM), F32),
                     jax.ShapeDtypeStruct((n_tok, QK_PAD_DIM), BF16)] + out_shape
    return pl.pallas_call(
        functools.partial(_mix_in_kernel, latent=latent),
        grid=(n_tok // tm,),
        in_specs=in_specs,
        out_specs=out_specs,
        out_shape=out_shape,
        compiler_params=_params(("arbitrary",)),
        name="mix_in_latent" if latent else "mix_in_context",
    )(*args)


def _attn_kernel(q_ref, kth_ref, ktg_ref, vh_ref, vg_ref, o_ref):
    c = ATTN_SCALE * math.log2(math.e)
    lane_head = lax.broadcasted_iota(jnp.int32, (1, PV_WIDTH), 1) // V_HEAD_DIM
    for sub in range(q_ref.shape[1] // ATTN_SUB_Q):
        rows = slice(sub * ATTN_SUB_Q, (sub + 1) * ATTN_SUB_Q)
        for grp in range(MLA_HEADS // PV_HEADS):
            vcols = slice(grp * PV_WIDTH, (grp + 1) * PV_WIDTH)
            acc = None
            for hh in range(PV_HEADS):
                h = grp * PV_HEADS + hh
                qk = slice(h * HEAD_PAD, (h + 1) * HEAD_PAD)
                q = q_ref[0, rows, qk]
                s_h = jnp.dot(q, kth_ref[0, qk, :], preferred_element_type=F32)
                s_g = jnp.dot(q, ktg_ref[qk, :], preferred_element_type=F32)
                m = jnp.maximum(jnp.max(s_h, axis=-1, keepdims=True),
                                jnp.max(s_g, axis=-1, keepdims=True))
                e_h = jnp.exp2((s_h - m) * c)
                e_g = jnp.exp2((s_g - m) * c)
                denom = jnp.sum(e_h, axis=-1, keepdims=True) + jnp.sum(e_g, axis=-1, keepdims=True)
                res = jnp.dot(e_h.astype(BF16), vh_ref[0, :, vcols], preferred_element_type=F32)
                res = res + jnp.dot(e_g.astype(BF16), vg_ref[0, :, vcols], preferred_element_type=F32)
                term = jnp.where(lane_head == hh, res * (1.0 / denom), 0.0)
                acc = term if acc is None else acc + term
            o_ref[0, rows, vcols] = acc.astype(BF16)


def _attention(q, kt_h, kt_g, v_h, v_g, tq):
    b, l, _ = q.shape
    t_g = v_g.shape[1]
    return pl.pallas_call(
        _attn_kernel,
        grid=(b, l // tq),
        in_specs=[
            pl.BlockSpec((1, tq, QK_PAD_DIM), lambda i, j: (i, j, 0)),
            pl.BlockSpec((1, QK_PAD_DIM, l), lambda i, j: (i, 0, 0)),
            pl.BlockSpec((QK_PAD_DIM, t_g), lambda i, j: (0, i)),
            pl.BlockSpec((1, l, ATTN_V_DIM), lambda i, j: (i, 0, 0)),
            pl.BlockSpec((1, t_g, ATTN_V_DIM), lambda i, j: (i, 0, 0)),
        ],
        out_specs=pl.BlockSpec((1, tq, ATTN_V_DIM), lambda i, j: (i, j, 0)),
        out_shape=jax.ShapeDtypeStruct((b, l, ATTN_V_DIM), BF16),
        compiler_params=_params(("arbitrary", "arbitrary")),
        name="latent_attention",
    )(q, kt_h, kt_g, v_h, v_g)


def _mix_out_kernel(h_hbm, mod_ref, pool_ref, poolp_ref, pooln_ref, attn_ref, pw_ref, ps_ref, wo_ref,
                    o_ref, win_scr, y_scr, h_ring, h_sem, *, tm, seq):
    tiles = seq // tm
    step = pl.program_id(0) * tiles + pl.program_id(1)
    n_steps = pl.num_programs(0) * tiles

    def h_copy(s):
        slot = s % H_RING_SLOTS
        return pltpu.make_async_copy(h_hbm.at[s // tiles, pl.ds((s % tiles) * tm, tm), :],
                                     h_ring.at[slot], h_sem.at[slot])

    @pl.when(step == 0)
    def _():
        for s in range(H_RING_SLOTS - 1):
            h_copy(s).start()

    @pl.when(step + (H_RING_SLOTS - 1) < n_steps)
    def _():
        h_copy(step + (H_RING_SLOTS - 1)).start()

    t0 = pl.program_id(1) * tm
    win_scr[0:HALO, :] = jnp.where(t0 > 0, poolp_ref[0], 0.0)
    win_scr[HALO:HALO + tm, :] = pool_ref[0]
    win_scr[HALO + tm:, :] = jnp.where(t0 + tm < seq, pooln_ref[0], 0.0)
    t = (t0 + lax.broadcasted_iota(jnp.int32, (tm, 1), 0)).astype(F32)
    n = tm + 2 * HALO
    for g, w in enumerate(POOL_WINDOWS):
        lanes = slice(g * POOL_GROUP_DIM, (g + 1) * POOL_GROUP_DIM)
        fwd = win_scr[:, lanes]
        span = 1
        while span < w:
            fwd = fwd + pltpu.roll(fwd, n - span, axis=0)
            span *= 2
        total = pltpu.roll(fwd, w // 2, axis=0)[HALO:HALO + tm]
        cnt = jnp.minimum(t, float(w // 2)) + jnp.minimum(float(seq - 1) - t, float(w - w // 2 - 1)) + 1.0
        p = total / cnt - win_scr[HALO:HALO + tm, lanes]
        y = _mm(p.astype(BF16), pw_ref[g])
        y_scr[:, lanes] = (y * ps_ref[:, lanes]).astype(BF16)
    out = _mm(y_scr[...], wo_ref[:POOL_DIM, :]) + _mm(attn_ref[0], wo_ref[POOL_DIM:, :])
    gate = mod_ref[0, 3 * 1 + 2:3 * 1 + 3, :]
    h_copy(step).wait()
    o_ref[0] = h_ring[step % H_RING_SLOTS] + gate * out


def _mix_out(h, mod, pool, attn, pool_w, pool_scale, w_out, tm):
    b, l, d = h.shape
    hb = tm // HALO
    const = lambda a: pl.BlockSpec((None,) + a.shape[1:], lambda i, j: (0,) * a.ndim,
                                   pipeline_mode=pl.Buffered(1))
    return pl.pallas_call(
        functools.partial(_mix_out_kernel, tm=tm, seq=l),
        grid=(b, l // tm),
        in_specs=[
            pl.BlockSpec(memory_space=pl.ANY),
            pl.BlockSpec((1, N_MOD, d), lambda i, j: (i, 0, 0)),
            pl.BlockSpec((1, tm, POOL_DIM), lambda i, j: (i, j, 0)),
            pl.BlockSpec((1, HALO, POOL_DIM), lambda i, j: (i, jnp.maximum(j * hb - 1, 0), 0)),
            pl.BlockSpec((1, HALO, POOL_DIM), lambda i, j: (i, jnp.minimum((j + 1) * hb, l // HALO - 1), 0)),
            pl.BlockSpec((1, tm, ATTN_V_DIM), lambda i, j: (i, j, 0)),
            const(pool_w), const(pool_scale), const(w_out),
        ],
        out_specs=pl.BlockSpec((1, tm, d), lambda i, j: (i, j, 0)),
        out_shape=jax.ShapeDtypeStruct((b, l, d), F32),
        scratch_shapes=[pltpu.VMEM((tm + 2 * HALO, POOL_DIM), F32), pltpu.VMEM((tm, POOL_DIM), BF16),
                        pltpu.VMEM((H_RING_SLOTS, tm, d), F32), pltpu.SemaphoreType.DMA((H_RING_SLOTS,))],
        compiler_params=_params(("arbitrary", "arbitrary")),
        name="mix_out",
    )(h, mod, pool, pool, pool, attn, pool_w, pool_scale, w_out)


def _conv_kernel(h_ref, hp_ref, hn_ref, mod_ref, gain_ref, w_in_ref, cw_ref, w_out_ref, o_ref,
                 u_scr, z_scr, *, tm, seq):
    t0 = pl.program_id(1) * tm
    gain = gain_ref[...]
    u_scr[0:HALO, :] = _adaln(hp_ref[0], gain, mod_ref, 1).astype(BF16)
    u_scr[HALO:HALO + tm, :] = _adaln(h_ref[0], gain, mod_ref, 1).astype(BF16)
    u_scr[HALO + tm:, :] = _adaln(hn_ref[0], gain, mod_ref, 1).astype(BF16)
    d = h_ref.shape[-1]
    cv = _mm(u_scr[...], w_in_ref[:, d:])
    t = t0 - HALO + lax.broadcasted_iota(jnp.int32, (tm + 2 * HALO, 1), 0)
    inside = jnp.logical_and(t >= 0, t < seq)
    z_scr[...] = jnp.where(inside, cv[:, :d] * cv[:, d:], 0.0)
    y = (cw_ref[0:1, :] * z_scr[HALO - 1:HALO - 1 + tm, :]
         + cw_ref[1:2, :] * z_scr[HALO:HALO + tm, :]
         + cw_ref[2:3, :] * z_scr[HALO + 1:HALO + 1 + tm, :])
    bg = _mm(u_scr[HALO:HALO + tm, :], w_in_ref[:, :d])
    out = _mm((bg * y).astype(BF16), w_out_ref[...])
    gate = mod_ref[0, 3 * 1 + 2:3 * 1 + 3, :]
    o_ref[0] = h_ref[0] + gate * out


def _conv_mixer(h, mod, gain, w_in, conv_w, w_out, tm):
    b, l, d = h.shape
    hb = tm // HALO
    const = lambda a: pl.BlockSpec((None,) + a.shape[1:], lambda i, j: (0,) * a.ndim,
                                   pipeline_mode=pl.Buffered(1))
    return pl.pallas_call(
        functools.partial(_conv_kernel, tm=tm, seq=l),
        grid=(b, l // tm),
        in_specs=[
            pl.BlockSpec((1, tm, d), lambda i, j: (i, j, 0)),
            pl.BlockSpec((1, HALO, d), lambda i, j: (i, jnp.maximum(j * hb - 1, 0), 0)),
            pl.BlockSpec((1, HALO, d), lambda i, j: (i, jnp.minimum((j + 1) * hb, l // HALO - 1), 0)),
            pl.BlockSpec((1, N_MOD, d), lambda i, j: (i, 0, 0)),
            pl.BlockSpec((1, d), lambda i, j: (0, 0)),
            const(w_in), const(conv_w), const(w_out),
        ],
        out_specs=pl.BlockSpec((1, tm, d), lambda i, j: (i, j, 0)),
        out_shape=jax.ShapeDtypeStruct((b, l, d), F32),
        scratch_shapes=[pltpu.VMEM((tm + 2 * HALO, d), BF16), pltpu.VMEM((tm + 2 * HALO, d), F32)],
        compiler_params=_params(("arbitrary", "arbitrary")),
        name="conv_mixer",
    )(h, h, h, mod, gain.reshape(1, d), w_in, conv_w, w_out)


def _head_pad(w, per_head, start, width):
    r = w.shape[0]
    w = w.reshape(r, MLA_HEADS, per_head)[:, :, start:start + width]
    return jnp.pad(w, ((0, 0), (0, 0), (0, HEAD_PAD - width))).reshape(r, QK_PAD_DIM)


def _rope_tables(length):
    pos = np.arange(length)
    row = (pos // GRID_W).astype(np.float32)
    col = (pos % GRID_W).astype(np.float32)
    half = ROPE_AXIS_DIM // 2
    freqs = np.power(np.float32(ROPE_THETA),
                     -np.arange(0, ROPE_AXIS_DIM, 2, dtype=np.float32) / np.float32(ROPE_AXIS_DIM))
    lane = np.arange(HEAD_PAD)
    o = lane - QK_NOPE_DIM
    rotary = np.logical_and(o >= 0, o < QK_ROPE_DIM)
    o = np.clip(o, 0, QK_ROPE_DIM - 1)
    ang = np.where((o // ROPE_AXIS_DIM == 0)[None, :], row[:, None], col[:, None]) * freqs[o % half][None, :]
    ang = ang.astype(np.float32)
    first = (o % ROPE_AXIS_DIM) < half
    cos = np.where(rotary[None, :], np.cos(ang), 1.0).astype(np.float32)
    sin = np.where(rotary[None, :], np.sin(ang), 0.0).astype(np.float32)
    return (jnp.asarray(cos), jnp.asarray(np.where(first[None, :], -sin, 0.0).astype(np.float32)),
            jnp.asarray(np.where(first[None, :], 0.0, sin).astype(np.float32)))


def kernel(x, c, ctx, c_ctx, norm_g, w_mod, b_mod, ffn_w_gate, ffn_w_up, ffn_w_down, ab_w_in, pool_w,
           pool_scale, q_norm_g, w_uq, kv_norm_g, w_ukv, ab_w_out, conv_w_in, conv_w, conv_w_out,
           final_norm_g):
    b, l, d = x.shape
    t_g = ctx.shape[1]
    tm_h, tq, tm_mix = FFN_TILE_ROWS, ATTN_TILE_ROWS, MIX_TILE_ROWS
    assert w_mod.shape[0] == 2 and ab_w_in.shape[0] == 1 and conv_w_in.shape[0] == 1
    assert l % tm_h == 0 and l % tq == 0 and l % tm_mix == 0 and t_g % LANES == 0 and l % GRID_W == 0

    cond = jnp.zeros((MOD_ROWS, d), F32).at[:b].set(c).at[b].set(c_ctx)
    n_first = 3 * d
    m_first = _modulation(cond, w_mod, b_mod, n_first).reshape(MOD_ROWS, 3, d)

    ffn_w = (ffn_w_gate, ffn_w_up, ffn_w_down)

    h = x.reshape(b * l, d)
    g = ctx.reshape(b * t_g, d)

    h, m_rest = _ffn_half(h, m_first[:b], norm_g[0, 0], *ffn_w, (0, 0), k=0, tm=tm_h,
                          side_mod=(cond, w_mod, b_mod, n_first))
    g = _ffn_half(g, m_first[b:b + 1], norm_g[0, 0], *ffn_w, (0, 0), k=0, tm=tm_h)
    m_rest = m_rest.reshape(MOD_ROWS, 2 * N_MOD - 3, d)
    m0 = jnp.concatenate([m_first, m_rest[:, :N_MOD - 3]], axis=1)
    mod_h = [m0[:b], m_rest[:b, N_MOD - 3:]]

    kr_start = POOL_DIM + Q_LORA_RANK + KV_LORA_RANK
    w_kr = jnp.pad(ab_w_in[0, :, kr_start:], ((0, 0), (0, HEAD_PAD - QK_ROPE_DIM)))
    wq = _head_pad(w_uq[0], QK_HEAD_DIM, 0, QK_HEAD_DIM).astype(BF16)
    wk = _head_pad(w_ukv[0], QK_NOPE_DIM + V_HEAD_DIM, 0, QK_NOPE_DIM).T.astype(BF16)
    wv = w_ukv[0].reshape(KV_LORA_RANK, MLA_HEADS, QK_NOPE_DIM + V_HEAD_DIM)[:, :, QK_NOPE_DIM:]
    wv = wv.reshape(KV_LORA_RANK, ATTN_V_DIM).astype(BF16)
    lane = jnp.arange(HEAD_PAD)
    rotary = jnp.logical_and(lane >= QK_NOPE_DIM, lane < QK_HEAD_DIM)
    place = jnp.tile(jnp.where(rotary[:, None], jnp.eye(HEAD_PAD, dtype=F32), 0.0), (MLA_HEADS, 1)).astype(BF16)
    qg = q_norm_g[0].reshape(1, Q_LORA_RANK)
    kvg = kv_norm_g[0].reshape(1, KV_LORA_RANK)
    tabs = _rope_tables(l)

    pool, q, k_h, v_h = _mix_in(h, mod_h[0], norm_g[0, 1], ab_w_in, w_kr, kvg, wk, wv, place, tm_mix,
                                latent_args=(qg, wq, tabs))
    k_g, v_g = _mix_in(g, m0[b:b + 1], norm_g[0, 1], ab_w_in, w_kr, kvg, wk, wv, place, tm_mix)
    attn = _attention(q.reshape(b, l, -1), k_h, k_g, v_h.reshape(b, l, -1), v_g.reshape(b, t_g, -1), tq)
    h = _mix_out(h.reshape(b, l, d), mod_h[0], pool.reshape(b, l, -1), attn, pool_w,
                 pool_scale.reshape(-1, 1, POOL_DIM), ab_w_out, tm_mix)
    h = _ffn_half(h.reshape(b * l, d), mod_h[0], norm_g[0, 2], *ffn_w, (0, 1), k=2, tm=tm_h)

    h = _ffn_half(h, mod_h[1], norm_g[1, 0], *ffn_w, (1, 0), k=0, tm=tm_h)
    h = _conv_mixer(h.reshape(b, l, d), mod_h[1], norm_g[1, 1], conv_w_in, conv_w, conv_w_out, tm_mix)
    h = _ffn_half(h.reshape(b * l, d), mod_h[1], norm_g[1, 2], *ffn_w, (1, 1), k=2, tm=tm_h,
                  final_gain=final_norm_g)
    return h.reshape(b, l, d)
```

```python
import functools
import math

import jax
import jax.numpy as jnp
import numpy as np
from jax import lax
from jax.experimental import pallas as pl
from jax.experimental.pallas import tpu as pltpu

D_MODEL = 1024
GRID_W = 64
RMS_EPS = 1e-6
N_MOD = 9
D_FF = 2816
POOL_WINDOWS = (2, 4, 8, 16)
POOL_DIM = D_MODEL // 2
POOL_GROUP_DIM = POOL_DIM // len(POOL_WINDOWS)
MLA_HEADS = D_MODEL // 128
QK_NOPE_DIM = 64
QK_ROPE_DIM = 32
QK_HEAD_DIM = QK_NOPE_DIM + QK_ROPE_DIM
V_HEAD_DIM = 64
Q_LORA_RANK = 768
KV_LORA_RANK = 256
ROPE_AXIS_DIM = QK_ROPE_DIM // 2
ROPE_THETA = 10000.0
ATTN_SCALE = 1.0 / math.sqrt(QK_HEAD_DIM)
ATTN_V_DIM = MLA_HEADS * V_HEAD_DIM

LANES = 128
SUBLANES = 8
HEAD_PAD = LANES
QK_PAD_DIM = MLA_HEADS * HEAD_PAD
MOD_ROWS = 16
MOD_TILE_N = 1536
SIDE_MOD_TILE_N = 512
FF_CHUNK = 256
N_FF_CHUNKS = D_FF // FF_CHUNK
FF_HEAD_ROWS = 256
FF_TAIL_ROWS = 256
HALO = SUBLANES
PV_WIDTH = 256
PV_HEADS = PV_WIDTH // V_HEAD_DIM
ATTN_SUB_Q = 128
V7X_VMEM_BYTES = 64 * 1024 * 1024
VMEM_LIMIT = V7X_VMEM_BYTES * 7 // 8
FFN_TILE_ROWS = 512
ATTN_TILE_ROWS = 512
MIX_TILE_ROWS = 1024
H_RING_SLOTS = 3

BF16 = jnp.bfloat16
F32 = jnp.float32


def _sigmoid(x):
    return 1.0 / (1.0 + jnp.exp(-x))


def _rms(x):
    return x * lax.rsqrt(jnp.mean(x * x, axis=-1, keepdims=True) + RMS_EPS)


def _adaln(x, gain, mod_ref, k):
    shift = mod_ref[0, 3 * k:3 * k + 1, :]
    scale = mod_ref[0, 3 * k + 1:3 * k + 2, :]
    return _rms(x) * gain * (1.0 + scale) + shift


_NT = (((1,), (1,)), ((), ()))


def _mm(a, w):
    return lax.dot_general(a, w, (((1,), (0,)), ((), ())), preferred_element_type=F32)


def _params(semantics):
    return pltpu.CompilerParams(dimension_semantics=semantics, vmem_limit_bytes=VMEM_LIMIT)


def _mod_kernel(cond_ref, w_ref, b_ref, o_ref):
    cond = cond_ref[...]
    o_ref[...] = _mm((cond * _sigmoid(cond)).astype(BF16), w_ref[...]) + b_ref[...]


def _modulation(cond, w_mod, b_mod, n_cols):
    depth, d, n = w_mod.shape
    return pl.pallas_call(
        _mod_kernel,
        grid=(n_cols // MOD_TILE_N,),
        in_specs=[
            pl.BlockSpec((MOD_ROWS, d), lambda j: (0, 0)),
            pl.BlockSpec((None, d, MOD_TILE_N), lambda j: (0, 0, j)),
            pl.BlockSpec((None, 1, MOD_TILE_N), lambda j: (0, 0, j)),
        ],
        out_specs=pl.BlockSpec((MOD_ROWS, MOD_TILE_N), lambda j: (0, j)),
        out_shape=jax.ShapeDtypeStruct((MOD_ROWS, n_cols), F32),
        compiler_params=_params(("arbitrary",)),
        name="modulation",
    )(cond, w_mod, b_mod.reshape(depth, 1, n))


def _ffn_kernel(*refs, k, final, sel, n_side):
    refs = list(refs)
    s_ref, mod_ref, gain_ref, wg_hbm, wu_hbm, wd_hbm = refs[:6]
    u_scr, a_scr, wg_ref, wu_ref, wd_ref, sem = refs[-6:]
    extra = refs[6:-6]
    fg_ref = extra.pop(0) if final else None
    if n_side:
        cond_ref, wm_ref, bm_ref, o_ref, om_ref = extra
    else:
        (o_ref,) = extra
    tm = s_ref.shape[0]

    def chunk_copies(f):
        cols = slice(f * FF_CHUNK, (f + 1) * FF_CHUNK)
        return (pltpu.make_async_copy(wg_hbm.at[sel[0], sel[1], :, cols], wg_ref.at[:, cols], sem.at[2 * f]),
                pltpu.make_async_copy(wu_hbm.at[sel[0], sel[1], :, cols], wu_ref.at[:, cols], sem.at[2 * f + 1]))

    def down_copy():
        return pltpu.make_async_copy(wd_hbm.at[sel[0], sel[1]], wd_ref, sem.at[2 * N_FF_CHUNKS])

    def hidden(rows, f):
        cols = slice(f * FF_CHUNK, (f + 1) * FF_CHUNK)
        g = _mm(u_scr[rows, :], wg_ref[:, cols])
        up = _mm(u_scr[rows, :], wu_ref[:, cols])
        a_scr[rows, cols] = (g * _sigmoid(g) * up).astype(BF16)

    def body(first_step):
        if first_step:
            for f in range(N_FF_CHUNKS):
                for cp in chunk_copies(f):
                    cp.start()
            down_copy().start()
        if n_side:
            cond = cond_ref[...]
            om_ref[...] = _mm((cond * _sigmoid(cond)).astype(BF16), wm_ref[...]) + bm_ref[...]
        for p in range(tm // FF_HEAD_ROWS):
            rows = slice(p * FF_HEAD_ROWS, (p + 1) * FF_HEAD_ROWS)
            u_scr[rows, :] = _adaln(s_ref[rows, :], gain_ref[...], mod_ref, k).astype(BF16)
            if first_step and p == 0:
                for cp in chunk_copies(0):
                    cp.wait()
            hidden(rows, 0)
        for f in range(1, N_FF_CHUNKS):
            if first_step:
                for cp in chunk_copies(f):
                    cp.wait()
            hidden(slice(None), f)
        if first_step:
            down_copy().wait()
        gate = mod_ref[0, 3 * k + 2:3 * k + 3, :]
        for p in range(tm // FF_TAIL_ROWS):
            rows = slice(p * FF_TAIL_ROWS, (p + 1) * FF_TAIL_ROWS)
            y = s_ref[rows, :] + 0.5 * gate * _mm(a_scr[rows, :], wd_ref[...])
            if final:
                y = _rms(y) * fg_ref[...]
            o_ref[rows, :] = y

    pl.when(pl.program_id(0) == 0)(functools.partial(body, True))
    pl.when(pl.program_id(0) != 0)(functools.partial(body, False))


def _ffn_half(s, mod, gain, wg, wu, wd, sel, k, tm, final_gain=None, side_mod=None):
    n_tok, d = s.shape
    rows_per_mod = n_tok // mod.shape[0]
    assert rows_per_mod % tm == 0
    final = final_gain is not None
    hbm = pl.BlockSpec(memory_space=pl.ANY)
    in_specs = [
        pl.BlockSpec((tm, d), lambda i: (i, 0)),
        pl.BlockSpec((1, mod.shape[1], d), lambda i: (i * tm // rows_per_mod, 0, 0)),
        pl.BlockSpec((1, d), lambda i: (0, 0)),
        hbm, hbm, hbm,
    ]
    args = [s, mod, gain.reshape(1, d), wg, wu, wd]
    if final:
        in_specs.append(pl.BlockSpec((1, d), lambda i: (0, 0)))
        args.append(final_gain.reshape(1, d))
    out_specs = pl.BlockSpec((tm, d), lambda i: (i, 0))
    out_shape = jax.ShapeDtypeStruct((n_tok, d), F32)
    n_side = 0
    if side_mod is not None:
        cond, w_mod, b_mod, first_col = side_mod
        n_mod = w_mod.shape[-1]
        c0 = first_col // SIDE_MOD_TILE_N
        n_first = n_mod // SIDE_MOD_TILE_N - c0
        n_side = n_first + n_mod // SIDE_MOD_TILE_N
        assert n_side <= n_tok // tm and first_col % SIDE_MOD_TILE_N == 0

        def src(i):
            j = jnp.minimum(i, n_side - 1)
            return jnp.where(j < n_first, 0, 1), 0, jnp.where(j < n_first, c0 + j, j - n_first)

        in_specs += [pl.BlockSpec(cond.shape, lambda i: (0, 0)),
                     pl.BlockSpec((None, d, SIDE_MOD_TILE_N), src),
                     pl.BlockSpec((None, 1, SIDE_MOD_TILE_N), src)]
        args += [cond, w_mod, b_mod.reshape(b_mod.shape[0], 1, n_mod)]
        out_specs = [out_specs, pl.BlockSpec((cond.shape[0], SIDE_MOD_TILE_N),
                                             lambda i: (0, jnp.minimum(i, n_side - 1)))]
        out_shape = [out_shape, jax.ShapeDtypeStruct((cond.shape[0], n_side * SIDE_MOD_TILE_N), F32)]
    return pl.pallas_call(
        functools.partial(_ffn_kernel, k=k, final=final, sel=sel, n_side=n_side),
        grid=(n_tok // tm,),
        in_specs=in_specs,
        out_specs=out_specs,
        out_shape=out_shape,
        scratch_shapes=[pltpu.VMEM((tm, d), BF16), pltpu.VMEM((tm, D_FF), BF16),
                        pltpu.VMEM(wg.shape[2:], F32), pltpu.VMEM(wu.shape[2:], F32),
                        pltpu.VMEM(wd.shape[2:], F32),
                        pltpu.SemaphoreType.DMA((2 * N_FF_CHUNKS + 1,))],
        compiler_params=_params(("arbitrary",)),
        name="ffn_half",
    )(*args)


def _rope(z, c_ref, s1_ref, s2_ref):
    fwd = pltpu.roll(z, HEAD_PAD - ROPE_AXIS_DIM // 2, axis=1)
    bwd = pltpu.roll(z, ROPE_AXIS_DIM // 2, axis=1)
    return z * c_ref[...] + fwd * s1_ref[...] + bwd * s2_ref[...]


def _mix_in_kernel(*refs, latent):
    if latent:
        (s_ref, mod_ref, gain_ref, w_in_ref, w_kr_ref, qg_ref, wq_ref, kvg_ref, wk_ref, wv_ref, place_ref,
         c_ref, s1_ref, s2_ref, pool_ref, q_ref, kt_ref, v_ref) = refs
    else:
        (s_ref, mod_ref, gain_ref, w_in_ref, w_kr_ref, kvg_ref, wk_ref, wv_ref, place_ref,
         kt_ref, v_ref) = refs
    u = _adaln(s_ref[...], gain_ref[...], mod_ref, 1).astype(BF16)
    cuts = (POOL_DIM, POOL_DIM + Q_LORA_RANK, POOL_DIM + Q_LORA_RANK + KV_LORA_RANK)
    z = _mm(u, w_in_ref[:, :cuts[2]])
    ckv = (_rms(z[:, cuts[1]:cuts[2]]) * kvg_ref[...]).astype(BF16)
    kr = pltpu.roll(_mm(u, w_kr_ref[...]), QK_NOPE_DIM, axis=1)
    if latent:
        pool_ref[...] = z[:, :cuts[0]]
        cq = (_rms(z[:, cuts[0]:cuts[1]]) * qg_ref[...]).astype(BF16)
        q = jnp.dot(cq, wq_ref[...], preferred_element_type=F32)
        for h in range(MLA_HEADS):
            sl = slice(h * HEAD_PAD, (h + 1) * HEAD_PAD)
            q_ref[:, sl] = _rope(q[:, sl], c_ref, s1_ref, s2_ref).astype(BF16)
        kr = _rope(kr, c_ref, s1_ref, s2_ref)
    kt = lax.dot_general(wk_ref[...], ckv, _NT, preferred_element_type=F32)
    kt = kt + lax.dot_general(place_ref[...], kr.astype(BF16), _NT, preferred_element_type=F32)
    kt_ref[...] = kt.astype(BF16).reshape(kt_ref.shape)
    v_ref[...] = jnp.dot(ckv, wv_ref[...], preferred_element_type=F32).astype(BF16)


def _mix_in(s, mod, gain, w_in, w_kr, kvg, wk, wv, place, tm, latent_args=None):
    n_tok, d = s.shape
    rows_per_mod = n_tok // mod.shape[0]
    assert rows_per_mod % tm == 0
    tiles_per_batch = rows_per_mod // tm
    latent = latent_args is not None
    const = lambda a: pl.BlockSpec(a.shape, lambda i: (0,) * a.ndim)
    tok = lambda w: pl.BlockSpec((tm, w), lambda i: (i, 0))
    in_specs = [tok(d), pl.BlockSpec((1, N_MOD, d), lambda i: (i // tiles_per_batch, 0, 0)),
                pl.BlockSpec((1, d), lambda i: (0, 0)),
                pl.BlockSpec((None,) + w_in.shape[1:], lambda i: (0, 0, 0), pipeline_mode=pl.Buffered(1)),
                const(w_kr)]
    args = [s, mod, gain.reshape(1, d), w_in, w_kr]
    if latent:
        qg, wq, tabs = latent_args
        in_specs += [const(qg), const(wq)]
        args += [qg, wq]
    in_specs += [const(kvg), const(wk), const(wv), const(place)]
    args += [kvg, wk, wv, place]
    if latent:
        kt_spec = pl.BlockSpec((1, QK_PAD_DIM, tm), lambda i: (i // tiles_per_batch, 0, i % tiles_per_batch))
        kt_shape = (mod.shape[0], QK_PAD_DIM, rows_per_mod)
    else:
        kt_spec = pl.BlockSpec((QK_PAD_DIM, tm), lambda i: (0, i))
        kt_shape = (QK_PAD_DIM, n_tok)
    out_specs = [kt_spec, tok(ATTN_V_DIM)]
    out_shape = [jax.ShapeDtypeStruct(kt_shape, BF16), jax.ShapeDtypeStruct((n_tok, ATTN_V_DIM), BF16)]
    if latent:
        rope_spec = pl.BlockSpec((tm, HEAD_PAD), lambda i: (i % tiles_per_batch, 0))
        in_specs += [rope_spec] * 3
        args += list(tabs)
        out_specs = [tok(POOL_DIM), tok(QK_PAD_DIM)] + out_specs
        out_shape = [jax.ShapeDtypeStruct((n_tok, POOL_DIM), F32),
                     jax.ShapeDtypeStruct((n_tok, QK_PAD_DIM), BF16)] + out_shape
    return pl.pallas_call(
        functools.partial(_mix_in_kernel, latent=latent),
        grid=(n_tok // tm,),
        in_specs=in_specs,
        out_specs=out_specs,
        out_shape=out_shape,
        compiler_params=_params(("arbitrary",)),
        name="mix_in_latent" if latent else "mix_in_context",
    )(*args)


def _attn_kernel(q_ref, kth_ref, ktg_ref, vh_ref, vg_ref, o_ref):
    c = ATTN_SCALE * math.log2(math.e)
    lane_head = lax.broadcasted_iota(jnp.int32, (1, PV_WIDTH), 1) // V_HEAD_DIM
    for sub in range(q_ref.shape[1] // ATTN_SUB_Q):
        rows = slice(sub * ATTN_SUB_Q, (sub + 1) * ATTN_SUB_Q)
        for grp in range(MLA_HEADS // PV_HEADS):
            vcols = slice(grp * PV_WIDTH, (grp + 1) * PV_WIDTH)
            acc = None
            for hh in range(PV_HEADS):
                h = grp * PV_HEADS + hh
                qk = slice(h * HEAD_PAD, (h + 1) * HEAD_PAD)
                q = q_ref[0, rows, qk]
                s_h = jnp.dot(q, kth_ref[0, qk, :], preferred_element_type=F32)
                s_g = jnp.dot(q, ktg_ref[qk, :], preferred_element_type=F32)
                m = jnp.maximum(jnp.max(s_h, axis=-1, keepdims=True),
                                jnp.max(s_g, axis=-1, keepdims=True))
                e_h = jnp.exp2((s_h - m) * c)
                e_g = jnp.exp2((s_g - m) * c)
                denom = jnp.sum(e_h, axis=-1, keepdims=True) + jnp.sum(e_g, axis=-1, keepdims=True)
                res = jnp.dot(e_h.astype(BF16), vh_ref[0, :, vcols], preferred_element_type=F32)
                res = res + jnp.dot(e_g.astype(BF16), vg_ref[0, :, vcols], preferred_element_type=F32)
                term = jnp.where(lane_head == hh, res * (1.0 / denom), 0.0)
                acc = term if acc is None else acc + term
            o_ref[0, rows, vcols] = acc.astype(BF16)


def _attention(q, kt_h, kt_g, v_h, v_g, tq):
    b, l, _ = q.shape
    t_g = v_g.shape[1]
    return pl.pallas_call(
        _attn_kernel,
        grid=(b, l // tq),
        in_specs=[
            pl.BlockSpec((1, tq, QK_PAD_DIM), lambda i, j: (i, j, 0)),
            pl.BlockSpec((1, QK_PAD_DIM, l), lambda i, j: (i, 0, 0)),
            pl.BlockSpec((QK_PAD_DIM, t_g), lambda i, j: (0, i)),
            pl.BlockSpec((1, l, ATTN_V_DIM), lambda i, j: (i, 0, 0)),
            pl.BlockSpec((1, t_g, ATTN_V_DIM), lambda i, j: (i, 0, 0)),
        ],
        out_specs=pl.BlockSpec((1, tq, ATTN_V_DIM), lambda i, j: (i, j, 0)),
        out_shape=jax.ShapeDtypeStruct((b, l, ATTN_V_DIM), BF16),
        compiler_params=_params(("arbitrary", "arbitrary")),
        name="latent_attention",
    )(q, kt_h, kt_g, v_h, v_g)


def _mix_out_kernel(h_hbm, mod_ref, pool_ref, poolp_ref, pooln_ref, attn_ref, pw_ref, ps_ref, wo_ref,
                    o_ref, win_scr, y_scr, h_ring, h_sem, *, tm, seq):
    tiles = seq // tm
    step = pl.program_id(0) * tiles + pl.program_id(1)
    n_steps = pl.num_programs(0) * tiles

    def h_copy(s):
        slot = s % H_RING_SLOTS
        return pltpu.make_async_copy(h_hbm.at[s // tiles, pl.ds((s % tiles) * tm, tm), :],
                                     h_ring.at[slot], h_sem.at[slot])

    @pl.when(step == 0)
    def _():
        for s in range(H_RING_SLOTS - 1):
            h_copy(s).start()

    @pl.when(step + (H_RING_SLOTS - 1) < n_steps)
    def _():
        h_copy(step + (H_RING_SLOTS - 1)).start()

    t0 = pl.program_id(1) * tm
    win_scr[0:HALO, :] = jnp.where(t0 > 0, poolp_ref[0], 0.0)
    win_scr[HALO:HALO + tm, :] = pool_ref[0]
    win_scr[HALO + tm:, :] = jnp.where(t0 + tm < seq, pooln_ref[0], 0.0)
    t = (t0 + lax.broadcasted_iota(jnp.int32, (tm, 1), 0)).astype(F32)
    n = tm + 2 * HALO
    for g, w in enumerate(POOL_WINDOWS):
        lanes = slice(g * POOL_GROUP_DIM, (g + 1) * POOL_GROUP_DIM)
        fwd = win_scr[:, lanes]
        span = 1
        while span < w:
            fwd = fwd + pltpu.roll(fwd, n - span, axis=0)
            span *= 2
        total = pltpu.roll(fwd, w // 2, axis=0)[HALO:HALO + tm]
        cnt = jnp.minimum(t, float(w // 2)) + jnp.minimum(float(seq - 1) - t, float(w - w // 2 - 1)) + 1.0
        p = total / cnt - win_scr[HALO:HALO + tm, lanes]
        y = _mm(p.astype(BF16), pw_ref[g])
        y_scr[:, lanes] = (y * ps_ref[:, lanes]).astype(BF16)
    out = _mm(y_scr[...], wo_ref[:POOL_DIM, :]) + _mm(attn_ref[0], wo_ref[POOL_DIM:, :])
    gate = mod_ref[0, 3 * 1 + 2:3 * 1 + 3, :]
    h_copy(step).wait()
    o_ref[0] = h_ring[step % H_RING_SLOTS] + gate * out


def _mix_out(h, mod, pool, attn, pool_w, pool_scale, w_out, tm):
    b, l, d = h.shape
    hb = tm // HALO
    const = lambda a: pl.BlockSpec((None,) + a.shape[1:], lambda i, j: (0,) * a.ndim,
                                   pipeline_mode=pl.Buffered(1))
    return pl.pallas_call(
        functools.partial(_mix_out_kernel, tm=tm, seq=l),
        grid=(b, l // tm),
        in_specs=[
            pl.BlockSpec(memory_space=pl.ANY),
            pl.BlockSpec((1, N_MOD, d), lambda i, j: (i, 0, 0)),
            pl.BlockSpec((1, tm, POOL_DIM), lambda i, j: (i, j, 0)),
            pl.BlockSpec((1, HALO, POOL_DIM), lambda i, j: (i, jnp.maximum(j * hb - 1, 0), 0)),
            pl.BlockSpec((1, HALO, POOL_DIM), lambda i, j: (i, jnp.minimum((j + 1) * hb, l // HALO - 1), 0)),
            pl.BlockSpec((1, tm, ATTN_V_DIM), lambda i, j: (i, j, 0)),
            const(pool_w), const(pool_scale), const(w_out),
        ],
        out_specs=pl.BlockSpec((1, tm, d), lambda i, j: (i, j, 0)),
        out_shape=jax.ShapeDtypeStruct((b, l, d), F32),
        scratch_shapes=[pltpu.VMEM((tm + 2 * HALO, POOL_DIM), F32), pltpu.VMEM((tm, POOL_DIM), BF16),
                        pltpu.VMEM((H_RING_SLOTS, tm, d), F32), pltpu.SemaphoreType.DMA((H_RING_SLOTS,))],
        compiler_params=_params(("arbitrary", "arbitrary")),
        name="mix_out",
    )(h, mod, pool, pool, pool, attn, pool_w, pool_scale, w_out)


def _conv_kernel(h_ref, hp_ref, hn_ref, mod_ref, gain_ref, w_in_ref, cw_ref, w_out_ref, o_ref,
                 u_scr, z_scr, *, tm, seq):
    t0 = pl.program_id(1) * tm
    gain = gain_ref[...]
    u_scr[0:HALO, :] = _adaln(hp_ref[0], gain, mod_ref, 1).astype(BF16)
    u_scr[HALO:HALO + tm, :] = _adaln(h_ref[0], gain, mod_ref, 1).astype(BF16)
    u_scr[HALO + tm:, :] = _adaln(hn_ref[0], gain, mod_ref, 1).astype(BF16)
    d = h_ref.shape[-1]
    cv = _mm(u_scr[...], w_in_ref[:, d:])
    t = t0 - HALO + lax.broadcasted_iota(jnp.int32, (tm + 2 * HALO, 1), 0)
    inside = jnp.logical_and(t >= 0, t < seq)
    z_scr[...] = jnp.where(inside, cv[:, :d] * cv[:, d:], 0.0)
    y = (cw_ref[0:1, :] * z_scr[HALO - 1:HALO - 1 + tm, :]
         + cw_ref[1:2, :] * z_scr[HALO:HALO + tm, :]
         + cw_ref[2:3, :] * z_scr[HALO + 1:HALO + 1 + tm, :])
    bg = _mm(u_scr[HALO:HALO + tm, :], w_in_ref[:, :d])
    out = _mm((bg * y).astype(BF16), w_out_ref[...])
    gate = mod_ref[0, 3 * 1 + 2:3 * 1 + 3, :]
    o_ref[0] = h_ref[0] + gate * out


def _conv_mixer(h, mod, gain, w_in, conv_w, w_out, tm):
    b, l, d = h.shape
    hb = tm // HALO
    const = lambda a: pl.BlockSpec((None,) + a.shape[1:], lambda i, j: (0,) * a.ndim,
                                   pipeline_mode=pl.Buffered(1))
    return pl.pallas_call(
        functools.partial(_conv_kernel, tm=tm, seq=l),
        grid=(b, l // tm),
        in_specs=[
            pl.BlockSpec((1, tm, d), lambda i, j: (i, j, 0)),
            pl.BlockSpec((1, HALO, d), lambda i, j: (i, jnp.maximum(j * hb - 1, 0), 0)),
            pl.BlockSpec((1, HALO, d), lambda i, j: (i, jnp.minimum((j + 1) * hb, l // HALO - 1), 0)),
            pl.BlockSpec((1, N_MOD, d), lambda i, j: (i, 0, 0)),
            pl.BlockSpec((1, d), lambda i, j: (0, 0)),
            const(w_in), const(conv_w), const(w_out),
        ],
        out_specs=pl.BlockSpec((1, tm, d), lambda i, j: (i, j, 0)),
        out_shape=jax.ShapeDtypeStruct((b, l, d), F32),
        scratch_shapes=[pltpu.VMEM((tm + 2 * HALO, d), BF16), pltpu.VMEM((tm + 2 * HALO, d), F32)],
        compiler_params=_params(("arbitrary", "arbitrary")),
        name="conv_mixer",
    )(h, h, h, mod, gain.reshape(1, d), w_in, conv_w, w_out)


def _head_pad(w, per_head, start, width):
    r = w.shape[0]
    w = w.reshape(r, MLA_HEADS, per_head)[:, :, start:start + width]
    return jnp.pad(w, ((0, 0), (0, 0), (0, HEAD_PAD - width))).reshape(r, QK_PAD_DIM)


def _rope_tables(length):
    pos = np.arange(length)
    row = (pos // GRID_W).astype(np.float32)
    col = (pos % GRID_W).astype(np.float32)
    half = ROPE_AXIS_DIM // 2
    freqs = np.power(np.float32(ROPE_THETA),
                     -np.arange(0, ROPE_AXIS_DIM, 2, dtype=np.float32) / np.float32(ROPE_AXIS_DIM))
    lane = np.arange(HEAD_PAD)
    o = lane - QK_NOPE_DIM
    rotary = np.logical_and(o >= 0, o < QK_ROPE_DIM)
    o = np.clip(o, 0, QK_ROPE_DIM - 1)
    ang = np.where((o // ROPE_AXIS_DIM == 0)[None, :], row[:, None], col[:, None]) * freqs[o % half][None, :]
    ang = ang.astype(np.float32)
    first = (o % ROPE_AXIS_DIM) < half
    cos = np.where(rotary[None, :], np.cos(ang), 1.0).astype(np.float32)
    sin = np.where(rotary[None, :], np.sin(ang), 0.0).astype(np.float32)
    return (jnp.asarray(cos), jnp.asarray(np.where(first[None, :], -sin, 0.0).astype(np.float32)),
            jnp.asarray(np.where(first[None, :], 0.0, sin).astype(np.float32)))


def kernel(x, c, ctx, c_ctx, norm_g, w_mod, b_mod, ffn_w_gate, ffn_w_up, ffn_w_down, ab_w_in, pool_w,
           pool_scale, q_norm_g, w_uq, kv_norm_g, w_ukv, ab_w_out, conv_w_in, conv_w, conv_w_out,
           final_norm_g):
    b, l, d = x.shape
    t_g = ctx.shape[1]
    tm_h, tq, tm_mix = FFN_TILE_ROWS, ATTN_TILE_ROWS, MIX_TILE_ROWS
    assert w_mod.shape[0] == 2 and ab_w_in.shape[0] == 1 and conv_w_in.shape[0] == 1
    assert l % tm_h == 0 and l % tq == 0 and l % tm_mix == 0 and t_g % LANES == 0 and l % GRID_W == 0

    cond = jnp.zeros((MOD_ROWS, d), F32).at[:b].set(c).at[b].set(c_ctx)
    n_first = 3 * d
    m_first = _modulation(cond, w_mod, b_mod, n_first).reshape(MOD_ROWS, 3, d)

    ffn_w = (ffn_w_gate, ffn_w_up, ffn_w_down)

    h = x.reshape(b * l, d)
    g = ctx.reshape(b * t_g, d)

    h, m_rest = _ffn_half(h, m_first[:b], norm_g[0, 0], *ffn_w, (0, 0), k=0, tm=tm_h,
                          side_mod=(cond, w_mod, b_mod, n_first))
    g = _ffn_half(g, m_first[b:b + 1], norm_g[0, 0], *ffn_w, (0, 0), k=0, tm=tm_h)
    m_rest = m_rest.reshape(MOD_ROWS, 2 * N_MOD - 3, d)
    m0 = jnp.concatenate([m_first, m_rest[:, :N_MOD - 3]], axis=1)
    mod_h = [m0[:b], m_rest[:b, N_MOD - 3:]]

    kr_start = POOL_DIM + Q_LORA_RANK + KV_LORA_RANK
    w_kr = jnp.pad(ab_w_in[0, :, kr_start:], ((0, 0), (0, HEAD_PAD - QK_ROPE_DIM)))
    wq = _head_pad(w_uq[0], QK_HEAD_DIM, 0, QK_HEAD_DIM).astype(BF16)
    wk = _head_pad(w_ukv[0], QK_NOPE_DIM + V_HEAD_DIM, 0, QK_NOPE_DIM).T.astype(BF16)
    wv = w_ukv[0].reshape(KV_LORA_RANK, MLA_HEADS, QK_NOPE_DIM + V_HEAD_DIM)[:, :, QK_NOPE_DIM:]
    wv = wv.reshape(KV_LORA_RANK, ATTN_V_DIM).astype(BF16)
    lane = jnp.arange(HEAD_PAD)
    rotary = jnp.logical_and(lane >= QK_NOPE_DIM, lane < QK_HEAD_DIM)
    place = jnp.tile(jnp.where(rotary[:, None], jnp.eye(HEAD_PAD, dtype=F32), 0.0), (MLA_HEADS, 1)).astype(BF16)
    qg = q_norm_g[0].reshape(1, Q_LORA_RANK)
    kvg = kv_norm_g[0].reshape(1, KV_LORA_RANK)
    tabs = _rope_tables(l)

    pool, q, k_h, v_h = _mix_in(h, mod_h[0], norm_g[0, 1], ab_w_in, w_kr, kvg, wk, wv, place, tm_mix,
                                latent_args=(qg, wq, tabs))
    k_g, v_g = _mix_in(g, m0[b:b + 1], norm_g[0, 1], ab_w_in, w_kr, kvg, wk, wv, place, tm_mix)
    attn = _attention(q.reshape(b, l, -1), k_h, k_g, v_h.reshape(b, l, -1), v_g.reshape(b, t_g, -1), tq)
    h = _mix_out(h.reshape(b, l, d), mod_h[0], pool.reshape(b, l, -1), attn, pool_w,
                 pool_scale.reshape(-1, 1, POOL_DIM), ab_w_out, tm_mix)
    h = _ffn_half(h.reshape(b * l, d), mod_h[0], norm_g[0, 2], *ffn_w, (0, 1), k=2, tm=tm_h)

    h = _ffn_half(h, mod_h[1], norm_g[1, 0], *ffn_w, (1, 0), k=0, tm=tm_h)
    h = _conv_mixer(h.reshape(b, l, d), mod_h[1], norm_g[1, 1], conv_w_in, conv_w, conv_w_out, tm_mix)
    h = _ffn_half(h.reshape(b * l, d), mod_h[1], norm_g[1, 2], *ffn_w, (1, 1), k=2, tm=tm_h,
                  final_gain=final_norm_g)
    return h.reshape(b, l, d)
```
